```python
import math
import jax, jax.numpy as jnp
from jax import lax
import numpy as np

D_MODEL = 1024
BATCH = 1
SEQ = 16384
DEPTH = 1
DEC_BATCH = 32
DEC_SEQ = 1
PAST_LEN = 16384
PAGE_SIZE = 128

A_HEADS = 8
A_KV_HEADS = 2
A_GROUP = A_HEADS // A_KV_HEADS
HEAD_DIM = 64
CMP_STRIDE = 16
CMP_BLOCK = 2 * CMP_STRIDE
SLC_BLOCK = 64
RATIO = SLC_BLOCK // CMP_STRIDE
SLC_TOPN = 16
WINDOW = 512
Q_BLOCK = 128
N_KV_SLOTS = 4
B_HEADS = 4
B_DK = 128
B_DV = 128
CONV_W = 4
DELTA_CHUNK = 64
N_EXPERTS = 64
TOP_K = 6
D_EXPERT = 128
D_SHARED = 128
ROUTED_SCALE = 2.5
MOE_BLOCK = 128
EPS = 1e-6

A_WIDTH = A_HEADS * HEAD_DIM
B_WIDTH = B_HEADS * B_DV
CONV_CH = 2 * B_HEADS * B_DK + B_HEADS * B_DV
IN_SPLITS = (A_WIDTH, 6 * A_KV_HEADS * HEAD_DIM, 3 * A_HEADS, CONV_CH, B_HEADS, B_HEADS, B_WIDTH, 2 * D_MODEL)
IN_WIDTH = sum(IN_SPLITS)
IN_OFFSETS = tuple(int(v) for v in np.cumsum(IN_SPLITS)[:-1])
F32 = jnp.float32
NEG = -1e30
BIG = 1e30

kernel_name = 'hybrid_nsa_gdn_moe_adaln_step'


def rmsnorm(x, g):
    xf = x.astype(F32)
    y = xf * lax.rsqrt(jnp.mean(xf * xf, axis=-1, keepdims=True) + EPS)
    return (y * g.astype(F32)).astype(x.dtype)


def l2norm(x):
    return x * lax.rsqrt(jnp.sum(x * x, axis=-1, keepdims=True) + EPS)


def alibi_slopes():
    h = jnp.arange(1, A_HEADS + 1, dtype=F32)
    return jnp.exp2(-8.0 * h / A_HEADS)


def adaln(c, w_ada, b_ada):
    mod = jnp.einsum('bd,de->be', jax.nn.silu(c), w_ada) + b_ada
    return jnp.split(mod[:, None, :], 6, axis=-1)


def modulate(x, g, shift, scale):
    return rmsnorm(x, g) * (1 + scale) + shift


def project(h, w_in):
    z = jnp.einsum('btd,de->bte', h, w_in)
    return jnp.split(z, IN_OFFSETS, axis=-1)


def nsa_query(q_a, q_norm_g):
    B, T = q_a.shape[:2]
    return rmsnorm(q_a.reshape(B, T, A_HEADS, HEAD_DIM), q_norm_g) * HEAD_DIM ** -0.5


def compress(rows, w, pe):
    B, L, K, D = rows.shape
    r = rows.reshape(B, L // CMP_STRIDE, CMP_STRIDE, K, D)
    first = jnp.einsum('bnjkd,jde->bnke', r, w[:CMP_STRIDE])
    second = jnp.einsum('bnjkd,jde->bnke', r, w[CMP_STRIDE:])
    bias = jnp.einsum('jd,jde->e', pe, w)
    return first[:, :-1] + second[:, 1:] + bias


def nsa_keys(rows, k_norm_g, w_cmp, pe_cmp):
    B, L = rows.shape[:2]
    nsb = L // SLC_BLOCK
    kc = rmsnorm(compress(rows[:, :, 0], w_cmp[0], pe_cmp[0]), k_norm_g[0])
    vc = compress(rows[:, :, 1], w_cmp[1], pe_cmp[1])
    cpos = jnp.arange(kc.shape[1], dtype=jnp.int32) * CMP_STRIDE + (CMP_BLOCK - 1)

    def blocks(r):
        return r.reshape(B, nsb, SLC_BLOCK, A_KV_HEADS, HEAD_DIM).transpose(0, 3, 1, 2, 4)

    ks = blocks(rmsnorm(rows[:, :, 2], k_norm_g[1]))
    vs = blocks(rows[:, :, 3])
    return kc, vc, cpos, ks, vs


def nsa_attend(q, qpos, kc, vc, cpos, ks, vs, kw, vw, wpos, gates):
    B, Q = q.shape[:2]
    qg = q.reshape(B, Q, A_KV_HEADS, A_GROUP, HEAD_DIM)
    sl = alibi_slopes().reshape(A_KV_HEADS, A_GROUP)[None, :, :, None, None]
    dist_c = qpos[:, None] - cpos[None, :]
    valid_c = dist_c >= 0
    s_c = jnp.einsum('bqkgd,bnkd->bkgqn', qg, kc).astype(F32)
    s_c = jnp.where(valid_c, s_c - sl * dist_c.astype(F32), NEG)
    p_c = jax.nn.softmax(s_c, axis=-1) * jnp.any(valid_c, axis=-1)[:, None].astype(F32)
    o_c = jnp.einsum('bkgqn,bnkd->bqkgd', p_c.astype(vc.dtype), vc)
    nsb = ks.shape[2]
    nc = kc.shape[1]
    pg = jnp.pad(p_c.sum(axis=2), ((0, 0), (0, 0), (0, 0), (0, RATIO * nsb - nc)))
    pg = pg.reshape(B, A_KV_HEADS, Q, nsb, RATIO)
    imp = pg.sum(-1) + jnp.pad(pg[..., :-1, RATIO - 1], ((0, 0), (0, 0), (0, 0), (1, 0)))
    jidx = jnp.arange(nsb, dtype=jnp.int32)
    cur = qpos // SLC_BLOCK
    forced = (jidx[None, :] == cur[:, None]) | (jidx[None, :] == 0)
    avail = jidx[None, :] <= cur[:, None]
    imp = jnp.where(forced, BIG, jnp.where(avail, imp, NEG))
    _, sel = lax.top_k(imp, min(SLC_TOPN, nsb))
    bi = jnp.arange(B)[:, None, None, None]
    ki = jnp.arange(A_KV_HEADS)[None, :, None, None]
    kg = ks[bi, ki, sel]
    vg = vs[bi, ki, sel]
    spos = sel[..., None] * SLC_BLOCK + jnp.arange(SLC_BLOCK, dtype=jnp.int32)
    dist_s = (qpos[:, None, None] - spos)[:, :, None]
    s_s = jnp.einsum('bqkgd,bkqnsd->bkgqns', qg, kg).astype(F32)
    s_s = jnp.where(dist_s >= 0, s_s - sl[..., None] * dist_s.astype(F32), NEG)
    shp = s_s.shape
    p_s = jax.nn.softmax(s_s.reshape(shp[:4] + (-1,)), axis=-1).reshape(shp)
    o_s = jnp.einsum('bkgqns,bkqnsd->bqkgd', p_s.astype(vg.dtype), vg)
    dist_w = qpos[:, None] - wpos[None, :]
    valid_w = (dist_w >= 0) & (dist_w <= WINDOW) & (wpos >= 0)[None, :]
    s_w = jnp.einsum('bqkgd,bnkd->bkgqn', qg, kw).astype(F32)
    s_w = jnp.where(valid_w, s_w - sl * dist_w.astype(F32), NEG)
    p_w = jax.nn.softmax(s_w, axis=-1)
    o_w = jnp.einsum('bkgqn,bnkd->bqkgd', p_w.astype(vw.dtype), vw)
    gr = gates.reshape(B, Q, A_KV_HEADS, A_GROUP, 3)
    o = gr[..., 0:1] * o_c + gr[..., 1:2] * o_s + gr[..., 2:3] * o_w
    return o.reshape(B, Q, A_WIDTH)


def nsa_prompt(q, gates, kv_rows, win_rows, lp):
    B, T = q.shape[:2]
    kc, vc, cpos, ks, vs = nsa_keys(kv_rows, lp['k_norm_g'], lp['w_cmp'], lp['pe_cmp'])
    kw = rmsnorm(win_rows[:, :, 0], lp['k_norm_g'][2])
    pad = ((0, 0), (WINDOW, 0), (0, 0), (0, 0))
    kw_pad = jnp.pad(kw, pad)
    vw_pad = jnp.pad(win_rows[:, :, 1], pad)

    def block(i):
        q0 = i * Q_BLOCK
        qb = lax.dynamic_slice_in_dim(q, q0, Q_BLOCK, axis=1)
        gb = lax.dynamic_slice_in_dim(gates, q0, Q_BLOCK, axis=1)
        qpos = q0 + jnp.arange(Q_BLOCK, dtype=jnp.int32)
        kwb = lax.dynamic_slice_in_dim(kw_pad, q0, WINDOW + Q_BLOCK, axis=1)
        vwb = lax.dynamic_slice_in_dim(vw_pad, q0, WINDOW + Q_BLOCK, axis=1)
        wpos = q0 - WINDOW + jnp.arange(WINDOW + Q_BLOCK, dtype=jnp.int32)
        return nsa_attend(qb, qpos, kc, vc, cpos, ks, vs, kwb, vwb, wpos, gb)

    o = lax.map(block, jnp.arange(T // Q_BLOCK, dtype=jnp.int32))
    return o.transpose(1, 0, 2, 3).reshape(B, T, A_WIDTH)


def nsa_sample(q, gates, kv_new, win_new, cache_kv, page_table, cache_win, lp):
    DB, S = q.shape[:2]
    P = page_table.shape[1] * cache_kv.shape[1]
    past = cache_kv[page_table].reshape(DB, P, N_KV_SLOTS, A_KV_HEADS, HEAD_DIM)
    rows = jnp.concatenate([past.astype(kv_new.dtype), kv_new], axis=1)
    L = P + S
    Lp = -(-L // SLC_BLOCK) * SLC_BLOCK
    rows = jnp.pad(rows, ((0, 0), (0, Lp - L), (0, 0), (0, 0), (0, 0)))
    kc, vc, cpos, ks, vs = nsa_keys(rows, lp['k_norm_g'], lp['w_cmp'], lp['pe_cmp'])
    nwb = cache_win.shape[1]
    win = jnp.concatenate([cache_win.astype(win_new.dtype), win_new], axis=1)
    wpos = P - nwb + jnp.arange(nwb + S, dtype=jnp.int32)
    kw = rmsnorm(win[:, :, 0], lp['k_norm_g'][2])
    qpos = P + jnp.arange(S, dtype=jnp.int32)
    o = nsa_attend(q, qpos, kc, vc, cpos, ks, vs, kw, win[:, :, 1], wpos, gates)
    return o, win[:, S:]


def causal_conv(xpad, w):
    T = xpad.shape[1] - (CONV_W - 1)
    acc = xpad[:, 0:T] * w[0]
    for j in range(1, CONV_W):
        acc = acc + xpad[:, j:j + T] * w[j]
    return jax.nn.silu(acc)


def delta_inputs(qkv, a_b, b_b, a_log, dt_bias):
    B, T, _ = qkv.shape
    qkv = qkv.astype(F32)
    nk = B_HEADS * B_DK
    q = l2norm(qkv[..., :nk].reshape(B, T, B_HEADS, B_DK)) * B_DK ** -0.5
    k = l2norm(qkv[..., nk:2 * nk].reshape(B, T, B_HEADS, B_DK))
    v = qkv[..., 2 * nk:].reshape(B, T, B_HEADS, B_DV)
    g = -jnp.exp(a_log.astype(F32)) * jax.nn.softplus(a_b.astype(F32) + dt_bias.astype(F32))
    beta = jax.nn.sigmoid(b_b.astype(F32))
    return q, k, v, g, beta


def delta_chunked(q, k, v, g, beta):
    B, T, H, _ = q.shape
    C = DELTA_CHUNK
    N = T // C

    def chunks(x):
        return jnp.moveaxis(x.reshape((B, N, C) + x.shape[2:]), 2, 3)

    qc, kc, vc, gc, bc = chunks(q), chunks(k), chunks(v), chunks(g), chunks(beta)
    G = jnp.cumsum(gc, axis=-1)
    i = jnp.arange(C)
    incl = i[:, None] >= i[None, :]
    strict = i[:, None] > i[None, :]
    decay = jnp.exp(jnp.where(incl, G[..., :, None] - G[..., None, :], -jnp.inf))
    kk = jnp.einsum('bnhid,bnhjd->bnhij', kc, kc)
    a_mat = jnp.where(strict, kk * decay * bc[..., :, None], 0.0) + jnp.eye(C, dtype=F32)
    eg = jnp.exp(G)[..., None]
    rhs = jnp.concatenate([vc * bc[..., None], kc * bc[..., None] * eg], axis=-1)
    sol = lax.linalg.triangular_solve(a_mat, rhs, left_side=True, lower=True, unit_diagonal=True)
    u_c, w_c = sol[..., :B_DV], sol[..., B_DV:]
    qk = jnp.einsum('bnhid,bnhjd->bnhij', qc, kc) * decay
    q_dec = qc * eg
    k_dec = kc * jnp.exp(G[..., -1:] - G)[..., None]
    g_last = jnp.exp(G[..., -1])

    def step(s, inp):
        u_i, w_i, qk_i, qd_i, kd_i, gl_i = inp
        v_new = u_i - jnp.einsum('bhck,bhkv->bhcv', w_i, s)
        o = jnp.einsum('bhck,bhkv->bhcv', qd_i, s) + jnp.einsum('bhij,bhjv->bhiv', qk_i, v_new)
        s = s * gl_i[..., None, None] + jnp.einsum('bhck,bhcv->bhkv', kd_i, v_new)
        return s, o

    xs = (jnp.moveaxis(u_c, 1, 0), jnp.moveaxis(w_c, 1, 0), jnp.moveaxis(qk, 1, 0),
          jnp.moveaxis(q_dec, 1, 0), jnp.moveaxis(k_dec, 1, 0), jnp.moveaxis(g_last, 1, 0))
    s0 = jnp.zeros((B, H, B_DK, B_DV), F32)
    s_fin, o = lax.scan(step, s0, xs)
    o = jnp.moveaxis(jnp.moveaxis(o, 0, 1), 3, 2).reshape(B, T, H, B_DV)
    return o, s_fin


def delta_recurrent(s0, q, k, v, g, beta):
    def step(s, inp):
        q_t, k_t, v_t, g_t, b_t = inp
        s = s * jnp.exp(g_t)[..., None, None]
        u = b_t[..., None] * (v_t - jnp.einsum('bhk,bhkv->bhv', k_t, s))
        s = s + jnp.einsum('bhk,bhv->bhkv', k_t, u)
        return s, jnp.einsum('bhk,bhkv->bhv', q_t, s)

    xs = (jnp.moveaxis(q, 1, 0), jnp.moveaxis(k, 1, 0), jnp.moveaxis(v, 1, 0),
          jnp.moveaxis(g, 1, 0), jnp.moveaxis(beta, 1, 0))
    s_fin, o = lax.scan(step, s0.astype(F32), xs)
    return jnp.moveaxis(o, 0, 1), s_fin


def delta_output(o, gate_b, o_norm_g):
    B, T = o.shape[:2]
    gt = gate_b.reshape(B, T, B_HEADS, B_DV).astype(F32)
    y = rmsnorm(o, o_norm_g) * jax.nn.silu(gt)
    return y.reshape(B, T, B_WIDTH).astype(gate_b.dtype)


def merge_out(o_a, o_b, merge, lp):
    m_a, m_b = jnp.split(merge, 2, axis=-1)
    m = jax.nn.sigmoid(m_a) * (o_a @ lp['w_proj_a']) + jax.nn.sigmoid(m_b) * (o_b @ lp['w_proj_b'])
    return m @ lp['w_out']


def moe_block(h, lp):
    scores = jax.nn.sigmoid(jnp.einsum('nd,de->ne', h, lp['w_router']).astype(F32))
    _, idx = lax.top_k(scores + lp['b_router'].astype(F32), TOP_K)
    sel = jnp.take_along_axis(scores, idx, axis=-1)
    wts = sel / jnp.sum(sel, axis=-1, keepdims=True) * ROUTED_SCALE
    gate = jnp.sum(jax.nn.one_hot(idx, N_EXPERTS, dtype=F32) * wts[..., None], axis=-2)
    a, u = jnp.split(jnp.einsum('nd,edf->nef', h, lp['w_exp_gu']), 2, axis=-1)
    act = jax.nn.silu(a) * u * gate[..., None].astype(h.dtype)
    y = jnp.einsum('nef,efd->nd', act, lp['w_exp_down'])
    sa, su = jnp.split(h @ lp['w_sh_gu'], 2, axis=-1)
    return y + (jax.nn.silu(sa) * su) @ lp['w_sh_down']


def moe_apply(h, lp):
    B, T, D = h.shape
    n = B * T
    flat = h.reshape(n, D)
    if n % MOE_BLOCK == 0 and n > MOE_BLOCK:
        y = lax.map(lambda hb: moe_block(hb, lp), flat.reshape(n // MOE_BLOCK, MOE_BLOCK, D)).reshape(n, D)
    else:
        y = moe_block(flat, lp)
    return y.reshape(B, T, D)


def split_mixer_inputs(h, lp):
    B, T, _ = h.shape
    q_a, kv_a, g_a, qkv_b, a_b, b_b, gate_b, merge = project(h, lp['w_in'])
    q = nsa_query(q_a, lp['q_norm_g'])
    gates = jax.nn.sigmoid(g_a).reshape(B, T, A_HEADS, 3)
    kv = kv_a.reshape(B, T, 6, A_KV_HEADS, HEAD_DIM)
    return q, gates, kv[:, :, :N_KV_SLOTS], kv[:, :, N_KV_SLOTS:], qkv_b, a_b, b_b, gate_b, merge


def layer_prompt(x, c, lp):
    sh1, sc1, gt1, sh2, sc2, gt2 = adaln(c, lp['w_ada'], lp['b_ada'])
    h = modulate(x, lp['norm1_g'], sh1, sc1)
    T = h.shape[1]
    q, gates, kv_rows, win_rows, qkv_b, a_b, b_b, gate_b, merge = split_mixer_inputs(h, lp)
    o_a = nsa_prompt(q, gates, kv_rows, win_rows, lp)
    qkv_pad = jnp.pad(qkv_b, ((0, 0), (CONV_W - 1, 0), (0, 0)))
    qd, kd, vd, g, beta = delta_inputs(causal_conv(qkv_pad, lp['conv_w']), a_b, b_b, lp['a_log'], lp['dt_bias'])
    o_d, s_fin = delta_chunked(qd, kd, vd, g, beta)
    o_b = delta_output(o_d, gate_b, lp['o_norm_g'])
    x = x + gt1 * merge_out(o_a, o_b, merge, lp)
    x = x + gt2 * moe_apply(modulate(x, lp['norm2_g'], sh2, sc2), lp)
    return x, kv_rows, win_rows[:, -min(WINDOW, T):], qkv_pad[:, T:], s_fin.astype(x.dtype)


def layer_sample(x, c, cache_kv, page_table, cache_win, conv_state, delta_state, lp):
    sh1, sc1, gt1, sh2, sc2, gt2 = adaln(c, lp['w_ada'], lp['b_ada'])
    h = modulate(x, lp['norm1_g'], sh1, sc1)
    S = h.shape[1]
    q, gates, kv_rows, win_rows, qkv_b, a_b, b_b, gate_b, merge = split_mixer_inputs(h, lp)
    o_a, new_win = nsa_sample(q, gates, kv_rows, win_rows, cache_kv, page_table, cache_win, lp)
    qkv_pad = jnp.concatenate([conv_state.astype(qkv_b.dtype), qkv_b], axis=1)
    qd, kd, vd, g, beta = delta_inputs(causal_conv(qkv_pad, lp['conv_w']), a_b, b_b, lp['a_log'], lp['dt_bias'])
    o_d, s_fin = delta_recurrent(delta_state, qd, kd, vd, g, beta)
    o_b = delta_output(o_d, gate_b, lp['o_norm_g'])
    x = x + gt1 * merge_out(o_a, o_b, merge, lp)
    x = x + gt2 * moe_apply(modulate(x, lp['norm2_g'], sh2, sc2), lp)
    return x, kv_rows, new_win, qkv_pad[:, S:], s_fin.astype(x.dtype)


def setup_inputs(seed: int = 0) -> dict:
    key = jax.random.key(seed)
    ks = jax.random.split(key, 32)
    n_pages = PAST_LEN // PAGE_SIZE
    n_used = DEC_BATCH * n_pages
    n_phys = n_used + max(1, n_used // 4)
    win_buf = min(WINDOW, PAST_LEN)

    def nrm(k, shape, s):
        return jax.random.normal(k, shape, F32) * s

    page_table = jax.random.permutation(ks[3], n_phys)[:n_used].reshape(DEC_BATCH, n_pages).astype(jnp.int32)
    a_log = jnp.log(jax.random.uniform(ks[19], (DEPTH, B_HEADS), F32, 1.0, 16.0))
    dt = jnp.exp(jax.random.uniform(ks[20], (DEPTH, B_HEADS), F32, math.log(1e-3), math.log(1e-1)))
    dt_bias = dt + jnp.log(-jnp.expm1(-dt))
    return {
        'x_prompt': nrm(ks[0], (BATCH, SEQ, D_MODEL), 1.0),
        'x_sample': nrm(ks[1], (DEC_BATCH, DEC_SEQ, D_MODEL), 1.0),
        'cache_nsa_kv': nrm(ks[2], (DEPTH, n_phys, PAGE_SIZE, N_KV_SLOTS, A_KV_HEADS, HEAD_DIM), 1.0),
        'page_table': page_table,
        'cache_win_kv': nrm(ks[4], (DEPTH, DEC_BATCH, win_buf, 2, A_KV_HEADS, HEAD_DIM), 1.0),
        'state_conv': nrm(ks[5], (DEPTH, DEC_BATCH, CONV_W - 1, CONV_CH), 1.0),
        'state_delta': nrm(ks[6], (DEPTH, DEC_BATCH, B_HEADS, B_DK, B_DV), 0.1),
        'c_prompt': nrm(ks[7], (BATCH, D_MODEL), 1.0),
        'c_sample': nrm(ks[8], (DEC_BATCH, D_MODEL), 1.0),
        'norm1_g': 1.0 + nrm(ks[9], (DEPTH, D_MODEL), 0.05),
        'norm2_g': 1.0 + nrm(ks[10], (DEPTH, D_MODEL), 0.05),
        'w_ada': nrm(ks[11], (DEPTH, D_MODEL, 6 * D_MODEL), 0.5 * D_MODEL ** -0.5),
        'b_ada': nrm(ks[12], (DEPTH, 6 * D_MODEL), 0.02),
        'w_in': nrm(ks[13], (DEPTH, D_MODEL, IN_WIDTH), D_MODEL ** -0.5),
        'q_norm_g': 1.0 + nrm(ks[14], (DEPTH, HEAD_DIM), 0.05),
        'k_norm_g': 1.0 + nrm(ks[15], (DEPTH, 3, HEAD_DIM), 0.05),
        'w_cmp': nrm(ks[16], (DEPTH, 2, CMP_BLOCK, HEAD_DIM, HEAD_DIM), (CMP_BLOCK * HEAD_DIM) ** -0.5),
        'pe_cmp': nrm(ks[17], (DEPTH, 2, CMP_BLOCK, HEAD_DIM), 0.1),
        'conv_w': nrm(ks[18], (DEPTH, CONV_W, CONV_CH), CONV_W ** -0.5),
        'a_log': a_log,
        'dt_bias': dt_bias,
        'o_norm_g': 1.0 + nrm(ks[21], (DEPTH, B_DV), 0.05),
        'w_proj_a': nrm(ks[22], (DEPTH, A_WIDTH, D_MODEL), A_WIDTH ** -0.5),
        'w_proj_b': nrm(ks[23], (DEPTH, B_WIDTH, D_MODEL), B_WIDTH ** -0.5),
        'w_out': nrm(ks[24], (DEPTH, D_MODEL, D_MODEL), D_MODEL ** -0.5),
        'w_router': nrm(ks[25], (DEPTH, D_MODEL, N_EXPERTS), D_MODEL ** -0.5),
        'b_router': nrm(ks[26], (DEPTH, N_EXPERTS), 0.01),
        'w_exp_gu': nrm(ks[27], (DEPTH, N_EXPERTS, D_MODEL, 2 * D_EXPERT), D_MODEL ** -0.5),
        'w_exp_down': nrm(ks[28], (DEPTH, N_EXPERTS, D_EXPERT, D_MODEL), D_EXPERT ** -0.5),
        'w_sh_gu': nrm(ks[29], (DEPTH, D_MODEL, 2 * D_SHARED), D_MODEL ** -0.5),
        'w_sh_down': nrm(ks[30], (DEPTH, D_SHARED, D_MODEL), D_SHARED ** -0.5),
    }


def reference(x_prompt, x_sample, cache_nsa_kv, page_table, cache_win_kv, state_conv, state_delta,
              c_prompt, c_sample, norm1_g, norm2_g, w_ada, b_ada, w_in, q_norm_g, k_norm_g, w_cmp, pe_cmp,
              conv_w, a_log, dt_bias, o_norm_g, w_proj_a, w_proj_b, w_out, w_router, b_router,
              w_exp_gu, w_exp_down, w_sh_gu, w_sh_down):
    xp, xs = x_prompt, x_sample
    kvp_l, winp_l, convp_l, dp_l = [], [], [], []
    kvs_l, wins_l, convs_l, ds_l = [], [], [], []
    for l in range(DEPTH):
        lp = {
            'norm1_g': norm1_g[l], 'norm2_g': norm2_g[l], 'w_ada': w_ada[l], 'b_ada': b_ada[l],
            'w_in': w_in[l], 'q_norm_g': q_norm_g[l], 'k_norm_g': k_norm_g[l], 'w_cmp': w_cmp[l],
            'pe_cmp': pe_cmp[l], 'conv_w': conv_w[l], 'a_log': a_log[l], 'dt_bias': dt_bias[l],
            'o_norm_g': o_norm_g[l], 'w_proj_a': w_proj_a[l], 'w_proj_b': w_proj_b[l], 'w_out': w_out[l],
            'w_router': w_router[l], 'b_router': b_router[l], 'w_exp_gu': w_exp_gu[l],
            'w_exp_down': w_exp_down[l], 'w_sh_gu': w_sh_gu[l], 'w_sh_down': w_sh_down[l],
        }
        xp, kvp, winp, convp, dp = layer_prompt(xp, c_prompt, lp)
        xs, kvs, wins, convs, ds = layer_sample(xs, c_sample, cache_nsa_kv[l], page_table, cache_win_kv[l],
                                                state_conv[l], state_delta[l], lp)
        kvp_l.append(kvp); winp_l.append(winp); convp_l.append(convp); dp_l.append(dp)
        kvs_l.append(kvs); wins_l.append(wins); convs_l.append(convs); ds_l.append(ds)
    kv_rows_prompt = jnp.stack(kvp_l)
    win_prompt = jnp.stack(winp_l)
    conv_prompt = jnp.stack(convp_l)
    delta_prompt = jnp.stack(dp_l)
    kv_rows_sample = jnp.stack(kvs_l)
    win_sample = jnp.stack(wins_l)
    conv_sample = jnp.stack(convs_l)
    delta_sample = jnp.stack(ds_l)
    return (xp, xs, kv_rows_prompt, win_prompt, conv_prompt, delta_prompt,
            kv_rows_sample, win_sample, conv_sample, delta_sample)
```

```python
import functools
import math

import jax
import jax.numpy as jnp
import numpy as np
from jax import lax
from jax.experimental import pallas as pl
from jax.experimental.pallas import tpu as pltpu

F32 = jnp.float32
BF16 = jnp.bfloat16
HIGHEST = lax.Precision.HIGHEST

D_MODEL = 1024
A_HEADS = 8
A_KV_HEADS = 2
A_GROUP = A_HEADS // A_KV_HEADS
HEAD_DIM = 64
CMP_STRIDE = 16
CMP_BLOCK = 32
SLC_BLOCK = 64
RATIO = SLC_BLOCK // CMP_STRIDE
SLC_TOPN = 16
WINDOW = 512
Q_BLOCK = 128
N_KV_SLOTS = 4
B_HEADS = 4
B_DK = 128
B_DV = 128
CONV_W = 4
DELTA_CHUNK = 64
N_EXPERTS = 64
TOP_K = 6
D_EXPERT = 128
D_SHARED = 128
ROUTED_SCALE = 2.5
EPS = 1e-6
NEG = -1e30
BIG = 1e30

A_WIDTH = A_HEADS * HEAD_DIM
B_WIDTH = B_HEADS * B_DV
CONV_CH = 2 * B_HEADS * B_DK + B_HEADS * B_DV
KV_WIDTH = 6 * A_KV_HEADS * HEAD_DIM
IN_SPLITS = (A_WIDTH, KV_WIDTH, 3 * A_HEADS, CONV_CH, B_HEADS, B_HEADS, B_WIDTH, 2 * D_MODEL)
IN_OFFSETS = tuple(int(v) for v in np.cumsum(IN_SPLITS)[:-1])

LANES = 128
SMALL_W = LANES
C_Q = 0
C_KV = C_Q + A_WIDTH
C_QKVB = C_KV + KV_WIDTH
C_GATEB = C_QKVB + CONV_CH
C_MERGE = C_GATEB + B_WIDTH
C_SMALL = C_MERGE + 2 * D_MODEL
W_MAIN = C_SMALL + SMALL_W

VMEM_LIMIT = 56 * 1024 * 1024


def _cparams(*sem):
    return pltpu.CompilerParams(dimension_semantics=sem, vmem_limit_bytes=VMEM_LIMIT)


def _const_spec(shape):
    nd = len(shape)
    return pl.BlockSpec(shape, lambda *_: (0,) * nd)


def _row_spec(tm, width, rows):
    if rows == 1:
        return pl.BlockSpec((1, width), lambda i: (0, 0))
    return pl.BlockSpec((tm, width), lambda i: (i, 0))


def _silu(x):
    return x * jax.nn.sigmoid(x)


def _rms_rows(x):
    return x * lax.rsqrt(jnp.mean(x * x, axis=-1, keepdims=True) + EPS)


def _ada_kernel(c_ref, w_ref, b_ref, o_ref):
    s = _silu(c_ref[...]).astype(BF16)
    o_ref[...] = jnp.dot(s, w_ref[...].astype(BF16), preferred_element_type=F32) + b_ref[...]


def _adaln(c_all, w_ada, b_ada):
    rows = c_all.shape[0]
    tn = 1024
    return pl.pallas_call(
        _ada_kernel,
        grid=(6 * D_MODEL // tn,),
        in_specs=[pl.BlockSpec((rows, D_MODEL), lambda j: (0, 0)),
                  pl.BlockSpec((D_MODEL, tn), lambda j: (0, j)),
                  pl.BlockSpec((1, tn), lambda j: (0, j))],
        out_specs=pl.BlockSpec((rows, tn), lambda j: (0, j)),
        out_shape=jax.ShapeDtypeStruct((rows, 6 * D_MODEL), F32),
        compiler_params=_cparams("arbitrary"),
        name="adaln",
    )(c_all, w_ada, b_ada.reshape(1, -1))


def _in_kernel(x_ref, g_ref, sc_ref, sh_ref, w_ref,
               q_ref, kv_ref, win_ref, qkvb_ref, gateb_ref, merge_ref, small_ref):
    h = _rms_rows(x_ref[...]) * g_ref[...]
    h = (h * (1.0 + sc_ref[...]) + sh_ref[...]).astype(BF16)

    def proj(c0, width):
        return jnp.dot(h, w_ref[:, c0:c0 + width], preferred_element_type=F32)

    q_ref[...] = proj(C_Q, A_WIDTH)
    kv_ref[...] = proj(C_KV, N_KV_SLOTS * A_KV_HEADS * HEAD_DIM)
    win_ref[...] = proj(C_KV + N_KV_SLOTS * A_KV_HEADS * HEAD_DIM, 2 * A_KV_HEADS * HEAD_DIM)
    qkvb_ref[...] = proj(C_QKVB, CONV_CH)
    gateb_ref[...] = proj(C_GATEB, B_WIDTH)
    merge_ref[...] = jax.nn.sigmoid(proj(C_MERGE, 2 * D_MODEL))
    small_ref[...] = proj(C_SMALL, SMALL_W)


def _in_proj(x, g1, sc, sh, w_main, tm):
    t = x.shape[0]
    widths = (A_WIDTH, 4 * A_KV_HEADS * HEAD_DIM, 2 * A_KV_HEADS * HEAD_DIM, CONV_CH, B_WIDTH,
              2 * D_MODEL, SMALL_W)
    return pl.pallas_call(
        _in_kernel,
        grid=(t // tm,),
        in_specs=[pl.BlockSpec((tm, D_MODEL), lambda i: (i, 0)),
                  _const_spec((1, D_MODEL)),
                  _row_spec(tm, D_MODEL, sc.shape[0]),
                  _row_spec(tm, D_MODEL, sh.shape[0]),
                  _const_spec((D_MODEL, W_MAIN))],
        out_specs=[pl.BlockSpec((tm, w), lambda i: (i, 0)) for w in widths],
        out_shape=[jax.ShapeDtypeStruct((t, w), F32) for w in widths],
        compiler_params=_cparams("arbitrary"),
        name="in_proj",
    )(x, g1, sc, sh, w_main)


def _merge_kernel(x_ref, oa_ref, ob_ref, msig_ref, gt1_ref, g2_ref, sc2_ref, sh2_ref,
                  wa_ref, wb_ref, wout_ref, wr_ref, x1_ref, h2_ref, score_ref):
    pa = jnp.dot(oa_ref[...].astype(BF16), wa_ref[...], preferred_element_type=F32)
    pb = jnp.dot(ob_ref[...].astype(BF16), wb_ref[...], preferred_element_type=F32)
    m = msig_ref[:, :D_MODEL] * pa + msig_ref[:, D_MODEL:] * pb
    y = jnp.dot(m.astype(BF16), wout_ref[...], preferred_element_type=F32)
    x1 = x_ref[...] + gt1_ref[...] * y
    x1_ref[...] = x1
    h2 = _rms_rows(x1) * g2_ref[...]
    h2 = h2 * (1.0 + sc2_ref[...]) + sh2_ref[...]
    h2 = h2.astype(BF16)
    h2_ref[...] = h2
    logits = jnp.dot(h2, wr_ref[...], preferred_element_type=F32)
    score_ref[...] = jax.nn.sigmoid(logits)


def _merge(x, o_a, o_b, msig, gt1, g2, sc2, sh2, wa, wb, wout, wr, tm):
    t = x.shape[0]
    return pl.pallas_call(
        _merge_kernel,
        grid=(t // tm,),
        in_specs=[pl.BlockSpec((tm, D_MODEL), lambda i: (i, 0)),
                  pl.BlockSpec((tm, A_WIDTH), lambda i: (i, 0)),
                  pl.BlockSpec((tm, B_WIDTH), lambda i: (i, 0)),
                  pl.BlockSpec((tm, 2 * D_MODEL), lambda i: (i, 0)),
                  _row_spec(tm, D_MODEL, gt1.shape[0]),
                  _const_spec((1, D_MODEL)),
                  _row_spec(tm, D_MODEL, sc2.shape[0]),
                  _row_spec(tm, D_MODEL, sh2.shape[0]),
                  _const_spec((A_WIDTH, D_MODEL)),
                  _const_spec((B_WIDTH, D_MODEL)),
                  _const_spec((D_MODEL, D_MODEL)),
                  _const_spec((D_MODEL, LANES))],
        out_specs=[pl.BlockSpec((tm, D_MODEL), lambda i: (i, 0)),
                   pl.BlockSpec((tm, D_MODEL), lambda i: (i, 0)),
                   pl.BlockSpec((tm, LANES), lambda i: (i, 0))],
        out_shape=[jax.ShapeDtypeStruct((t, D_MODEL), F32),
                   jax.ShapeDtypeStruct((t, D_MODEL), BF16),
                   jax.ShapeDtypeStruct((t, LANES), F32)],
        compiler_params=_cparams("arbitrary"),
        name="merge_out",
    )(x, o_a, o_b, msig, gt1, g2, sc2, sh2, wa, wb, wout, wr)


def _route(scores, bias):
    lane = lax.broadcasted_iota(jnp.int32, scores.shape, 1)
    live = lane < N_EXPERTS
    v = jnp.where(live, scores + bias, -jnp.inf)
    sel = jnp.zeros(scores.shape, jnp.bool_)
    for _ in range(TOP_K):
        m = jnp.max(v, axis=-1, keepdims=True)
        first = jnp.min(jnp.where(v == m, lane, LANES), axis=-1, keepdims=True)
        hit = lane == first
        sel = jnp.logical_or(sel, hit)
        v = jnp.where(hit, -jnp.inf, v)
    picked = jnp.where(sel, scores, 0.0)
    return picked / jnp.sum(picked, axis=-1, keepdims=True) * ROUTED_SCALE


def _moe_kernel(h2_ref, score_ref, bias_ref, x1_ref, gt2_ref, wgu_ref, wd_ref, wsgu_ref, wsd_ref,
                out_ref, gate_ref, acc_ref):
    p = pl.program_id(1)
    h2 = h2_ref[...]

    @pl.when(p == 0)
    def _():
        gate = _route(score_ref[...], bias_ref[...])
        hi = gate.astype(BF16)
        lo = (gate - hi.astype(F32)).astype(BF16)
        gate_ref[...] = jnp.concatenate([hi, lo], axis=-1)
        s = jnp.dot(h2, wsgu_ref[...], preferred_element_type=F32)
        sact = _silu(s[:, :D_SHARED]) * s[:, D_SHARED:]
        acc_ref[...] = jnp.dot(sact.astype(BF16), wsd_ref[...], preferred_element_type=F32)

    row = lax.broadcasted_iota(jnp.int32, (2 * LANES, 2 * D_EXPERT), 0) % LANES
    col = lax.broadcasted_iota(jnp.int32, (2 * LANES, 2 * D_EXPERT), 1) // D_EXPERT
    onehot = jnp.where(row == 2 * p + col, 1.0, 0.0).astype(BF16)
    gsel = jnp.dot(gate_ref[...], onehot, preferred_element_type=F32)

    acts = []
    for e in range(2):
        au = jnp.dot(h2, wgu_ref[e], preferred_element_type=F32)
        acts.append(_silu(au[:, :D_EXPERT]) * au[:, D_EXPERT:])
    act = (jnp.concatenate(acts, axis=-1) * gsel).astype(BF16)
    acc_ref[...] += jnp.dot(act, wd_ref[0], preferred_element_type=F32)

    @pl.when(p == pl.num_programs(1) - 1)
    def _():
        out_ref[...] = x1_ref[...] + gt2_ref[...] * acc_ref[...]


def _moe(h2, scores, bias, x1, gt2, wgu, wd2, wsgu, wsd, tm):
    t = h2.shape[0]
    npairs = N_EXPERTS // 2
    return pl.pallas_call(
        _moe_kernel,
        grid=(t // tm, npairs),
        in_specs=[pl.BlockSpec((tm, D_MODEL), lambda i, p: (i, 0)),
                  pl.BlockSpec((tm, LANES), lambda i, p: (i, 0)),
                  pl.BlockSpec((1, LANES), lambda i, p: (0, 0)),
                  pl.BlockSpec((tm, D_MODEL), lambda i, p: (i, 0)),
                  (pl.BlockSpec((1, D_MODEL), lambda i, p: (0, 0)) if gt2.shape[0] == 1
                   else pl.BlockSpec((tm, D_MODEL), lambda i, p: (i, 0))),
                  pl.BlockSpec((2, D_MODEL, 2 * D_EXPERT), lambda i, p: (p, 0, 0)),
                  pl.BlockSpec((1, 2 * D_EXPERT, D_MODEL), lambda i, p: (p, 0, 0)),
                  pl.BlockSpec((D_MODEL, 2 * D_SHARED), lambda i, p: (0, 0)),
                  pl.BlockSpec((D_SHARED, D_MODEL), lambda i, p: (0, 0))],
        out_specs=pl.BlockSpec((tm, D_MODEL), lambda i, p: (i, 0)),
        out_shape=jax.ShapeDtypeStruct((t, D_MODEL), F32),
        scratch_shapes=[pltpu.VMEM((tm, 2 * LANES), BF16), pltpu.VMEM((tm, D_MODEL), F32)],
        compiler_params=_cparams("parallel", "arbitrary"),
        name="moe",
    )(h2, scores, bias, x1, gt2, wgu, wd2, wsgu, wsd)


def _jx_rmsnorm(x, g):
    y = x * lax.rsqrt(jnp.mean(x * x, axis=-1, keepdims=True) + EPS)
    return y * g


def _jx_l2norm(x):
    return x * lax.rsqrt(jnp.sum(x * x, axis=-1, keepdims=True) + EPS)


def _jx_slopes():
    h = jnp.arange(1, A_HEADS + 1, dtype=F32)
    return jnp.exp2(-8.0 * h / A_HEADS)


def _jx_compress(rows, w, pe):
    B, L, K, D = rows.shape
    r = rows.reshape(B, L // CMP_STRIDE, CMP_STRIDE, K, D)
    first = jnp.einsum('bnjkd,jde->bnke', r, w[:CMP_STRIDE])
    second = jnp.einsum('bnjkd,jde->bnke', r, w[CMP_STRIDE:])
    bias = jnp.einsum('jd,jde->e', pe, w)
    return first[:, :-1] + second[:, 1:] + bias


def _jx_nsa_keys(rows, k_norm_g, w_cmp, pe_cmp):
    B, L = rows.shape[:2]
    nsb = L // SLC_BLOCK
    kc = _jx_rmsnorm(_jx_compress(rows[:, :, 0], w_cmp[0], pe_cmp[0]), k_norm_g[0])
    vc = _jx_compress(rows[:, :, 1], w_cmp[1], pe_cmp[1])
    cpos = jnp.arange(kc.shape[1], dtype=jnp.int32) * CMP_STRIDE + (CMP_BLOCK - 1)

    def blocks(r):
        return r.reshape(B, nsb, SLC_BLOCK, A_KV_HEADS, HEAD_DIM).transpose(0, 3, 1, 2, 4)

    ks = blocks(_jx_rmsnorm(rows[:, :, 2], k_norm_g[1]))
    vs = blocks(rows[:, :, 3])
    return kc, vc, cpos, ks, vs


def _jx_nsa_attend(q, qpos, kc, vc, cpos, ks, vs, kw, vw, wpos, gates):
    B, Q = q.shape[:2]
    qg = q.reshape(B, Q, A_KV_HEADS, A_GROUP, HEAD_DIM)
    sl = _jx_slopes().reshape(A_KV_HEADS, A_GROUP)[None, :, :, None, None]
    dist_c = qpos[:, None] - cpos[None, :]
    valid_c = dist_c >= 0
    s_c = jnp.einsum('bqkgd,bnkd->bkgqn', qg, kc).astype(F32)
    s_c = jnp.where(valid_c, s_c - sl * dist_c.astype(F32), NEG)
    p_c = jax.nn.softmax(s_c, axis=-1) * jnp.any(valid_c, axis=-1)[:, None].astype(F32)
    o_c = jnp.einsum('bkgqn,bnkd->bqkgd', p_c, vc)
    nsb = ks.shape[2]
    nc = kc.shape[1]
    pg = jnp.pad(p_c.sum(axis=2), ((0, 0), (0, 0), (0, 0), (0, RATIO * nsb - nc)))
    pg = pg.reshape(B, A_KV_HEADS, Q, nsb, RATIO)
    imp = pg.sum(-1) + jnp.pad(pg[..., :-1, RATIO - 1], ((0, 0), (0, 0), (0, 0), (1, 0)))
    jidx = jnp.arange(nsb, dtype=jnp.int32)
    cur = qpos // SLC_BLOCK
    forced = (jidx[None, :] == cur[:, None]) | (jidx[None, :] == 0)
    avail = jidx[None, :] <= cur[:, None]
    imp = jnp.where(forced, BIG, jnp.where(avail, imp, NEG))
    _, sel = lax.top_k(imp, min(SLC_TOPN, nsb))
    bi = jnp.arange(B)[:, None, None, None]
    ki = jnp.arange(A_KV_HEADS)[None, :, None, None]
    kg = ks[bi, ki, sel]
    vg = vs[bi, ki, sel]
    spos = sel[..., None] * SLC_BLOCK + jnp.arange(SLC_BLOCK, dtype=jnp.int32)
    dist_s = (qpos[:, None, None] - spos)[:, :, None]
    s_s = jnp.einsum('bqkgd,bkqnsd->bkgqns', qg, kg).astype(F32)
    s_s = jnp.where(dist_s >= 0, s_s - sl[..., None] * dist_s.astype(F32), NEG)
    shp = s_s.shape
    p_s = jax.nn.softmax(s_s.reshape(shp[:4] + (-1,)), axis=-1).reshape(shp)
    o_s = jnp.einsum('bkgqns,bkqnsd->bqkgd', p_s, vg)
    dist_w = qpos[:, None] - wpos[None, :]
    valid_w = (dist_w >= 0) & (dist_w <= WINDOW) & (wpos >= 0)[None, :]
    s_w = jnp.einsum('bqkgd,bnkd->bkgqn', qg, kw).astype(F32)
    s_w = jnp.where(valid_w, s_w - sl * dist_w.astype(F32), NEG)
    p_w = jax.nn.softmax(s_w, axis=-1)
    o_w = jnp.einsum('bkgqn,bnkd->bqkgd', p_w, vw)
    gr = gates.reshape(B, Q, A_KV_HEADS, A_GROUP, 3)
    o = gr[..., 0:1] * o_c + gr[..., 1:2] * o_s + gr[..., 2:3] * o_w
    return o.reshape(B, Q, A_WIDTH)


def _jx_nsa_prompt(q, gates, kv_rows, win_rows, lp):
    B, T = q.shape[:2]
    kc, vc, cpos, ks, vs = _jx_nsa_keys(kv_rows, lp['k_norm_g'], lp['w_cmp'], lp['pe_cmp'])
    kw = _jx_rmsnorm(win_rows[:, :, 0], lp['k_norm_g'][2])
    pad = ((0, 0), (WINDOW, 0), (0, 0), (0, 0))
    kw_pad = jnp.pad(kw, pad)
    vw_pad = jnp.pad(win_rows[:, :, 1], pad)

    def block(i):
        q0 = i * Q_BLOCK
        qb = lax.dynamic_slice_in_dim(q, q0, Q_BLOCK, axis=1)
        gb = lax.dynamic_slice_in_dim(gates, q0, Q_BLOCK, axis=1)
        qpos = q0 + jnp.arange(Q_BLOCK, dtype=jnp.int32)
        kwb = lax.dynamic_slice_in_dim(kw_pad, q0, WINDOW + Q_BLOCK, axis=1)
        vwb = lax.dynamic_slice_in_dim(vw_pad, q0, WINDOW + Q_BLOCK, axis=1)
        wpos = q0 - WINDOW + jnp.arange(WINDOW + Q_BLOCK, dtype=jnp.int32)
        return _jx_nsa_attend(qb, qpos, kc, vc, cpos, ks, vs, kwb, vwb, wpos, gb)

    o = lax.map(block, jnp.arange(T // Q_BLOCK, dtype=jnp.int32))
    return o.transpose(1, 0, 2, 3).reshape(B, T, A_WIDTH)


def _jx_nsa_sample(q, gates, kv_new, win_new, cache_kv, page_table, cache_win, lp):
    DB, S = q.shape[:2]
    P = page_table.shape[1] * cache_kv.shape[1]
    past = cache_kv[page_table].reshape(DB, P, N_KV_SLOTS, A_KV_HEADS, HEAD_DIM)
    rows = jnp.concatenate([past, kv_new], axis=1)
    L = P + S
    Lp = -(-L // SLC_BLOCK) * SLC_BLOCK
    rows = jnp.pad(rows, ((0, 0), (0, Lp - L), (0, 0), (0, 0), (0, 0)))
    kc, vc, cpos, ks, vs = _jx_nsa_keys(rows, lp['k_norm_g'], lp['w_cmp'], lp['pe_cmp'])
    nwb = cache_win.shape[1]
    win = jnp.concatenate([cache_win, win_new], axis=1)
    wpos = P - nwb + jnp.arange(nwb + S, dtype=jnp.int32)
    kw = _jx_rmsnorm(win[:, :, 0], lp['k_norm_g'][2])
    qpos = P + jnp.arange(S, dtype=jnp.int32)
    o = _jx_nsa_attend(q, qpos, kc, vc, cpos, ks, vs, kw, win[:, :, 1], wpos, gates)
    return o, win[:, S:]


def _jx_causal_conv(xpad, w):
    T = xpad.shape[1] - (CONV_W - 1)
    acc = xpad[:, 0:T] * w[0]
    for j in range(1, CONV_W):
        acc = acc + xpad[:, j:j + T] * w[j]
    return jax.nn.silu(acc)


def _jx_delta_inputs(qkv, a_b, b_b, a_log, dt_bias):
    B, T, _ = qkv.shape
    nk = B_HEADS * B_DK
    q = _jx_l2norm(qkv[..., :nk].reshape(B, T, B_HEADS, B_DK)) * B_DK ** -0.5
    k = _jx_l2norm(qkv[..., nk:2 * nk].reshape(B, T, B_HEADS, B_DK))
    v = qkv[..., 2 * nk:].reshape(B, T, B_HEADS, B_DV)
    g = -jnp.exp(a_log) * jax.nn.softplus(a_b + dt_bias)
    beta = jax.nn.sigmoid(b_b)
    return q, k, v, g, beta


def _jx_delta_chunked(q, k, v, g, beta):
    B, T, H, _ = q.shape
    C = DELTA_CHUNK
    N = T // C

    def chunks(x):
        return jnp.moveaxis(x.reshape((B, N, C) + x.shape[2:]), 2, 3)

    qc, kc, vc, gc, bc = chunks(q), chunks(k), chunks(v), chunks(g), chunks(beta)
    G = jnp.cumsum(gc, axis=-1)
    i = jnp.arange(C)
    incl = i[:, None] >= i[None, :]
    strict = i[:, None] > i[None, :]
    decay = jnp.exp(jnp.where(incl, G[..., :, None] - G[..., None, :], -jnp.inf))
    kk = jnp.einsum('bnhid,bnhjd->bnhij', kc, kc)
    a_mat = jnp.where(strict, kk * decay * bc[..., :, None], 0.0) + jnp.eye(C, dtype=F32)
    eg = jnp.exp(G)[..., None]
    rhs = jnp.concatenate([vc * bc[..., None], kc * bc[..., None] * eg], axis=-1)
    sol = lax.linalg.triangular_solve(a_mat, rhs, left_side=True, lower=True, unit_diagonal=True)
    u_c, w_c = sol[..., :B_DV], sol[..., B_DV:]
    qk = jnp.einsum('bnhid,bnhjd->bnhij', qc, kc) * decay
    q_dec = qc * eg
    k_dec = kc * jnp.exp(G[..., -1:] - G)[..., None]
    g_last = jnp.exp(G[..., -1])

    def step(s, inp):
        u_i, w_i, qk_i, qd_i, kd_i, gl_i = inp
        v_new = u_i - jnp.einsum('bhck,bhkv->bhcv', w_i, s)
        o = jnp.einsum('bhck,bhkv->bhcv', qd_i, s) + jnp.einsum('bhij,bhjv->bhiv', qk_i, v_new)
        s = s * gl_i[..., None, None] + jnp.einsum('bhck,bhcv->bhkv', kd_i, v_new)
        return s, o

    xs = (jnp.moveaxis(u_c, 1, 0), jnp.moveaxis(w_c, 1, 0), jnp.moveaxis(qk, 1, 0),
          jnp.moveaxis(q_dec, 1, 0), jnp.moveaxis(k_dec, 1, 0), jnp.moveaxis(g_last, 1, 0))
    s0 = jnp.zeros((B, H, B_DK, B_DV), F32)
    s_fin, o = lax.scan(step, s0, xs)
    o = jnp.moveaxis(jnp.moveaxis(o, 0, 1), 3, 2).reshape(B, T, H, B_DV)
    return o, s_fin


def _jx_delta_recurrent(s0, q, k, v, g, beta):
    def step(s, inp):
        q_t, k_t, v_t, g_t, b_t = inp
        s = s * jnp.exp(g_t)[..., None, None]
        u = b_t[..., None] * (v_t - jnp.einsum('bhk,bhkv->bhv', k_t, s))
        s = s + jnp.einsum('bhk,bhv->bhkv', k_t, u)
        return s, jnp.einsum('bhk,bhkv->bhv', q_t, s)

    xs = (jnp.moveaxis(q, 1, 0), jnp.moveaxis(k, 1, 0), jnp.moveaxis(v, 1, 0),
          jnp.moveaxis(g, 1, 0), jnp.moveaxis(beta, 1, 0))
    s_fin, o = lax.scan(step, s0, xs)
    return jnp.moveaxis(o, 0, 1), s_fin


def _jx_delta_output(o, gate_b, o_norm_g):
    B, T = o.shape[:2]
    gt = gate_b.reshape(B, T, B_HEADS, B_DV)
    y = _jx_rmsnorm(o, o_norm_g) * jax.nn.silu(gt)
    return y.reshape(B, T, B_WIDTH)


def _rearranged_w_in(w_in):
    o = IN_OFFSETS
    q_a, kv_a, g_a, qkv_b, a_b, b_b, gate_b, merge = (
        w_in[:, :o[0]], w_in[:, o[0]:o[1]], w_in[:, o[1]:o[2]], w_in[:, o[2]:o[3]],
        w_in[:, o[3]:o[4]], w_in[:, o[4]:o[5]], w_in[:, o[5]:o[6]], w_in[:, o[6]:])
    small = jnp.concatenate([g_a, a_b, b_b], axis=1)
    small = jnp.pad(small, ((0, 0), (0, SMALL_W - small.shape[1])))
    return jnp.concatenate([q_a, kv_a, qkv_b, gate_b, merge, small], axis=1).astype(BF16)


def kernel(x_prompt, x_sample, cache_nsa_kv, page_table, cache_win_kv, state_conv, state_delta,
           c_prompt, c_sample, norm1_g, norm2_g, w_ada, b_ada, w_in, q_norm_g, k_norm_g, w_cmp, pe_cmp,
           conv_w, a_log, dt_bias, o_norm_g, w_proj_a, w_proj_b, w_out, w_router, b_router,
           w_exp_gu, w_exp_down, w_sh_gu, w_sh_down):
    T = x_prompt.shape[1]
    DB = x_sample.shape[0]
    lp = {'q_norm_g': q_norm_g[0], 'k_norm_g': k_norm_g[0], 'w_cmp': w_cmp[0], 'pe_cmp': pe_cmp[0]}

    w_main = _rearranged_w_in(w_in[0])
    wa, wb, wout = w_proj_a[0].astype(BF16), w_proj_b[0].astype(BF16), w_out[0].astype(BF16)
    wr = jnp.pad(w_router[0], ((0, 0), (0, LANES - N_EXPERTS))).astype(BF16)
    br = jnp.pad(b_router[0], (0, LANES - N_EXPERTS)).reshape(1, LANES)
    wgu = w_exp_gu[0].astype(BF16)
    wd2 = w_exp_down[0].astype(BF16).reshape(N_EXPERTS // 2, 2 * D_EXPERT, D_MODEL)
    wsgu, wsd = w_sh_gu[0].astype(BF16), w_sh_down[0].astype(BF16)
    g1, g2 = norm1_g[0].reshape(1, -1), norm2_g[0].reshape(1, -1)

    c_all = jnp.concatenate([c_prompt, jnp.zeros((7, D_MODEL), F32), c_sample], axis=0)
    mod = _adaln(c_all, w_ada[0], b_ada[0])
    mp = [mod[0:1, i * D_MODEL:(i + 1) * D_MODEL] for i in range(6)]
    ms = [mod[8:8 + DB, i * D_MODEL:(i + 1) * D_MODEL] for i in range(6)]

    def mixer_inputs(outs, B, S):
        q_raw, kv, win, qkvb, gateb, msig, small = outs
        q = _jx_rmsnorm(q_raw.reshape(B, S, A_HEADS, HEAD_DIM), lp['q_norm_g']) * HEAD_DIM ** -0.5
        gates = jax.nn.sigmoid(small[:, :3 * A_HEADS]).reshape(B, S, A_HEADS, 3)
        kv_rows = kv.reshape(B, S, N_KV_SLOTS, A_KV_HEADS, HEAD_DIM)
        win_rows = win.reshape(B, S, 2, A_KV_HEADS, HEAD_DIM)
        a_b = small[:, 3 * A_HEADS:3 * A_HEADS + B_HEADS].reshape(B, S, B_HEADS)
        b_b = small[:, 3 * A_HEADS + B_HEADS:3 * A_HEADS + 2 * B_HEADS].reshape(B, S, B_HEADS)
        return q, gates, kv_rows, win_rows, qkvb.reshape(B, S, CONV_CH), a_b, b_b, gateb.reshape(B, S, B_WIDTH), msig

    xp = x_prompt.reshape(T, D_MODEL)
    outs = _in_proj(xp, g1, mp[1], mp[0], w_main, 256)
    q, gates, kv_rows, win_rows, qkv_b, a_b, b_b, gate_b, msig = mixer_inputs(outs, 1, T)
    o_a = _jx_nsa_prompt(q, gates, kv_rows, win_rows, lp)
    qkv_pad = jnp.pad(qkv_b, ((0, 0), (CONV_W - 1, 0), (0, 0)))
    qd, kd, vd, g, beta = _jx_delta_inputs(_jx_causal_conv(qkv_pad, conv_w[0]), a_b, b_b, a_log[0], dt_bias[0])
    o_d, s_fin_p = _jx_delta_chunked(qd, kd, vd, g, beta)
    o_b = _jx_delta_output(o_d, gate_b, o_norm_g[0])
    x1, h2, scores = _merge(xp, o_a.reshape(T, A_WIDTH), o_b.reshape(T, B_WIDTH), msig, mp[2], g2, mp[4], mp[3],
                            wa, wb, wout, wr, 256)
    y_prompt = _moe(h2, scores, br, x1, mp[5], wgu, wd2, wsgu, wsd, 1024).reshape(1, T, D_MODEL)
    win_prompt = win_rows[:, -min(WINDOW, T):]
    conv_prompt = qkv_pad[:, T:]

    xs = x_sample.reshape(DB, D_MODEL)
    outs = _in_proj(xs, g1, ms[1], ms[0], w_main, DB)
    q, gates, kv_new, win_new, qkv_b, a_b, b_b, gate_b, msig = mixer_inputs(outs, DB, 1)
    o_a, new_win = _jx_nsa_sample(q, gates, kv_new, win_new, cache_nsa_kv[0], page_table, cache_win_kv[0], lp)
    qkv_pad_s = jnp.concatenate([state_conv[0], qkv_b], axis=1)
    qd, kd, vd, g, beta = _jx_delta_inputs(_jx_causal_conv(qkv_pad_s, conv_w[0]), a_b, b_b, a_log[0], dt_bias[0])
    o_d, s_fin_s = _jx_delta_recurrent(state_delta[0], qd, kd, vd, g, beta)
    o_b = _jx_delta_output(o_d, gate_b, o_norm_g[0])
    x1, h2, scores = _merge(xs, o_a.reshape(DB, A_WIDTH), o_b.reshape(DB, B_WIDTH), msig, ms[2], g2, ms[4], ms[3],
                            wa, wb, wout, wr, DB)
    y_sample = _moe(h2, scores, br, x1, ms[5], wgu, wd2, wsgu, wsd, DB).reshape(DB, 1, D_MODEL)

    return (y_prompt, y_sample, kv_rows[None], win_prompt[None], conv_prompt[None], s_fin_p[None],
            kv_new[None], new_win[None], qkv_pad_s[:, 1:][None], s_fin_s[None])
```

```python
import functools
import math

import jax
import jax.numpy as jnp
import numpy as np
from jax import lax
from jax.experimental import pallas as pl
from jax.experimental.pallas import tpu as pltpu

F32 = jnp.float32
BF16 = jnp.bfloat16
HIGHEST = lax.Precision.HIGHEST

D_MODEL = 1024
A_HEADS = 8
A_KV_HEADS = 2
A_GROUP = A_HEADS // A_KV_HEADS
HEAD_DIM = 64
CMP_STRIDE = 16
CMP_BLOCK = 32
SLC_BLOCK = 64
RATIO = SLC_BLOCK // CMP_STRIDE
SLC_TOPN = 16
WINDOW = 512
Q_BLOCK = 128
N_KV_SLOTS = 4
B_HEADS = 4
B_DK = 128
B_DV = 128
CONV_W = 4
DELTA_CHUNK = 64
N_EXPERTS = 64
TOP_K = 6
D_EXPERT = 128
D_SHARED = 128
ROUTED_SCALE = 2.5
EPS = 1e-6
NEG = -1e30
BIG = 1e30

A_WIDTH = A_HEADS * HEAD_DIM
B_WIDTH = B_HEADS * B_DV
CONV_CH = 2 * B_HEADS * B_DK + B_HEADS * B_DV
KV_WIDTH = 6 * A_KV_HEADS * HEAD_DIM
IN_SPLITS = (A_WIDTH, KV_WIDTH, 3 * A_HEADS, CONV_CH, B_HEADS, B_HEADS, B_WIDTH, 2 * D_MODEL)
IN_OFFSETS = tuple(int(v) for v in np.cumsum(IN_SPLITS)[:-1])

LANES = 128
SMALL_W = LANES
C_Q = 0
C_KV = C_Q + A_WIDTH
C_QKVB = C_KV + KV_WIDTH
C_GATEB = C_QKVB + CONV_CH
C_MERGE = C_GATEB + B_WIDTH
C_SMALL = C_MERGE + 2 * D_MODEL
W_MAIN = C_SMALL + SMALL_W

VMEM_LIMIT = 56 * 1024 * 1024


def _cparams(*sem):
    return pltpu.CompilerParams(dimension_semantics=sem, vmem_limit_bytes=VMEM_LIMIT)


def _const_spec(shape, single=False):
    nd = len(shape)
    if single:
        return pl.BlockSpec(shape, lambda *_: (0,) * nd, pipeline_mode=pl.Buffered(1))
    return pl.BlockSpec(shape, lambda *_: (0,) * nd)


def _row_spec(tm, width, rows):
    if rows == 1:
        return pl.BlockSpec((1, width), lambda i: (0, 0))
    return pl.BlockSpec((tm, width), lambda i: (i, 0))


def _silu(x):
    return x * jax.nn.sigmoid(x)


def _rms_rows(x):
    return x * lax.rsqrt(jnp.mean(x * x, axis=-1, keepdims=True) + EPS)


def _mm(a, b, exact):
    if exact:
        return jnp.dot(a.astype(F32), b, preferred_element_type=F32, precision=HIGHEST)
    return jnp.dot(a.astype(BF16), b, preferred_element_type=F32)


def _ada_kernel(c_ref, w_ref, b_ref, o_ref):
    o_ref[...] = _mm(_silu(c_ref[...]), w_ref[...], True) + b_ref[...]


def _adaln(c_all, w_ada, b_ada):
    rows = c_all.shape[0]
    tn = 1024
    return pl.pallas_call(
        _ada_kernel,
        grid=(6 * D_MODEL // tn,),
        in_specs=[pl.BlockSpec((rows, D_MODEL), lambda j: (0, 0)),
                  pl.BlockSpec((D_MODEL, tn), lambda j: (0, j)),
                  pl.BlockSpec((1, tn), lambda j: (0, j))],
        out_specs=pl.BlockSpec((rows, tn), lambda j: (0, j)),
        out_shape=jax.ShapeDtypeStruct((rows, 6 * D_MODEL), F32),
        compiler_params=_cparams("arbitrary"),
        name="adaln",
    )(c_all, w_ada, b_ada.reshape(1, -1))


def _in_kernel(x_ref, g_ref, sc_ref, sh_ref, w_ref,
               q_ref, kv_ref, win_ref, qkvb_ref, gateb_ref, merge_ref, small_ref, *, exact):
    h = _rms_rows(x_ref[...]) * g_ref[...]
    h = h * (1.0 + sc_ref[...]) + sh_ref[...]
    if not exact:
        h = h.astype(BF16)

    def proj(c0, width):
        return _mm(h, w_ref[:, c0:c0 + width], exact)

    q_ref[...] = proj(C_Q, A_WIDTH)
    kv_ref[...] = proj(C_KV, N_KV_SLOTS * A_KV_HEADS * HEAD_DIM)
    win_ref[...] = proj(C_KV + N_KV_SLOTS * A_KV_HEADS * HEAD_DIM, 2 * A_KV_HEADS * HEAD_DIM)
    qkvb_ref[...] = proj(C_QKVB, CONV_CH)
    gateb_ref[...] = proj(C_GATEB, B_WIDTH)
    merge_ref[...] = jax.nn.sigmoid(proj(C_MERGE, 2 * D_MODEL))
    small_ref[...] = proj(C_SMALL, SMALL_W)


def _in_proj(x, g1, sc, sh, w_main, tm):
    t = x.shape[0]
    widths = (A_WIDTH, 4 * A_KV_HEADS * HEAD_DIM, 2 * A_KV_HEADS * HEAD_DIM, CONV_CH, B_WIDTH,
              2 * D_MODEL, SMALL_W)
    return pl.pallas_call(
        functools.partial(_in_kernel, exact=w_main.dtype == F32),
        grid=(t // tm,),
        in_specs=[pl.BlockSpec((tm, D_MODEL), lambda i: (i, 0)),
                  _const_spec((1, D_MODEL)),
                  _row_spec(tm, D_MODEL, sc.shape[0]),
                  _row_spec(tm, D_MODEL, sh.shape[0]),
                  _const_spec((D_MODEL, W_MAIN), single=True)],
        out_specs=[pl.BlockSpec((tm, w), lambda i: (i, 0)) for w in widths],
        out_shape=[jax.ShapeDtypeStruct((t, w), F32) for w in widths],
        compiler_params=_cparams("arbitrary"),
        name="in_proj",
    )(x, g1, sc, sh, w_main)


def _merge_kernel(x_ref, oa_ref, ob_ref, msig_ref, gt1_ref, g2_ref, sc2_ref, sh2_ref,
                  wa_ref, wb_ref, wout_ref, wr_ref, x1_ref, h2_ref, score_ref, *, exact):
    pa = _mm(oa_ref[...], wa_ref[...], exact)
    pb = _mm(ob_ref[...], wb_ref[...], exact)
    m = msig_ref[:, :D_MODEL] * pa + msig_ref[:, D_MODEL:] * pb
    y = _mm(m, wout_ref[...], exact)
    x1 = x_ref[...] + gt1_ref[...] * y
    x1_ref[...] = x1
    h2 = _rms_rows(x1) * g2_ref[...]
    h2 = h2 * (1.0 + sc2_ref[...]) + sh2_ref[...]
    h2_ref[...] = h2.astype(BF16)
    score_ref[...] = jax.nn.sigmoid(_mm(h2, wr_ref[...], exact))


def _merge(x, o_a, o_b, msig, gt1, g2, sc2, sh2, wa, wb, wout, wr, tm):
    t = x.shape[0]
    return pl.pallas_call(
        functools.partial(_merge_kernel, exact=wout.dtype == F32),
        grid=(t // tm,),
        in_specs=[pl.BlockSpec((tm, D_MODEL), lambda i: (i, 0)),
                  pl.BlockSpec((tm, A_WIDTH), lambda i: (i, 0)),
                  pl.BlockSpec((tm, B_WIDTH), lambda i: (i, 0)),
                  pl.BlockSpec((tm, 2 * D_MODEL), lambda i: (i, 0)),
                  _row_spec(tm, D_MODEL, gt1.shape[0]),
                  _const_spec((1, D_MODEL)),
                  _row_spec(tm, D_MODEL, sc2.shape[0]),
                  _row_spec(tm, D_MODEL, sh2.shape[0]),
                  _const_spec((A_WIDTH, D_MODEL)),
                  _const_spec((B_WIDTH, D_MODEL)),
                  _const_spec((D_MODEL, D_MODEL)),
                  _const_spec((D_MODEL, LANES))],
        out_specs=[pl.BlockSpec((tm, D_MODEL), lambda i: (i, 0)),
                   pl.BlockSpec((tm, D_MODEL), lambda i: (i, 0)),
                   pl.BlockSpec((tm, LANES), lambda i: (i, 0))],
        out_shape=[jax.ShapeDtypeStruct((t, D_MODEL), F32),
                   jax.ShapeDtypeStruct((t, D_MODEL), BF16),
                   jax.ShapeDtypeStruct((t, LANES), F32)],
        compiler_params=_cparams("arbitrary"),
        name="merge_out",
    )(x, o_a, o_b, msig, gt1, g2, sc2, sh2, wa, wb, wout, wr)


def _route(scores, bias):
    lane = lax.broadcasted_iota(jnp.int32, scores.shape, 1)
    live = lane < N_EXPERTS
    v = jnp.where(live, scores + bias, -jnp.inf)
    sel = jnp.zeros(scores.shape, jnp.bool_)
    for _ in range(TOP_K):
        m = jnp.max(v, axis=-1, keepdims=True)
        first = jnp.min(jnp.where(v == m, lane, LANES), axis=-1, keepdims=True)
        hit = lane == first
        sel = jnp.logical_or(sel, hit)
        v = jnp.where(hit, -jnp.inf, v)
    picked = jnp.where(sel, scores, 0.0)
    return picked / jnp.sum(picked, axis=-1, keepdims=True) * ROUTED_SCALE


def _moe_kernel(h2_ref, score_ref, bias_ref, x1_ref, gt2_ref, wgu_ref, wd_ref, wsgu_ref, wsd_ref,
                out_ref, gate_ref, acc_ref):
    p = pl.program_id(1)
    h2 = h2_ref[...]

    @pl.when(p == 0)
    def _():
        gate = _route(score_ref[...], bias_ref[...])
        hi = gate.astype(BF16)
        lo = (gate - hi.astype(F32)).astype(BF16)
        gate_ref[...] = jnp.concatenate([hi, lo], axis=-1)
        s = jnp.dot(h2, wsgu_ref[...], preferred_element_type=F32)
        sact = _silu(s[:, :D_SHARED]) * s[:, D_SHARED:]
        acc_ref[...] = jnp.dot(sact.astype(BF16), wsd_ref[...], preferred_element_type=F32)

    row = lax.broadcasted_iota(jnp.int32, (2 * LANES, 2 * D_EXPERT), 0) % LANES
    col = lax.broadcasted_iota(jnp.int32, (2 * LANES, 2 * D_EXPERT), 1) // D_EXPERT
    onehot = jnp.where(row == 2 * p + col, 1.0, 0.0).astype(BF16)
    gsel = jnp.dot(gate_ref[...], onehot, preferred_element_type=F32)

    acts = []
    for e in range(2):
        au = jnp.dot(h2, wgu_ref[e], preferred_element_type=F32)
        acts.append(_silu(au[:, :D_EXPERT]) * au[:, D_EXPERT:])
    act = (jnp.concatenate(acts, axis=-1) * gsel).astype(BF16)
    acc_ref[...] += jnp.dot(act, wd_ref[0], preferred_element_type=F32)

    @pl.when(p == pl.num_programs(1) - 1)
    def _():
        out_ref[...] = x1_ref[...] + gt2_ref[...] * acc_ref[...]


def _moe(h2, scores, bias, x1, gt2, wgu, wd2, wsgu, wsd, tm):
    t = h2.shape[0]
    npairs = N_EXPERTS // 2
    return pl.pallas_call(
        _moe_kernel,
        grid=(t // tm, npairs),
        in_specs=[pl.BlockSpec((tm, D_MODEL), lambda i, p: (i, 0)),
                  pl.BlockSpec((tm, LANES), lambda i, p: (i, 0)),
                  pl.BlockSpec((1, LANES), lambda i, p: (0, 0)),
                  pl.BlockSpec((tm, D_MODEL), lambda i, p: (i, 0)),
                  (pl.BlockSpec((1, D_MODEL), lambda i, p: (0, 0)) if gt2.shape[0] == 1
                   else pl.BlockSpec((tm, D_MODEL), lambda i, p: (i, 0))),
                  pl.BlockSpec((2, D_MODEL, 2 * D_EXPERT), lambda i, p: (p, 0, 0)),
                  pl.BlockSpec((1, 2 * D_EXPERT, D_MODEL), lambda i, p: (p, 0, 0)),
                  pl.BlockSpec((D_MODEL, 2 * D_SHARED), lambda i, p: (0, 0)),
                  pl.BlockSpec((D_SHARED, D_MODEL), lambda i, p: (0, 0))],
        out_specs=pl.BlockSpec((tm, D_MODEL), lambda i, p: (i, 0)),
        out_shape=jax.ShapeDtypeStruct((t, D_MODEL), F32),
        scratch_shapes=[pltpu.VMEM((tm, 2 * LANES), BF16), pltpu.VMEM((tm, D_MODEL), F32)],
        compiler_params=_cparams("parallel", "arbitrary"),
        name="moe",
    )(h2, scores, bias, x1, gt2, wgu, wd2, wsgu, wsd)


SEL_TILE = 512
SEL_TILE_BLOCKS = SEL_TILE // SLC_BLOCK
WIN_TILES = WINDOW // Q_BLOCK + 1
CMP_ROWS = CMP_STRIDE * N_KV_SLOTS * A_KV_HEADS * HEAD_DIM
CMP_COLS = 2 * A_KV_HEADS * HEAD_DIM


def _head_slope(h):
    return float(2.0 ** (-8.0 * (h + 1) / A_HEADS))


def _compress_kernel(x_ref, w1_ref, w2_ref, pe1_ref, pe2_ref, gk_ref, kc_ref, vct_ref, f1_ref, f2_ref, vc_ref,
                     *, nb, tile):
    i = pl.program_id(0)
    nsb = nb // RATIO

    half = A_KV_HEADS * HEAD_DIM

    @pl.when(i == 0)
    def _():
        for s in range(2):
            f2_ref[s, pl.ds(nb, 8), :] = jnp.zeros((8, half), F32)

    x = x_ref[...].astype(BF16)
    row0 = pl.multiple_of(i * tile, tile)
    first = jnp.dot(x, w1_ref[...], preferred_element_type=F32)
    second = jnp.dot(x, w2_ref[...], preferred_element_type=F32)
    for s in range(2):
        f1_ref[s, pl.ds(row0, tile), :] = first[:, s * half:(s + 1) * half]
        f2_ref[s, pl.ds(row0, tile), :] = second[:, s * half:(s + 1) * half]

    @pl.when(i == pl.num_programs(0) - 1)
    def _():
        bias = (jnp.dot(pe1_ref[...], w1_ref[...], preferred_element_type=F32)
                + jnp.dot(pe2_ref[...], w2_ref[...], preferred_element_type=F32))[0:1]
        for r in range(RATIO):
            blk = [f1_ref[s, pl.ds(r, nsb, stride=RATIO), :] + f2_ref[s, pl.ds(r + 1, nsb, stride=RATIO), :]
                   + bias[:, s * half:(s + 1) * half] for s in range(2)]
            for k in range(A_KV_HEADS):
                kc = _rms_rows(blk[0][:, k * HEAD_DIM:(k + 1) * HEAD_DIM]) * gk_ref[...]
                kc_ref[k, pl.ds(r * nsb, nsb), :] = kc.astype(BF16)
            vc_ref[pl.ds(r * nsb, nsb), :] = blk[1]
        vct = vc_ref[...].T
        for k in range(A_KV_HEADS):
            vct_ref[k] = vct[k * HEAD_DIM:(k + 1) * HEAD_DIM].astype(BF16)


def _nsa_compress(kv_rows, w1, w2, pe1, pe2, gk):
    t = kv_rows.shape[0]
    nb = t // CMP_STRIDE
    tile = min(256, nb)
    x16 = kv_rows.reshape(nb, CMP_ROWS)
    return pl.pallas_call(
        functools.partial(_compress_kernel, nb=nb, tile=tile),
        grid=(nb // tile,),
        in_specs=[pl.BlockSpec((tile, CMP_ROWS), lambda i: (i, 0)),
                  _const_spec((CMP_ROWS, CMP_COLS)), _const_spec((CMP_ROWS, CMP_COLS)),
                  _const_spec((16, CMP_ROWS)), _const_spec((16, CMP_ROWS)), _const_spec((1, HEAD_DIM))],
        out_specs=[_const_spec((A_KV_HEADS, nb, HEAD_DIM)), _const_spec((A_KV_HEADS, HEAD_DIM, nb))],
        out_shape=[jax.ShapeDtypeStruct((A_KV_HEADS, nb, HEAD_DIM), BF16),
                   jax.ShapeDtypeStruct((A_KV_HEADS, HEAD_DIM, nb), BF16)],
        scratch_shapes=[pltpu.VMEM((2, nb + 8, CMP_COLS // 2), F32), pltpu.VMEM((2, nb + 8, CMP_COLS // 2), F32),
                        pltpu.VMEM((nb, A_KV_HEADS * HEAD_DIM), F32)],
        compiler_params=_cparams("arbitrary"),
        name="nsa_compress",
    )(x16, w1, w2, pe1, pe2, gk)


def _nsa_rows_kernel(kv_ref, win_ref, gs_ref, gw_ref, ks_ref, vst_ref, kw_ref, vwt_ref):
    half = A_KV_HEADS * HEAD_DIM
    kv = kv_ref[...]
    win = win_ref[...]
    vst = kv[:, 3 * half:4 * half].T
    vwt = win[:, half:2 * half].T
    for k in range(A_KV_HEADS):
        lo, hi = k * HEAD_DIM, (k + 1) * HEAD_DIM
        ks_ref[k, 0] = (_rms_rows(kv[:, 2 * half + lo:2 * half + hi]) * gs_ref[...]).astype(BF16)
        kw = (_rms_rows(win[:, lo:hi]) * gw_ref[...]).astype(BF16)
        for j in range(SEL_TILE // Q_BLOCK):
            kw_ref[k, j] = kw[j * Q_BLOCK:(j + 1) * Q_BLOCK]
            vwt_ref[k, j] = vwt[lo:hi, j * Q_BLOCK:(j + 1) * Q_BLOCK].astype(BF16)
        vst_ref[k, 0] = vst[lo:hi].astype(BF16)


def _nsa_rows(kv_rows, win_rows, gs, gw):
    t = kv_rows.shape[0]
    nt = t // SEL_TILE
    per = SEL_TILE // Q_BLOCK
    return pl.pallas_call(
        _nsa_rows_kernel,
        grid=(nt,),
        in_specs=[pl.BlockSpec((SEL_TILE, kv_rows.shape[1]), lambda i: (i, 0)),
                  pl.BlockSpec((SEL_TILE, win_rows.shape[1]), lambda i: (i, 0)),
                  _const_spec((1, HEAD_DIM)), _const_spec((1, HEAD_DIM))],
        out_specs=[pl.BlockSpec((A_KV_HEADS, 1, SEL_TILE, HEAD_DIM), lambda i: (0, i, 0, 0)),
                   pl.BlockSpec((A_KV_HEADS, 1, HEAD_DIM, SEL_TILE), lambda i: (0, i, 0, 0)),
                   pl.BlockSpec((A_KV_HEADS, per, Q_BLOCK, HEAD_DIM), lambda i: (0, i, 0, 0)),
                   pl.BlockSpec((A_KV_HEADS, per, HEAD_DIM, Q_BLOCK), lambda i: (0, i, 0, 0))],
        out_shape=[jax.ShapeDtypeStruct((A_KV_HEADS, nt, SEL_TILE, HEAD_DIM), BF16),
                   jax.ShapeDtypeStruct((A_KV_HEADS, nt, HEAD_DIM, SEL_TILE), BF16),
                   jax.ShapeDtypeStruct((A_KV_HEADS, nt * per, Q_BLOCK, HEAD_DIM), BF16),
                   jax.ShapeDtypeStruct((A_KV_HEADS, nt * per, HEAD_DIM, Q_BLOCK), BF16)],
        compiler_params=_cparams("arbitrary"),
        name="nsa_rows",
    )(kv_rows, win_rows, gs, gw)


def _nsa_cmp_kernel(q_ref, gq_ref, kc_ref, vct_ref, qt_ref, oct_ref, sel_ref, any_ref, *, nb):
    nsb = nb // RATIO
    shift = nsb.bit_length() - 1
    q0 = pl.program_id(0) * Q_BLOCK
    qt_full = q_ref[...].T
    crow = lax.broadcasted_iota(jnp.int32, (nb, Q_BLOCK), 0)
    cpos = (((crow & (nsb - 1)) * RATIO + (crow >> shift)) * CMP_STRIDE) + (CMP_BLOCK - 1)
    qpos = q0 + lax.broadcasted_iota(jnp.int32, (nb, Q_BLOCK), 1)
    dist = qpos - cpos
    valid = dist >= 0
    distf = dist.astype(F32)
    qpos_row = q0 + lax.broadcasted_iota(jnp.int32, (1, Q_BLOCK), 1)
    any_valid = jnp.where(qpos_row >= CMP_BLOCK - 1, 1.0, 0.0)
    jrow = lax.broadcasted_iota(jnp.int32, (nsb, Q_BLOCK), 0)
    cur = (q0 + lax.broadcasted_iota(jnp.int32, (nsb, Q_BLOCK), 1)) >> (SLC_BLOCK.bit_length() - 1)
    forced = jnp.logical_or(jrow == cur, jrow == 0)
    avail = jrow <= cur

    for k in range(A_KV_HEADS):
        slabs = []
        for g in range(A_GROUP):
            h = k * A_GROUP + g
            slab = qt_full[h * HEAD_DIM:(h + 1) * HEAD_DIM]
            inv = lax.rsqrt(jnp.mean(slab * slab, axis=0, keepdims=True) + EPS)
            slabs.append(slab * inv * gq_ref[...] * HEAD_DIM ** -0.5)
        qt = jnp.concatenate(slabs, axis=1).astype(BF16)
        qt_ref[0, k] = qt
        st = jnp.dot(kc_ref[k], qt, preferred_element_type=F32)
        pg = jnp.zeros((nb, Q_BLOCK), F32)
        ps = []
        for g in range(A_GROUP):
            s = st[:, g * Q_BLOCK:(g + 1) * Q_BLOCK]
            s = jnp.where(valid, s - _head_slope(k * A_GROUP + g) * distf, NEG)
            e = jnp.exp(s - jnp.max(s, axis=0, keepdims=True))
            p = e / jnp.sum(e, axis=0, keepdims=True) * any_valid
            pg = pg + p
            ps.append(p.astype(BF16))
        oct_ref[0, k] = jnp.dot(vct_ref[k], jnp.concatenate(ps, axis=1), preferred_element_type=F32)
        last = pg[3 * nsb:4 * nsb]
        prev = jnp.where(jrow == 0, 0.0, pltpu.roll(last, 1, axis=0))
        imp = pg[0:nsb] + pg[nsb:2 * nsb] + pg[2 * nsb:3 * nsb] + last + prev
        x = jnp.where(forced, BIG, jnp.where(avail, imp, NEG))
        picked = jnp.zeros((nsb, Q_BLOCK), jnp.bool_)
        for _ in range(SLC_TOPN):
            m = jnp.max(x, axis=0, keepdims=True)
            first = jnp.min(jnp.where(x == m, jrow, nsb), axis=0, keepdims=True)
            hit = jrow == first
            picked = jnp.logical_or(picked, hit)
            x = jnp.where(hit, -jnp.inf, x)
        sel = jnp.where(jnp.logical_and(picked, avail), 1.0, 0.0)
        sel_ref[0, k] = sel
        any_ref[0, k] = jnp.max(sel, axis=1, keepdims=True)


def _nsa_cmp(q_raw, gq_col, kc, vct):
    t = q_raw.shape[0]
    nqb = t // Q_BLOCK
    nb = kc.shape[1]
    nsb = nb // RATIO
    shp = (nqb, A_KV_HEADS, HEAD_DIM, A_GROUP * Q_BLOCK)
    blk = lambda *s: pl.BlockSpec((1,) + s, lambda i: (i,) + (0,) * len(s))
    return pl.pallas_call(
        functools.partial(_nsa_cmp_kernel, nb=nb),
        grid=(nqb,),
        in_specs=[pl.BlockSpec((Q_BLOCK, A_WIDTH), lambda i: (i, 0)),
                  _const_spec((HEAD_DIM, 1)),
                  _const_spec((A_KV_HEADS, nb, HEAD_DIM)),
                  _const_spec((A_KV_HEADS, HEAD_DIM, nb))],
        out_specs=[blk(*shp[1:]), blk(*shp[1:]), blk(A_KV_HEADS, nsb, Q_BLOCK), blk(A_KV_HEADS, nsb, 1)],
        out_shape=[jax.ShapeDtypeStruct(shp, BF16), jax.ShapeDtypeStruct(shp, F32),
                   jax.ShapeDtypeStruct((nqb, A_KV_HEADS, nsb, Q_BLOCK), F32),
                   jax.ShapeDtypeStruct((nqb, A_KV_HEADS, nsb, 1), F32)],
        compiler_params=_cparams("arbitrary"),
        name="nsa_compressed",
    )(q_raw, gq_col, kc, vct)


def _nsa_sel_kernel(tiles_ref, counts_ref, qt_ref, oct_ref, sel_ref, small_ref, ks_ref, vst_ref, kw_ref, vwt_ref,
                    o_ref, *, ntiles):
    qb = pl.program_id(0)
    q0 = qb * Q_BLOCK
    gates_t = jax.nn.sigmoid(small_ref[...]).T
    qpos = q0 + lax.broadcasted_iota(jnp.int32, (SEL_TILE, Q_BLOCK), 1)
    krow = lax.broadcasted_iota(jnp.int32, (SEL_TILE, Q_BLOCK), 0)
    wrow = lax.broadcasted_iota(jnp.int32, (WIN_TILES * Q_BLOCK, Q_BLOCK), 0)
    wdist = (q0 + lax.broadcasted_iota(jnp.int32, (WIN_TILES * Q_BLOCK, Q_BLOCK), 1)) - (q0 - WINDOW + wrow)
    wvalid = jnp.logical_and(jnp.logical_and(wdist >= 0, wdist <= WINDOW), q0 - WINDOW + wrow >= 0)
    wdistf = wdist.astype(F32)
    outs = []
    for k in range(A_KV_HEADS):
        qt = qt_ref[0, k]

        def tile_step(i, carry, k=k, qt=qt):
            m_prev, l_prev, acc = carry
            t = tiles_ref[(qb * A_KV_HEADS + k) * ntiles + i]
            st = jnp.dot(ks_ref[k, t], qt, preferred_element_type=F32)
            mrows = sel_ref[0, k, pl.ds(pl.multiple_of(t * SEL_TILE_BLOCKS, SEL_TILE_BLOCKS), SEL_TILE_BLOCKS), :]
            picked = jnp.concatenate(
                [jnp.broadcast_to(mrows[b:b + 1], (SLC_BLOCK, Q_BLOCK)) for b in range(SEL_TILE_BLOCKS)], axis=0)
            dist = qpos - (t * SEL_TILE + krow)
            valid = jnp.logical_and(picked > 0.5, dist >= 0)
            distf = dist.astype(F32)
            ms, ls, ps, alphas = [], [], [], []
            for g in range(A_GROUP):
                lanes = slice(g * Q_BLOCK, (g + 1) * Q_BLOCK)
                s = jnp.where(valid, st[:, lanes] - _head_slope(k * A_GROUP + g) * distf, NEG)
                m_new = jnp.maximum(m_prev[:, lanes], jnp.max(s, axis=0, keepdims=True))
                alpha = jnp.exp(m_prev[:, lanes] - m_new)
                p = jnp.where(valid, jnp.exp(s - m_new), 0.0)
                ms.append(m_new)
                ls.append(alpha * l_prev[:, lanes] + jnp.sum(p, axis=0, keepdims=True))
                alphas.append(alpha)
                ps.append(p.astype(BF16))
            pv = jnp.dot(vst_ref[k, t], jnp.concatenate(ps, axis=1), preferred_element_type=F32)
            return (jnp.concatenate(ms, axis=1), jnp.concatenate(ls, axis=1),
                    jnp.concatenate(alphas, axis=1) * acc + pv)

        init = (jnp.full((1, A_GROUP * Q_BLOCK), NEG, F32), jnp.zeros((1, A_GROUP * Q_BLOCK), F32),
                jnp.zeros((HEAD_DIM, A_GROUP * Q_BLOCK), F32))
        _, l_fin, acc = lax.fori_loop(0, counts_ref[qb * A_KV_HEADS + k], tile_step, init)
        os_t = acc / l_fin

        wk, wv = [], []
        for j in range(WIN_TILES):
            ti = jnp.maximum(qb - (WIN_TILES - 1) + j, 0)
            wk.append(kw_ref[k, ti])
            wv.append(vwt_ref[k, ti])
        sw = jnp.dot(jnp.concatenate(wk, axis=0), qt, preferred_element_type=F32)
        pw = []
        for g in range(A_GROUP):
            s = jnp.where(wvalid, sw[:, g * Q_BLOCK:(g + 1) * Q_BLOCK] - _head_slope(k * A_GROUP + g) * wdistf, NEG)
            e = jnp.exp(s - jnp.max(s, axis=0, keepdims=True))
            pw.append((e / jnp.sum(e, axis=0, keepdims=True)).astype(BF16))
        ow_t = jnp.dot(jnp.concatenate(wv, axis=1), jnp.concatenate(pw, axis=1), preferred_element_type=F32)

        oc_t = oct_ref[0, k]
        for g in range(A_GROUP):
            h = k * A_GROUP + g
            lanes = slice(g * Q_BLOCK, (g + 1) * Q_BLOCK)
            outs.append(gates_t[3 * h:3 * h + 1] * oc_t[:, lanes] + gates_t[3 * h + 1:3 * h + 2] * os_t[:, lanes]
                        + gates_t[3 * h + 2:3 * h + 3] * ow_t[:, lanes])
    o_ref[...] = jnp.concatenate(outs, axis=0).T


def _nsa_sel(tiles, counts, qt, oct, sel, small, ks, vst, kw, vwt):
    nqb = qt.shape[0]
    t = nqb * Q_BLOCK
    ntiles = ks.shape[1]
    nsb = sel.shape[2]
    shp = (A_KV_HEADS, HEAD_DIM, A_GROUP * Q_BLOCK)
    full = lambda a: pl.BlockSpec(a.shape, lambda i, *_: (0,) * a.ndim)
    grid_spec = pltpu.PrefetchScalarGridSpec(
        num_scalar_prefetch=2,
        grid=(nqb,),
        in_specs=[pl.BlockSpec((1,) + shp, lambda i, *_: (i, 0, 0, 0)),
                  pl.BlockSpec((1,) + shp, lambda i, *_: (i, 0, 0, 0)),
                  pl.BlockSpec((1, A_KV_HEADS, nsb, Q_BLOCK), lambda i, *_: (i, 0, 0, 0)),
                  pl.BlockSpec((Q_BLOCK, SMALL_W), lambda i, *_: (i, 0)),
                  full(ks), full(vst), full(kw), full(vwt)],
        out_specs=pl.BlockSpec((Q_BLOCK, A_WIDTH), lambda i, *_: (i, 0)),
    )
    return pl.pallas_call(
        functools.partial(_nsa_sel_kernel, ntiles=ntiles),
        grid_spec=grid_spec,
        out_shape=jax.ShapeDtypeStruct((t, A_WIDTH), F32),
        compiler_params=_cparams("arbitrary"),
        name="nsa_selected",
    )(tiles, counts, qt, oct, sel, small, ks, vst, kw, vwt)


def _cmp_weights(w_cmp, pe_cmp):
    eye_k = jnp.eye(A_KV_HEADS, dtype=F32)
    ws, pes = [], []
    for half in range(2):
        w = w_cmp[:, half * CMP_STRIDE:(half + 1) * CMP_STRIDE]
        full = jnp.einsum('sjde,ts,kl->jtkdsle', w, jnp.eye(N_KV_SLOTS, 2, dtype=F32), eye_k)
        ws.append(full.reshape(CMP_ROWS, CMP_COLS).astype(BF16))
        pe = pe_cmp[:, half * CMP_STRIDE:(half + 1) * CMP_STRIDE]
        pe = jnp.pad(pe, ((0, N_KV_SLOTS - 2), (0, 0), (0, 0)))
        pe = jnp.broadcast_to(pe.transpose(1, 0, 2)[:, :, None, :], (CMP_STRIDE, N_KV_SLOTS, A_KV_HEADS, HEAD_DIM))
        pes.append(jnp.broadcast_to(pe.reshape(1, CMP_ROWS), (16, CMP_ROWS)).astype(BF16))
    return ws[0], ws[1], pes[0], pes[1]


def _nsa_prompt(q_raw, kv_rows, win_rows, small, q_norm_g, k_norm_g, w_cmp, pe_cmp):
    t = q_raw.shape[0]
    w1, w2, pe1, pe2 = _cmp_weights(w_cmp, pe_cmp)
    kc, vct = _nsa_compress(kv_rows, w1, w2, pe1, pe2, k_norm_g[0].reshape(1, HEAD_DIM))
    ks, vst, kw, vwt = _nsa_rows(kv_rows, win_rows, k_norm_g[1].reshape(1, HEAD_DIM), k_norm_g[2].reshape(1, HEAD_DIM))
    qt, oct, sel, bany = _nsa_cmp(q_raw, q_norm_g.reshape(HEAD_DIM, 1), kc, vct)
    nqb, ntiles = t // Q_BLOCK, t // SEL_TILE
    flags = bany.reshape(nqb, A_KV_HEADS, ntiles, SEL_TILE_BLOCKS).max(-1) > 0.5
    order = jnp.argsort(jnp.logical_not(flags), axis=-1, stable=True).astype(jnp.int32)
    counts = flags.sum(-1).astype(jnp.int32)
    return _nsa_sel(order.reshape(-1), counts.reshape(-1), qt, oct, sel, small, ks, vst, kw, vwt)


def _jx_rmsnorm(x, g):
    y = x * lax.rsqrt(jnp.mean(x * x, axis=-1, keepdims=True) + EPS)
    return y * g


def _jx_l2norm(x):
    return x * lax.rsqrt(jnp.sum(x * x, axis=-1, keepdims=True) + EPS)


def _jx_slopes():
    h = jnp.arange(1, A_HEADS + 1, dtype=F32)
    return jnp.exp2(-8.0 * h / A_HEADS)


def _jx_compress(rows, w, pe):
    B, L, K, D = rows.shape
    r = rows.reshape(B, L // CMP_STRIDE, CMP_STRIDE, K, D)
    first = jnp.einsum('bnjkd,jde->bnke', r, w[:CMP_STRIDE])
    second = jnp.einsum('bnjkd,jde->bnke', r, w[CMP_STRIDE:])
    bias = jnp.einsum('jd,jde->e', pe, w)
    return first[:, :-1] + second[:, 1:] + bias


def _jx_nsa_keys(rows, k_norm_g, w_cmp, pe_cmp):
    B, L = rows.shape[:2]
    nsb = L // SLC_BLOCK
    kc = _jx_rmsnorm(_jx_compress(rows[:, :, 0], w_cmp[0], pe_cmp[0]), k_norm_g[0])
    vc = _jx_compress(rows[:, :, 1], w_cmp[1], pe_cmp[1])
    cpos = jnp.arange(kc.shape[1], dtype=jnp.int32) * CMP_STRIDE + (CMP_BLOCK - 1)

    def blocks(r):
        return r.reshape(B, nsb, SLC_BLOCK, A_KV_HEADS, HEAD_DIM).transpose(0, 3, 1, 2, 4)

    ks = blocks(_jx_rmsnorm(rows[:, :, 2], k_norm_g[1]))
    vs = blocks(rows[:, :, 3])
    return kc, vc, cpos, ks, vs


def _jx_nsa_attend(q, qpos, kc, vc, cpos, ks, vs, kw, vw, wpos, gates):
    B, Q = q.shape[:2]
    qg = q.reshape(B, Q, A_KV_HEADS, A_GROUP, HEAD_DIM)
    sl = _jx_slopes().reshape(A_KV_HEADS, A_GROUP)[None, :, :, None, None]
    dist_c = qpos[:, None] - cpos[None, :]
    valid_c = dist_c >= 0
    s_c = jnp.einsum('bqkgd,bnkd->bkgqn', qg, kc).astype(F32)
    s_c = jnp.where(valid_c, s_c - sl * dist_c.astype(F32), NEG)
    p_c = jax.nn.softmax(s_c, axis=-1) * jnp.any(valid_c, axis=-1)[:, None].astype(F32)
    o_c = jnp.einsum('bkgqn,bnkd->bqkgd', p_c, vc)
    nsb = ks.shape[2]
    nc = kc.shape[1]
    pg = jnp.pad(p_c.sum(axis=2), ((0, 0), (0, 0), (0, 0), (0, RATIO * nsb - nc)))
    pg = pg.reshape(B, A_KV_HEADS, Q, nsb, RATIO)
    imp = pg.sum(-1) + jnp.pad(pg[..., :-1, RATIO - 1], ((0, 0), (0, 0), (0, 0), (1, 0)))
    jidx = jnp.arange(nsb, dtype=jnp.int32)
    cur = qpos // SLC_BLOCK
    forced = (jidx[None, :] == cur[:, None]) | (jidx[None, :] == 0)
    avail = jidx[None, :] <= cur[:, None]
    imp = jnp.where(forced, BIG, jnp.where(avail, imp, NEG))
    _, sel = lax.top_k(imp, min(SLC_TOPN, nsb))
    bi = jnp.arange(B)[:, None, None, None]
    ki = jnp.arange(A_KV_HEADS)[None, :, None, None]
    kg = ks[bi, ki, sel]
    vg = vs[bi, ki, sel]
    spos = sel[..., None] * SLC_BLOCK + jnp.arange(SLC_BLOCK, dtype=jnp.int32)
    dist_s = (qpos[:, None, None] - spos)[:, :, None]
    s_s = jnp.einsum('bqkgd,bkqnsd->bkgqns', qg, kg).astype(F32)
    s_s = jnp.where(dist_s >= 0, s_s - sl[..., None] * dist_s.astype(F32), NEG)
    shp = s_s.shape
    p_s = jax.nn.softmax(s_s.reshape(shp[:4] + (-1,)), axis=-1).reshape(shp)
    o_s = jnp.einsum('bkgqns,bkqnsd->bqkgd', p_s, vg)
    dist_w = qpos[:, None] - wpos[None, :]
    valid_w = (dist_w >= 0) & (dist_w <= WINDOW) & (wpos >= 0)[None, :]
    s_w = jnp.einsum('bqkgd,bnkd->bkgqn', qg, kw).astype(F32)
    s_w = jnp.where(valid_w, s_w - sl * dist_w.astype(F32), NEG)
    p_w = jax.nn.softmax(s_w, axis=-1)
    o_w = jnp.einsum('bkgqn,bnkd->bqkgd', p_w, vw)
    gr = gates.reshape(B, Q, A_KV_HEADS, A_GROUP, 3)
    o = gr[..., 0:1] * o_c + gr[..., 1:2] * o_s + gr[..., 2:3] * o_w
    return o.reshape(B, Q, A_WIDTH)


def _jx_nsa_prompt(q, gates, kv_rows, win_rows, lp):
    B, T = q.shape[:2]
    kc, vc, cpos, ks, vs = _jx_nsa_keys(kv_rows, lp['k_norm_g'], lp['w_cmp'], lp['pe_cmp'])
    kw = _jx_rmsnorm(win_rows[:, :, 0], lp['k_norm_g'][2])
    pad = ((0, 0), (WINDOW, 0), (0, 0), (0, 0))
    kw_pad = jnp.pad(kw, pad)
    vw_pad = jnp.pad(win_rows[:, :, 1], pad)

    def block(i):
        q0 = i * Q_BLOCK
        qb = lax.dynamic_slice_in_dim(q, q0, Q_BLOCK, axis=1)
        gb = lax.dynamic_slice_in_dim(gates, q0, Q_BLOCK, axis=1)
        qpos = q0 + jnp.arange(Q_BLOCK, dtype=jnp.int32)
        kwb = lax.dynamic_slice_in_dim(kw_pad, q0, WINDOW + Q_BLOCK, axis=1)
        vwb = lax.dynamic_slice_in_dim(vw_pad, q0, WINDOW + Q_BLOCK, axis=1)
        wpos = q0 - WINDOW + jnp.arange(WINDOW + Q_BLOCK, dtype=jnp.int32)
        return _jx_nsa_attend(qb, qpos, kc, vc, cpos, ks, vs, kwb, vwb, wpos, gb)

    o = lax.map(block, jnp.arange(T // Q_BLOCK, dtype=jnp.int32))
    return o.transpose(1, 0, 2, 3).reshape(B, T, A_WIDTH)


def _jx_nsa_sample(q, gates, kv_new, win_new, cache_kv, page_table, cache_win, lp):
    DB, S = q.shape[:2]
    P = page_table.shape[1] * cache_kv.shape[1]
    past = cache_kv[page_table].reshape(DB, P, N_KV_SLOTS, A_KV_HEADS, HEAD_DIM)
    rows = jnp.concatenate([past, kv_new], axis=1)
    L = P + S
    Lp = -(-L // SLC_BLOCK) * SLC_BLOCK
    rows = jnp.pad(rows, ((0, 0), (0, Lp - L), (0, 0), (0, 0), (0, 0)))
    kc, vc, cpos, ks, vs = _jx_nsa_keys(rows, lp['k_norm_g'], lp['w_cmp'], lp['pe_cmp'])
    nwb = cache_win.shape[1]
    win = jnp.concatenate([cache_win, win_new], axis=1)
    wpos = P - nwb + jnp.arange(nwb + S, dtype=jnp.int32)
    kw = _jx_rmsnorm(win[:, :, 0], lp['k_norm_g'][2])
    qpos = P + jnp.arange(S, dtype=jnp.int32)
    o = _jx_nsa_attend(q, qpos, kc, vc, cpos, ks, vs, kw, win[:, :, 1], wpos, gates)
    return o, win[:, S:]


def _jx_causal_conv(xpad, w):
    T = xpad.shape[1] - (CONV_W - 1)
    acc = xpad[:, 0:T] * w[0]
    for j in range(1, CONV_W):
        acc = acc + xpad[:, j:j + T] * w[j]
    return jax.nn.silu(acc)


def _jx_delta_inputs(qkv, a_b, b_b, a_log, dt_bias):
    B, T, _ = qkv.shape
    nk = B_HEADS * B_DK
    q = _jx_l2norm(qkv[..., :nk].reshape(B, T, B_HEADS, B_DK)) * B_DK ** -0.5
    k = _jx_l2norm(qkv[..., nk:2 * nk].reshape(B, T, B_HEADS, B_DK))
    v = qkv[..., 2 * nk:].reshape(B, T, B_HEADS, B_DV)
    g = -jnp.exp(a_log) * jax.nn.softplus(a_b + dt_bias)
    beta = jax.nn.sigmoid(b_b)
    return q, k, v, g, beta


def _jx_delta_chunked(q, k, v, g, beta):
    B, T, H, _ = q.shape
    C = DELTA_CHUNK
    N = T // C

    def chunks(x):
        return jnp.moveaxis(x.reshape((B, N, C) + x.shape[2:]), 2, 3)

    qc, kc, vc, gc, bc = chunks(q), chunks(k), chunks(v), chunks(g), chunks(beta)
    G = jnp.cumsum(gc, axis=-1)
    i = jnp.arange(C)
    incl = i[:, None] >= i[None, :]
    strict = i[:, None] > i[None, :]
    decay = jnp.exp(jnp.where(incl, G[..., :, None] - G[..., None, :], -jnp.inf))
    kk = jnp.einsum('bnhid,bnhjd->bnhij', kc, kc)
    a_mat = jnp.where(strict, kk * decay * bc[..., :, None], 0.0) + jnp.eye(C, dtype=F32)
    eg = jnp.exp(G)[..., None]
    rhs = jnp.concatenate([vc * bc[..., None], kc * bc[..., None] * eg], axis=-1)
    sol = lax.linalg.triangular_solve(a_mat, rhs, left_side=True, lower=True, unit_diagonal=True)
    u_c, w_c = sol[..., :B_DV], sol[..., B_DV:]
    qk = jnp.einsum('bnhid,bnhjd->bnhij', qc, kc) * decay
    q_dec = qc * eg
    k_dec = kc * jnp.exp(G[..., -1:] - G)[..., None]
    g_last = jnp.exp(G[..., -1])

    def step(s, inp):
        u_i, w_i, qk_i, qd_i, kd_i, gl_i = inp
        v_new = u_i - jnp.einsum('bhck,bhkv->bhcv', w_i, s)
        o = jnp.einsum('bhck,bhkv->bhcv', qd_i, s) + jnp.einsum('bhij,bhjv->bhiv', qk_i, v_new)
        s = s * gl_i[..., None, None] + jnp.einsum('bhck,bhcv->bhkv', kd_i, v_new)
        return s, o

    xs = (jnp.moveaxis(u_c, 1, 0), jnp.moveaxis(w_c, 1, 0), jnp.moveaxis(qk, 1, 0),
          jnp.moveaxis(q_dec, 1, 0), jnp.moveaxis(k_dec, 1, 0), jnp.moveaxis(g_last, 1, 0))
    s0 = jnp.zeros((B, H, B_DK, B_DV), F32)
    s_fin, o = lax.scan(step, s0, xs)
    o = jnp.moveaxis(jnp.moveaxis(o, 0, 1), 3, 2).reshape(B, T, H, B_DV)
    return o, s_fin


def _jx_delta_recurrent(s0, q, k, v, g, beta):
    def step(s, inp):
        q_t, k_t, v_t, g_t, b_t = inp
        s = s * jnp.exp(g_t)[..., None, None]
        u = b_t[..., None] * (v_t - jnp.einsum('bhk,bhkv->bhv', k_t, s))
        s = s + jnp.einsum('bhk,bhv->bhkv', k_t, u)
        return s, jnp.einsum('bhk,bhkv->bhv', q_t, s)

    xs = (jnp.moveaxis(q, 1, 0), jnp.moveaxis(k, 1, 0), jnp.moveaxis(v, 1, 0),
          jnp.moveaxis(g, 1, 0), jnp.moveaxis(beta, 1, 0))
    s_fin, o = lax.scan(step, s0, xs)
    return jnp.moveaxis(o, 0, 1), s_fin


def _jx_delta_output(o, gate_b, o_norm_g):
    B, T = o.shape[:2]
    gt = gate_b.reshape(B, T, B_HEADS, B_DV)
    y = _jx_rmsnorm(o, o_norm_g) * jax.nn.silu(gt)
    return y.reshape(B, T, B_WIDTH)


def _rearranged_w_in(w_in):
    o = IN_OFFSETS
    q_a, kv_a, g_a, qkv_b, a_b, b_b, gate_b, merge = (
        w_in[:, :o[0]], w_in[:, o[0]:o[1]], w_in[:, o[1]:o[2]], w_in[:, o[2]:o[3]],
        w_in[:, o[3]:o[4]], w_in[:, o[4]:o[5]], w_in[:, o[5]:o[6]], w_in[:, o[6]:])
    small = jnp.concatenate([g_a, a_b, b_b], axis=1)
    small = jnp.pad(small, ((0, 0), (0, SMALL_W - small.shape[1])))
    return jnp.concatenate([q_a, kv_a, qkv_b, gate_b, merge, small], axis=1)


def kernel(x_prompt, x_sample, cache_nsa_kv, page_table, cache_win_kv, state_conv, state_delta,
           c_prompt, c_sample, norm1_g, norm2_g, w_ada, b_ada, w_in, q_norm_g, k_norm_g, w_cmp, pe_cmp,
           conv_w, a_log, dt_bias, o_norm_g, w_proj_a, w_proj_b, w_out, w_router, b_router,
           w_exp_gu, w_exp_down, w_sh_gu, w_sh_down):
    T = x_prompt.shape[1]
    DB = x_sample.shape[0]
    lp = {'q_norm_g': q_norm_g[0], 'k_norm_g': k_norm_g[0], 'w_cmp': w_cmp[0], 'pe_cmp': pe_cmp[0]}

    w_main_f = _rearranged_w_in(w_in[0])
    w_main = w_main_f.astype(BF16)
    wa, wb, wout = w_proj_a[0].astype(BF16), w_proj_b[0].astype(BF16), w_out[0].astype(BF16)
    wr_f = jnp.pad(w_router[0], ((0, 0), (0, LANES - N_EXPERTS)))
    wr = wr_f.astype(BF16)
    br = jnp.pad(b_router[0], (0, LANES - N_EXPERTS)).reshape(1, LANES)
    wgu = w_exp_gu[0].astype(BF16)
    wd2 = w_exp_down[0].astype(BF16).reshape(N_EXPERTS // 2, 2 * D_EXPERT, D_MODEL)
    wsgu, wsd = w_sh_gu[0].astype(BF16), w_sh_down[0].astype(BF16)
    g1, g2 = norm1_g[0].reshape(1, -1), norm2_g[0].reshape(1, -1)

    c_all = jnp.concatenate([c_prompt, jnp.zeros((7, D_MODEL), F32), c_sample], axis=0)
    mod = _adaln(c_all, w_ada[0], b_ada[0])
    mp = [mod[0:1, i * D_MODEL:(i + 1) * D_MODEL] for i in range(6)]
    ms = [mod[8:8 + DB, i * D_MODEL:(i + 1) * D_MODEL] for i in range(6)]

    def mixer_inputs(outs, B, S):
        q_raw, kv, win, qkvb, gateb, msig, small = outs
        q = _jx_rmsnorm(q_raw.reshape(B, S, A_HEADS, HEAD_DIM), lp['q_norm_g']) * HEAD_DIM ** -0.5
        gates = jax.nn.sigmoid(small[:, :3 * A_HEADS]).reshape(B, S, A_HEADS, 3)
        kv_rows = kv.reshape(B, S, N_KV_SLOTS, A_KV_HEADS, HEAD_DIM)
        win_rows = win.reshape(B, S, 2, A_KV_HEADS, HEAD_DIM)
        a_b = small[:, 3 * A_HEADS:3 * A_HEADS + B_HEADS].reshape(B, S, B_HEADS)
        b_b = small[:, 3 * A_HEADS + B_HEADS:3 * A_HEADS + 2 * B_HEADS].reshape(B, S, B_HEADS)
        return q, gates, kv_rows, win_rows, qkvb.reshape(B, S, CONV_CH), a_b, b_b, gateb.reshape(B, S, B_WIDTH), msig

    xp = x_prompt.reshape(T, D_MODEL)
    outs = _in_proj(xp, g1, mp[1], mp[0], w_main, 256)
    q, gates, kv_rows, win_rows, qkv_b, a_b, b_b, gate_b, msig = mixer_inputs(outs, 1, T)
    o_a = _nsa_prompt(outs[0], outs[1], outs[2], outs[6], q_norm_g[0], k_norm_g[0], w_cmp[0], pe_cmp[0])
    qkv_pad = jnp.pad(qkv_b, ((0, 0), (CONV_W - 1, 0), (0, 0)))
    qd, kd, vd, g, beta = _jx_delta_inputs(_jx_causal_conv(qkv_pad, conv_w[0]), a_b, b_b, a_log[0], dt_bias[0])
    o_d, s_fin_p = _jx_delta_chunked(qd, kd, vd, g, beta)
    o_b = _jx_delta_output(o_d, gate_b, o_norm_g[0])
    x1, h2, scores = _merge(xp, o_a.reshape(T, A_WIDTH), o_b.reshape(T, B_WIDTH), msig, mp[2], g2, mp[4], mp[3],
                            wa, wb, wout, wr, 256)
    y_prompt = _moe(h2, scores, br, x1, mp[5], wgu, wd2, wsgu, wsd, 1024).reshape(1, T, D_MODEL)
    win_prompt = win_rows[:, -min(WINDOW, T):]
    conv_prompt = qkv_pad[:, T:]

    xs = x_sample.reshape(DB, D_MODEL)
    outs = _in_proj(xs, g1, ms[1], ms[0], w_main_f, DB)
    q, gates, kv_new, win_new, qkv_b, a_b, b_b, gate_b, msig = mixer_inputs(outs, DB, 1)
    with jax.default_matmul_precision("highest"):
        o_a, new_win = _jx_nsa_sample(q, gates, kv_new, win_new, cache_nsa_kv[0], page_table, cache_win_kv[0], lp)
        qkv_pad_s = jnp.concatenate([state_conv[0], qkv_b], axis=1)
        qd, kd, vd, g, beta = _jx_delta_inputs(_jx_causal_conv(qkv_pad_s, conv_w[0]), a_b, b_b, a_log[0],
                                               dt_bias[0])
        o_d, s_fin_s = _jx_delta_recurrent(state_delta[0], qd, kd, vd, g, beta)
        o_b = _jx_delta_output(o_d, gate_b, o_norm_g[0])
    x1, h2, scores = _merge(xs, o_a.reshape(DB, A_WIDTH), o_b.reshape(DB, B_WIDTH), msig, ms[2], g2, ms[4], ms[3],
                            w_proj_a[0], w_proj_b[0], w_out[0], wr_f, DB)
    y_sample = _moe(h2, scores, br, x1, ms[5], wgu, wd2, wsgu, wsd, DB).reshape(DB, 1, D_MODEL)

    return (y_prompt, y_sample, kv_rows[None], win_prompt[None], conv_prompt[None], s_fin_p[None],
            kv_new[None], new_win[None], qkv_pad_s[:, 1:][None], s_fin_s[None])
```

```python
import functools
import math

import jax
import jax.numpy as jnp
import numpy as np
from jax import lax
from jax.experimental import pallas as pl
from jax.experimental.pallas import tpu as pltpu

F32 = jnp.float32
BF16 = jnp.bfloat16
HIGHEST = lax.Precision.HIGHEST

D_MODEL = 1024
A_HEADS = 8
A_KV_HEADS = 2
A_GROUP = A_HEADS // A_KV_HEADS
HEAD_DIM = 64
CMP_STRIDE = 16
CMP_BLOCK = 32
SLC_BLOCK = 64
RATIO = SLC_BLOCK // CMP_STRIDE
SLC_TOPN = 16
WINDOW = 512
Q_BLOCK = 128
N_KV_SLOTS = 4
B_HEADS = 4
B_DK = 128
B_DV = 128
CONV_W = 4
DELTA_CHUNK = 64
N_EXPERTS = 64
TOP_K = 6
D_EXPERT = 128
D_SHARED = 128
ROUTED_SCALE = 2.5
EPS = 1e-6
NEG = -1e30
BIG = 1e30

A_WIDTH = A_HEADS * HEAD_DIM
B_WIDTH = B_HEADS * B_DV
CONV_CH = 2 * B_HEADS * B_DK + B_HEADS * B_DV
KV_WIDTH = 6 * A_KV_HEADS * HEAD_DIM
IN_SPLITS = (A_WIDTH, KV_WIDTH, 3 * A_HEADS, CONV_CH, B_HEADS, B_HEADS, B_WIDTH, 2 * D_MODEL)
IN_OFFSETS = tuple(int(v) for v in np.cumsum(IN_SPLITS)[:-1])

LANES = 128
SMALL_W = LANES
C_Q = 0
C_KV = C_Q + A_WIDTH
C_QKVB = C_KV + KV_WIDTH
C_GATEB = C_QKVB + CONV_CH
C_MERGE = C_GATEB + B_WIDTH
C_SMALL = C_MERGE + 2 * D_MODEL
W_MAIN = C_SMALL + SMALL_W

VMEM_LIMIT = 56 * 1024 * 1024


def _cparams(*sem):
    return pltpu.CompilerParams(dimension_semantics=sem, vmem_limit_bytes=VMEM_LIMIT)


def _const_spec(shape, single=False):
    nd = len(shape)
    if single:
        return pl.BlockSpec(shape, lambda *_: (0,) * nd, pipeline_mode=pl.Buffered(1))
    return pl.BlockSpec(shape, lambda *_: (0,) * nd)


def _row_spec(tm, width, rows):
    if rows == 1:
        return pl.BlockSpec((1, width), lambda i: (0, 0))
    return pl.BlockSpec((tm, width), lambda i: (i, 0))


def _silu(x):
    return x * jax.nn.sigmoid(x)


def _rms_rows(x):
    return x * lax.rsqrt(jnp.mean(x * x, axis=-1, keepdims=True) + EPS)


def _mm(a, b, exact):
    if exact:
        return jnp.dot(a.astype(F32), b, preferred_element_type=F32, precision=HIGHEST)
    return jnp.dot(a.astype(BF16), b, preferred_element_type=F32)


def _split3(x):
    hi = x.astype(BF16)
    lo = (x - hi.astype(F32)).astype(BF16)
    return jnp.concatenate([hi, lo, hi], axis=1)


def _stack3(x):
    hi = x.astype(BF16)
    lo = (x - hi.astype(F32)).astype(BF16)
    return jnp.concatenate([hi, hi, lo], axis=0)


def _mm3(a, b3):
    return jnp.dot(_split3(a), b3, preferred_element_type=F32)


def _ada_kernel(c_ref, w_ref, b_ref, o_ref):
    o_ref[...] = _mm(_silu(c_ref[...]), w_ref[...], True) + b_ref[...]


def _adaln(c_all, w_ada, b_ada):
    rows = c_all.shape[0]
    tn = 1024
    return pl.pallas_call(
        _ada_kernel,
        grid=(6 * D_MODEL // tn,),
        in_specs=[pl.BlockSpec((rows, D_MODEL), lambda j: (0, 0)),
                  pl.BlockSpec((D_MODEL, tn), lambda j: (0, j)),
                  pl.BlockSpec((1, tn), lambda j: (0, j))],
        out_specs=pl.BlockSpec((rows, tn), lambda j: (0, j)),
        out_shape=jax.ShapeDtypeStruct((rows, 6 * D_MODEL), F32),
        compiler_params=_cparams("arbitrary"),
        name="adaln",
    )(c_all, w_ada, b_ada.reshape(1, -1))


def _in_kernel(x_ref, g_ref, sc_ref, sh_ref, w_ref,
               q_ref, kv_ref, win_ref, qkvb_ref, gateb_ref, merge_ref, small_ref, *, exact):
    h = _rms_rows(x_ref[...]) * g_ref[...]
    h = h * (1.0 + sc_ref[...]) + sh_ref[...]
    if not exact:
        h = h.astype(BF16)

    def proj(c0, width):
        return _mm(h, w_ref[:, c0:c0 + width], exact)

    q_ref[...] = proj(C_Q, A_WIDTH)
    kv_ref[...] = proj(C_KV, N_KV_SLOTS * A_KV_HEADS * HEAD_DIM)
    win_ref[...] = proj(C_KV + N_KV_SLOTS * A_KV_HEADS * HEAD_DIM, 2 * A_KV_HEADS * HEAD_DIM)
    qkvb_ref[...] = proj(C_QKVB, CONV_CH)
    gateb_ref[...] = proj(C_GATEB, B_WIDTH)
    merge_ref[...] = jax.nn.sigmoid(proj(C_MERGE, 2 * D_MODEL))
    small_ref[...] = proj(C_SMALL, SMALL_W)


def _in_proj(x, g1, sc, sh, w_main, tm):
    t = x.shape[0]
    widths = (A_WIDTH, 4 * A_KV_HEADS * HEAD_DIM, 2 * A_KV_HEADS * HEAD_DIM, CONV_CH, B_WIDTH,
              2 * D_MODEL, SMALL_W)
    return pl.pallas_call(
        functools.partial(_in_kernel, exact=w_main.dtype == F32),
        grid=(t // tm,),
        in_specs=[pl.BlockSpec((tm, D_MODEL), lambda i: (i, 0)),
                  _const_spec((1, D_MODEL)),
                  _row_spec(tm, D_MODEL, sc.shape[0]),
                  _row_spec(tm, D_MODEL, sh.shape[0]),
                  _const_spec((D_MODEL, W_MAIN), single=True)],
        out_specs=[pl.BlockSpec((tm, w), lambda i: (i, 0)) for w in widths],
        out_shape=[jax.ShapeDtypeStruct((t, w), F32) for w in widths],
        compiler_params=_cparams("arbitrary"),
        name="in_proj",
    )(x, g1, sc, sh, w_main)


def _merge_kernel(x_ref, oa_ref, ob_ref, msig_ref, gt1_ref, g2_ref, sc2_ref, sh2_ref,
                  wa_ref, wb_ref, wout_ref, wr_ref, x1_ref, h2_ref, score_ref, *, exact):
    pa = _mm(oa_ref[...], wa_ref[...], exact)
    pb = _mm(ob_ref[...], wb_ref[...], exact)
    m = msig_ref[:, :D_MODEL] * pa + msig_ref[:, D_MODEL:] * pb
    y = _mm(m, wout_ref[...], exact)
    x1 = x_ref[...] + gt1_ref[...] * y
    x1_ref[...] = x1
    h2 = _rms_rows(x1) * g2_ref[...]
    h2 = h2 * (1.0 + sc2_ref[...]) + sh2_ref[...]
    h2_ref[...] = h2.astype(BF16)
    score_ref[...] = jax.nn.sigmoid(_mm(h2, wr_ref[...], exact))


def _merge(x, o_a, o_b, msig, gt1, g2, sc2, sh2, wa, wb, wout, wr, tm):
    t = x.shape[0]
    return pl.pallas_call(
        functools.partial(_merge_kernel, exact=wout.dtype == F32),
        grid=(t // tm,),
        in_specs=[pl.BlockSpec((tm, D_MODEL), lambda i: (i, 0)),
                  pl.BlockSpec((tm, A_WIDTH), lambda i: (i, 0)),
                  pl.BlockSpec((tm, B_WIDTH), lambda i: (i, 0)),
                  pl.BlockSpec((tm, 2 * D_MODEL), lambda i: (i, 0)),
                  _row_spec(tm, D_MODEL, gt1.shape[0]),
                  _const_spec((1, D_MODEL)),
                  _row_spec(tm, D_MODEL, sc2.shape[0]),
                  _row_spec(tm, D_MODEL, sh2.shape[0]),
                  _const_spec((A_WIDTH, D_MODEL)),
                  _const_spec((B_WIDTH, D_MODEL)),
                  _const_spec((D_MODEL, D_MODEL)),
                  _const_spec((D_MODEL, LANES))],
        out_specs=[pl.BlockSpec((tm, D_MODEL), lambda i: (i, 0)),
                   pl.BlockSpec((tm, D_MODEL), lambda i: (i, 0)),
                   pl.BlockSpec((tm, LANES), lambda i: (i, 0))],
        out_shape=[jax.ShapeDtypeStruct((t, D_MODEL), F32),
                   jax.ShapeDtypeStruct((t, D_MODEL), BF16),
                   jax.ShapeDtypeStruct((t, LANES), F32)],
        compiler_params=_cparams("arbitrary"),
        name="merge_out",
    )(x, o_a, o_b, msig, gt1, g2, sc2, sh2, wa, wb, wout, wr)


def _route(scores, bias):
    lane = lax.broadcasted_iota(jnp.int32, scores.shape, 1)
    live = lane < N_EXPERTS
    v = jnp.where(live, scores + bias, -jnp.inf)
    sel = jnp.zeros(scores.shape, jnp.bool_)
    for _ in range(TOP_K):
        m = jnp.max(v, axis=-1, keepdims=True)
        first = jnp.min(jnp.where(v == m, lane, LANES), axis=-1, keepdims=True)
        hit = lane == first
        sel = jnp.logical_or(sel, hit)
        v = jnp.where(hit, -jnp.inf, v)
    picked = jnp.where(sel, scores, 0.0)
    return picked / jnp.sum(picked, axis=-1, keepdims=True) * ROUTED_SCALE


def _moe_kernel(h2_ref, score_ref, bias_ref, x1_ref, gt2_ref, wgu_ref, wd_ref, wsgu_ref, wsd_ref,
                out_ref, gate_ref, acc_ref):
    p = pl.program_id(1)
    h2 = h2_ref[...]

    @pl.when(p == 0)
    def _():
        gate = _route(score_ref[...], bias_ref[...])
        hi = gate.astype(BF16)
        lo = (gate - hi.astype(F32)).astype(BF16)
        gate_ref[...] = jnp.concatenate([hi, lo], axis=-1)
        s = jnp.dot(h2, wsgu_ref[...], preferred_element_type=F32)
        sact = _silu(s[:, :D_SHARED]) * s[:, D_SHARED:]
        acc_ref[...] = jnp.dot(sact.astype(BF16), wsd_ref[...], preferred_element_type=F32)

    row = lax.broadcasted_iota(jnp.int32, (2 * LANES, 2 * D_EXPERT), 0) % LANES
    col = lax.broadcasted_iota(jnp.int32, (2 * LANES, 2 * D_EXPERT), 1) // D_EXPERT
    onehot = jnp.where(row == 2 * p + col, 1.0, 0.0).astype(BF16)
    gsel = jnp.dot(gate_ref[...], onehot, preferred_element_type=F32)

    acts = []
    for e in range(2):
        au = jnp.dot(h2, wgu_ref[e], preferred_element_type=F32)
        acts.append(_silu(au[:, :D_EXPERT]) * au[:, D_EXPERT:])
    act = (jnp.concatenate(acts, axis=-1) * gsel).astype(BF16)
    acc_ref[...] += jnp.dot(act, wd_ref[0], preferred_element_type=F32)

    @pl.when(p == pl.num_programs(1) - 1)
    def _():
        out_ref[...] = x1_ref[...] + gt2_ref[...] * acc_ref[...]


def _moe(h2, scores, bias, x1, gt2, wgu, wd2, wsgu, wsd, tm):
    t = h2.shape[0]
    npairs = N_EXPERTS // 2
    return pl.pallas_call(
        _moe_kernel,
        grid=(t // tm, npairs),
        in_specs=[pl.BlockSpec((tm, D_MODEL), lambda i, p: (i, 0)),
                  pl.BlockSpec((tm, LANES), lambda i, p: (i, 0)),
                  pl.BlockSpec((1, LANES), lambda i, p: (0, 0)),
                  pl.BlockSpec((tm, D_MODEL), lambda i, p: (i, 0)),
                  (pl.BlockSpec((1, D_MODEL), lambda i, p: (0, 0)) if gt2.shape[0] == 1
                   else pl.BlockSpec((tm, D_MODEL), lambda i, p: (i, 0))),
                  pl.BlockSpec((2, D_MODEL, 2 * D_EXPERT), lambda i, p: (p, 0, 0)),
                  pl.BlockSpec((1, 2 * D_EXPERT, D_MODEL), lambda i, p: (p, 0, 0)),
                  pl.BlockSpec((D_MODEL, 2 * D_SHARED), lambda i, p: (0, 0)),
                  pl.BlockSpec((D_SHARED, D_MODEL), lambda i, p: (0, 0))],
        out_specs=pl.BlockSpec((tm, D_MODEL), lambda i, p: (i, 0)),
        out_shape=jax.ShapeDtypeStruct((t, D_MODEL), F32),
        scratch_shapes=[pltpu.VMEM((tm, 2 * LANES), BF16), pltpu.VMEM((tm, D_MODEL), F32)],
        compiler_params=_cparams("parallel", "arbitrary"),
        name="moe",
    )(h2, scores, bias, x1, gt2, wgu, wd2, wsgu, wsd)


SEL_TILE = 512
SEL_TILE_BLOCKS = SEL_TILE // SLC_BLOCK
WIN_TILES = WINDOW // Q_BLOCK + 1
CMP_ROWS = CMP_STRIDE * N_KV_SLOTS * A_KV_HEADS * HEAD_DIM
CMP_COLS = 2 * A_KV_HEADS * HEAD_DIM


def _head_slope(h):
    return float(2.0 ** (-8.0 * (h + 1) / A_HEADS))


def _compress_kernel(x_ref, w1_ref, w2_ref, pe1_ref, pe2_ref, gk_ref, kc_ref, vct_ref, f1_ref, f2_ref, vc_ref,
                     *, nb, tile):
    i = pl.program_id(0)
    nsb = nb // RATIO

    half = A_KV_HEADS * HEAD_DIM

    @pl.when(i == 0)
    def _():
        for s in range(2):
            f2_ref[s, pl.ds(nb, 8), :] = jnp.zeros((8, half), F32)

    x = x_ref[...].astype(BF16)
    row0 = pl.multiple_of(i * tile, tile)
    first = jnp.dot(x, w1_ref[...], preferred_element_type=F32)
    second = jnp.dot(x, w2_ref[...], preferred_element_type=F32)
    for s in range(2):
        f1_ref[s, pl.ds(row0, tile), :] = first[:, s * half:(s + 1) * half]
        f2_ref[s, pl.ds(row0, tile), :] = second[:, s * half:(s + 1) * half]

    @pl.when(i == pl.num_programs(0) - 1)
    def _():
        bias = (jnp.dot(pe1_ref[...], w1_ref[...], preferred_element_type=F32)
                + jnp.dot(pe2_ref[...], w2_ref[...], preferred_element_type=F32))[0:1]
        for r in range(RATIO):
            blk = [f1_ref[s, pl.ds(r, nsb, stride=RATIO), :] + f2_ref[s, pl.ds(r + 1, nsb, stride=RATIO), :]
                   + bias[:, s * half:(s + 1) * half] for s in range(2)]
            for k in range(A_KV_HEADS):
                kc = _rms_rows(blk[0][:, k * HEAD_DIM:(k + 1) * HEAD_DIM]) * gk_ref[...]
                kc_ref[k, pl.ds(r * nsb, nsb), :] = kc.astype(BF16)
            vc_ref[pl.ds(r * nsb, nsb), :] = blk[1]
        vct = vc_ref[...].T
        for k in range(A_KV_HEADS):
            vct_ref[k] = vct[k * HEAD_DIM:(k + 1) * HEAD_DIM].astype(BF16)


def _nsa_compress(kv_rows, w1, w2, pe1, pe2, gk):
    t = kv_rows.shape[0]
    nb = t // CMP_STRIDE
    tile = min(256, nb)
    x16 = kv_rows.reshape(nb, CMP_ROWS)
    return pl.pallas_call(
        functools.partial(_compress_kernel, nb=nb, tile=tile),
        grid=(nb // tile,),
        in_specs=[pl.BlockSpec((tile, CMP_ROWS), lambda i: (i, 0)),
                  _const_spec((CMP_ROWS, CMP_COLS)), _const_spec((CMP_ROWS, CMP_COLS)),
                  _const_spec((16, CMP_ROWS)), _const_spec((16, CMP_ROWS)), _const_spec((1, HEAD_DIM))],
        out_specs=[_const_spec((A_KV_HEADS, nb, HEAD_DIM)), _const_spec((A_KV_HEADS, HEAD_DIM, nb))],
        out_shape=[jax.ShapeDtypeStruct((A_KV_HEADS, nb, HEAD_DIM), BF16),
                   jax.ShapeDtypeStruct((A_KV_HEADS, HEAD_DIM, nb), BF16)],
        scratch_shapes=[pltpu.VMEM((2, nb + 8, CMP_COLS // 2), F32), pltpu.VMEM((2, nb + 8, CMP_COLS // 2), F32),
                        pltpu.VMEM((nb, A_KV_HEADS * HEAD_DIM), F32)],
        compiler_params=_cparams("arbitrary"),
        name="nsa_compress",
    )(x16, w1, w2, pe1, pe2, gk)


def _nsa_rows_kernel(kv_ref, win_ref, gs_ref, gw_ref, ks_ref, vst_ref, kw_ref, vwt_ref):
    half = A_KV_HEADS * HEAD_DIM
    kv = kv_ref[...]
    win = win_ref[...]
    vst = kv[:, 3 * half:4 * half].T
    vwt = win[:, half:2 * half].T
    for k in range(A_KV_HEADS):
        lo, hi = k * HEAD_DIM, (k + 1) * HEAD_DIM
        ks_ref[k, 0] = (_rms_rows(kv[:, 2 * half + lo:2 * half + hi]) * gs_ref[...]).astype(BF16)
        kw = (_rms_rows(win[:, lo:hi]) * gw_ref[...]).astype(BF16)
        for j in range(SEL_TILE // Q_BLOCK):
            kw_ref[k, j] = kw[j * Q_BLOCK:(j + 1) * Q_BLOCK]
            vwt_ref[k, j] = vwt[lo:hi, j * Q_BLOCK:(j + 1) * Q_BLOCK].astype(BF16)
        vst_ref[k, 0] = vst[lo:hi].astype(BF16)


def _nsa_rows(kv_rows, win_rows, gs, gw):
    t = kv_rows.shape[0]
    nt = t // SEL_TILE
    per = SEL_TILE // Q_BLOCK
    return pl.pallas_call(
        _nsa_rows_kernel,
        grid=(nt,),
        in_specs=[pl.BlockSpec((SEL_TILE, kv_rows.shape[1]), lambda i: (i, 0)),
                  pl.BlockSpec((SEL_TILE, win_rows.shape[1]), lambda i: (i, 0)),
                  _const_spec((1, HEAD_DIM)), _const_spec((1, HEAD_DIM))],
        out_specs=[pl.BlockSpec((A_KV_HEADS, 1, SEL_TILE, HEAD_DIM), lambda i: (0, i, 0, 0)),
                   pl.BlockSpec((A_KV_HEADS, 1, HEAD_DIM, SEL_TILE), lambda i: (0, i, 0, 0)),
                   pl.BlockSpec((A_KV_HEADS, per, Q_BLOCK, HEAD_DIM), lambda i: (0, i, 0, 0)),
                   pl.BlockSpec((A_KV_HEADS, per, HEAD_DIM, Q_BLOCK), lambda i: (0, i, 0, 0))],
        out_shape=[jax.ShapeDtypeStruct((A_KV_HEADS, nt, SEL_TILE, HEAD_DIM), BF16),
                   jax.ShapeDtypeStruct((A_KV_HEADS, nt, HEAD_DIM, SEL_TILE), BF16),
                   jax.ShapeDtypeStruct((A_KV_HEADS, nt * per, Q_BLOCK, HEAD_DIM), BF16),
                   jax.ShapeDtypeStruct((A_KV_HEADS, nt * per, HEAD_DIM, Q_BLOCK), BF16)],
        compiler_params=_cparams("arbitrary"),
        name="nsa_rows",
    )(kv_rows, win_rows, gs, gw)


def _nsa_cmp_kernel(q_ref, gq_ref, kc_ref, vct_ref, qt_ref, oct_ref, sel_ref, any_ref, *, nb):
    nsb = nb // RATIO
    shift = nsb.bit_length() - 1
    q0 = pl.program_id(0) * Q_BLOCK
    qt_full = q_ref[...].T
    crow = lax.broadcasted_iota(jnp.int32, (nb, Q_BLOCK), 0)
    cpos = (((crow & (nsb - 1)) * RATIO + (crow >> shift)) * CMP_STRIDE) + (CMP_BLOCK - 1)
    qpos = q0 + lax.broadcasted_iota(jnp.int32, (nb, Q_BLOCK), 1)
    dist = qpos - cpos
    valid = dist >= 0
    distf = dist.astype(F32)
    qpos_row = q0 + lax.broadcasted_iota(jnp.int32, (1, Q_BLOCK), 1)
    any_valid = jnp.where(qpos_row >= CMP_BLOCK - 1, 1.0, 0.0)
    jrow = lax.broadcasted_iota(jnp.int32, (nsb, Q_BLOCK), 0)
    cur = (q0 + lax.broadcasted_iota(jnp.int32, (nsb, Q_BLOCK), 1)) >> (SLC_BLOCK.bit_length() - 1)
    forced = jnp.logical_or(jrow == cur, jrow == 0)
    avail = jrow <= cur

    for k in range(A_KV_HEADS):
        slabs = []
        for g in range(A_GROUP):
            h = k * A_GROUP + g
            slab = qt_full[h * HEAD_DIM:(h + 1) * HEAD_DIM]
            inv = lax.rsqrt(jnp.mean(slab * slab, axis=0, keepdims=True) + EPS)
            slabs.append(slab * inv * gq_ref[...] * HEAD_DIM ** -0.5)
        qt = jnp.concatenate(slabs, axis=1).astype(BF16)
        qt_ref[0, k] = qt
        st = jnp.dot(kc_ref[k], qt, preferred_element_type=F32)
        pg = jnp.zeros((nb, Q_BLOCK), F32)
        ps = []
        for g in range(A_GROUP):
            s = st[:, g * Q_BLOCK:(g + 1) * Q_BLOCK]
            s = jnp.where(valid, s - _head_slope(k * A_GROUP + g) * distf, NEG)
            e = jnp.exp(s - jnp.max(s, axis=0, keepdims=True))
            p = e / jnp.sum(e, axis=0, keepdims=True) * any_valid
            pg = pg + p
            ps.append(p.astype(BF16))
        oct_ref[0, k] = jnp.dot(vct_ref[k], jnp.concatenate(ps, axis=1), preferred_element_type=F32)
        last = pg[3 * nsb:4 * nsb]
        prev = jnp.where(jrow == 0, 0.0, pltpu.roll(last, 1, axis=0))
        imp = pg[0:nsb] + pg[nsb:2 * nsb] + pg[2 * nsb:3 * nsb] + last + prev
        x = jnp.where(forced, BIG, jnp.where(avail, imp, NEG))
        picked = jnp.zeros((nsb, Q_BLOCK), jnp.bool_)
        for _ in range(SLC_TOPN):
            m = jnp.max(x, axis=0, keepdims=True)
            first = jnp.min(jnp.where(x == m, jrow, nsb), axis=0, keepdims=True)
            hit = jrow == first
            picked = jnp.logical_or(picked, hit)
            x = jnp.where(hit, -jnp.inf, x)
        sel = jnp.where(jnp.logical_and(picked, avail), 1.0, 0.0)
        sel_ref[0, k] = sel
        any_ref[0, k] = jnp.max(sel, axis=1, keepdims=True)


def _nsa_cmp(q_raw, gq_col, kc, vct):
    t = q_raw.shape[0]
    nqb = t // Q_BLOCK
    nb = kc.shape[1]
    nsb = nb // RATIO
    shp = (nqb, A_KV_HEADS, HEAD_DIM, A_GROUP * Q_BLOCK)
    blk = lambda *s: pl.BlockSpec((1,) + s, lambda i: (i,) + (0,) * len(s))
    return pl.pallas_call(
        functools.partial(_nsa_cmp_kernel, nb=nb),
        grid=(nqb,),
        in_specs=[pl.BlockSpec((Q_BLOCK, A_WIDTH), lambda i: (i, 0)),
                  _const_spec((HEAD_DIM, 1)),
                  _const_spec((A_KV_HEADS, nb, HEAD_DIM)),
                  _const_spec((A_KV_HEADS, HEAD_DIM, nb))],
        out_specs=[blk(*shp[1:]), blk(*shp[1:]), blk(A_KV_HEADS, nsb, Q_BLOCK), blk(A_KV_HEADS, nsb, 1)],
        out_shape=[jax.ShapeDtypeStruct(shp, BF16), jax.ShapeDtypeStruct(shp, F32),
                   jax.ShapeDtypeStruct((nqb, A_KV_HEADS, nsb, Q_BLOCK), F32),
                   jax.ShapeDtypeStruct((nqb, A_KV_HEADS, nsb, 1), F32)],
        compiler_params=_cparams("arbitrary"),
        name="nsa_compressed",
    )(q_raw, gq_col, kc, vct)


def _nsa_sel_kernel(tiles_ref, counts_ref, qt_ref, oct_ref, sel_ref, small_ref, ks_ref, vst_ref, kw_ref, vwt_ref,
                    o_ref, *, ntiles):
    qb = pl.program_id(0)
    q0 = qb * Q_BLOCK
    gates_t = jax.nn.sigmoid(small_ref[...]).T
    qpos = q0 + lax.broadcasted_iota(jnp.int32, (SEL_TILE, Q_BLOCK), 1)
    krow = lax.broadcasted_iota(jnp.int32, (SEL_TILE, Q_BLOCK), 0)
    wrow = lax.broadcasted_iota(jnp.int32, (WIN_TILES * Q_BLOCK, Q_BLOCK), 0)
    wdist = (q0 + lax.broadcasted_iota(jnp.int32, (WIN_TILES * Q_BLOCK, Q_BLOCK), 1)) - (q0 - WINDOW + wrow)
    wvalid = jnp.logical_and(jnp.logical_and(wdist >= 0, wdist <= WINDOW), q0 - WINDOW + wrow >= 0)
    wdistf = wdist.astype(F32)
    outs = []
    for k in range(A_KV_HEADS):
        qt = qt_ref[0, k]

        def tile_step(i, carry, k=k, qt=qt):
            m_prev, l_prev, acc = carry
            t = tiles_ref[(qb * A_KV_HEADS + k) * ntiles + i]
            st = jnp.dot(ks_ref[k, t], qt, preferred_element_type=F32)
            mrows = sel_ref[0, k, pl.ds(pl.multiple_of(t * SEL_TILE_BLOCKS, SEL_TILE_BLOCKS), SEL_TILE_BLOCKS), :]
            picked = jnp.concatenate(
                [jnp.broadcast_to(mrows[b:b + 1], (SLC_BLOCK, Q_BLOCK)) for b in range(SEL_TILE_BLOCKS)], axis=0)
            dist = qpos - (t * SEL_TILE + krow)
            valid = jnp.logical_and(picked > 0.5, dist >= 0)
            distf = dist.astype(F32)
            ms, ls, ps, alphas = [], [], [], []
            for g in range(A_GROUP):
                lanes = slice(g * Q_BLOCK, (g + 1) * Q_BLOCK)
                s = jnp.where(valid, st[:, lanes] - _head_slope(k * A_GROUP + g) * distf, NEG)
                m_new = jnp.maximum(m_prev[:, lanes], jnp.max(s, axis=0, keepdims=True))
                alpha = jnp.exp(m_prev[:, lanes] - m_new)
                p = jnp.where(valid, jnp.exp(s - m_new), 0.0)
                ms.append(m_new)
                ls.append(alpha * l_prev[:, lanes] + jnp.sum(p, axis=0, keepdims=True))
                alphas.append(alpha)
                ps.append(p.astype(BF16))
            pv = jnp.dot(vst_ref[k, t], jnp.concatenate(ps, axis=1), preferred_element_type=F32)
            return (jnp.concatenate(ms, axis=1), jnp.concatenate(ls, axis=1),
                    jnp.concatenate(alphas, axis=1) * acc + pv)

        init = (jnp.full((1, A_GROUP * Q_BLOCK), NEG, F32), jnp.zeros((1, A_GROUP * Q_BLOCK), F32),
                jnp.zeros((HEAD_DIM, A_GROUP * Q_BLOCK), F32))
        _, l_fin, acc = lax.fori_loop(0, counts_ref[qb * A_KV_HEADS + k], tile_step, init)
        os_t = acc / l_fin

        wk, wv = [], []
        for j in range(WIN_TILES):
            ti = jnp.maximum(qb - (WIN_TILES - 1) + j, 0)
            wk.append(kw_ref[k, ti])
            wv.append(vwt_ref[k, ti])
        sw = jnp.dot(jnp.concatenate(wk, axis=0), qt, preferred_element_type=F32)
        pw = []
        for g in range(A_GROUP):
            s = jnp.where(wvalid, sw[:, g * Q_BLOCK:(g + 1) * Q_BLOCK] - _head_slope(k * A_GROUP + g) * wdistf, NEG)
            e = jnp.exp(s - jnp.max(s, axis=0, keepdims=True))
            pw.append((e / jnp.sum(e, axis=0, keepdims=True)).astype(BF16))
        ow_t = jnp.dot(jnp.concatenate(wv, axis=1), jnp.concatenate(pw, axis=1), preferred_element_type=F32)

        oc_t = oct_ref[0, k]
        for g in range(A_GROUP):
            h = k * A_GROUP + g
            lanes = slice(g * Q_BLOCK, (g + 1) * Q_BLOCK)
            outs.append(gates_t[3 * h:3 * h + 1] * oc_t[:, lanes] + gates_t[3 * h + 1:3 * h + 2] * os_t[:, lanes]
                        + gates_t[3 * h + 2:3 * h + 3] * ow_t[:, lanes])
    o_ref[...] = jnp.concatenate(outs, axis=0).T


def _nsa_sel(tiles, counts, qt, oct, sel, small, ks, vst, kw, vwt):
    nqb = qt.shape[0]
    t = nqb * Q_BLOCK
    ntiles = ks.shape[1]
    nsb = sel.shape[2]
    shp = (A_KV_HEADS, HEAD_DIM, A_GROUP * Q_BLOCK)
    full = lambda a: pl.BlockSpec(a.shape, lambda i, *_: (0,) * a.ndim)
    grid_spec = pltpu.PrefetchScalarGridSpec(
        num_scalar_prefetch=2,
        grid=(nqb,),
        in_specs=[pl.BlockSpec((1,) + shp, lambda i, *_: (i, 0, 0, 0)),
                  pl.BlockSpec((1,) + shp, lambda i, *_: (i, 0, 0, 0)),
                  pl.BlockSpec((1, A_KV_HEADS, nsb, Q_BLOCK), lambda i, *_: (i, 0, 0, 0)),
                  pl.BlockSpec((Q_BLOCK, SMALL_W), lambda i, *_: (i, 0)),
                  full(ks), full(vst), full(kw), full(vwt)],
        out_specs=pl.BlockSpec((Q_BLOCK, A_WIDTH), lambda i, *_: (i, 0)),
    )
    return pl.pallas_call(
        functools.partial(_nsa_sel_kernel, ntiles=ntiles),
        grid_spec=grid_spec,
        out_shape=jax.ShapeDtypeStruct((t, A_WIDTH), F32),
        compiler_params=_cparams("arbitrary"),
        name="nsa_selected",
    )(tiles, counts, qt, oct, sel, small, ks, vst, kw, vwt)


def _cmp_weights(w_cmp, pe_cmp):
    eye_k = jnp.eye(A_KV_HEADS, dtype=F32)
    ws, pes = [], []
    for half in range(2):
        w = w_cmp[:, half * CMP_STRIDE:(half + 1) * CMP_STRIDE]
        full = jnp.einsum('sjde,ts,kl->jtkdsle', w, jnp.eye(N_KV_SLOTS, 2, dtype=F32), eye_k)
        ws.append(full.reshape(CMP_ROWS, CMP_COLS).astype(BF16))
        pe = pe_cmp[:, half * CMP_STRIDE:(half + 1) * CMP_STRIDE]
        pe = jnp.pad(pe, ((0, N_KV_SLOTS - 2), (0, 0), (0, 0)))
        pe = jnp.broadcast_to(pe.transpose(1, 0, 2)[:, :, None, :], (CMP_STRIDE, N_KV_SLOTS, A_KV_HEADS, HEAD_DIM))
        pes.append(jnp.broadcast_to(pe.reshape(1, CMP_ROWS), (16, CMP_ROWS)).astype(BF16))
    return ws[0], ws[1], pes[0], pes[1]


def _nsa_prompt(q_raw, kv_rows, win_rows, small, q_norm_g, k_norm_g, w_cmp, pe_cmp):
    t = q_raw.shape[0]
    w1, w2, pe1, pe2 = _cmp_weights(w_cmp, pe_cmp)
    kc, vct = _nsa_compress(kv_rows, w1, w2, pe1, pe2, k_norm_g[0].reshape(1, HEAD_DIM))
    ks, vst, kw, vwt = _nsa_rows(kv_rows, win_rows, k_norm_g[1].reshape(1, HEAD_DIM), k_norm_g[2].reshape(1, HEAD_DIM))
    qt, oct, sel, bany = _nsa_cmp(q_raw, q_norm_g.reshape(HEAD_DIM, 1), kc, vct)
    nqb, ntiles = t // Q_BLOCK, t // SEL_TILE
    flags = bany.reshape(nqb, A_KV_HEADS, ntiles, SEL_TILE_BLOCKS).max(-1) > 0.5
    order = jnp.argsort(jnp.logical_not(flags), axis=-1, stable=True).astype(jnp.int32)
    counts = flags.sum(-1).astype(jnp.int32)
    return _nsa_sel(order.reshape(-1), counts.reshape(-1), qt, oct, sel, small, ks, vst, kw, vwt)


DELTA_STEP = 2 * DELTA_CHUNK
CONV_HALO = 8
SM_A = 3 * A_HEADS
SM_B = SM_A + B_HEADS


def _softplus(x):
    return jnp.maximum(x, 0.0) + jnp.log(1.0 + jnp.exp(-jnp.abs(x)))


def _l2_rows(x):
    return x * lax.rsqrt(jnp.sum(x * x, axis=-1, keepdims=True) + EPS)


def _dot_nt3(a, b):
    hi = b.astype(BF16)
    lo = (b - hi.astype(F32)).astype(BF16)
    return lax.dot_general(_split3(a), jnp.concatenate([hi, hi, lo], axis=1), (((1,), (1,)), ((), ())),
                           preferred_element_type=F32)


def _delta_kernel(x_ref, small_ref, gate_ref, cw_ref, alog_ref, dtb_ref, on_ref, o_ref, s_out_ref, xbuf, s_ref):
    i = pl.program_id(0)
    n = DELTA_STEP

    @pl.when(i == 0)
    def _():
        xbuf[0:CONV_HALO, :] = jnp.zeros((CONV_HALO, CONV_CH), F32)
        s_ref[...] = jnp.zeros(s_ref.shape, F32)

    x = x_ref[...]
    xbuf[CONV_HALO:CONV_HALO + n, :] = x
    base = CONV_HALO - (CONV_W - 1)
    acc = xbuf[pl.ds(base, n), :] * cw_ref[0:1, :]
    for j in range(1, CONV_W):
        acc = acc + xbuf[pl.ds(base + j, n), :] * cw_ref[j:j + 1, :]
    c = _silu(acc)
    xbuf[0:CONV_HALO, :] = x[n - CONV_HALO:n, :]

    sm = small_ref[...]
    g_all = -jnp.exp(alog_ref[...]) * _softplus(sm + dtb_ref[...])
    beta_all = jax.nn.sigmoid(sm)
    r = lax.broadcasted_iota(jnp.int32, (n, n), 0)
    cc = lax.broadcasted_iota(jnp.int32, (n, n), 1)
    same = (r >= DELTA_CHUNK) == (cc >= DELTA_CHUNK)
    incl = jnp.logical_and(same, r >= cc)
    strict = jnp.logical_and(same, r > cc)
    gcum = jnp.dot(jnp.where(incl, 1.0, 0.0), g_all, preferred_element_type=F32, precision=HIGHEST)
    glast = jnp.dot(jnp.where(same, 1.0, 0.0), g_all, preferred_element_type=F32, precision=HIGHEST)
    gcum_t = gcum.T

    nk = B_HEADS * B_DK
    for h in range(B_HEADS):
        gc = gcum[:, SM_A + h:SM_A + h + 1]
        gl = glast[:, SM_A + h:SM_A + h + 1]
        decay = jnp.exp(jnp.where(incl, gc - gcum_t[SM_A + h:SM_A + h + 1, :], -jnp.inf))
        bcol = beta_all[:, SM_B + h:SM_B + h + 1]
        q = _l2_rows(c[:, h * B_DK:(h + 1) * B_DK]) * B_DK ** -0.5
        k = _l2_rows(c[:, nk + h * B_DK:nk + (h + 1) * B_DK])
        v = c[:, 2 * nk + h * B_DV:2 * nk + (h + 1) * B_DV]
        pw = -jnp.where(strict, _dot_nt3(k, k) * decay * bcol, 0.0)
        tm = pw
        for _ in range(DELTA_CHUNK.bit_length() - 2):
            pw = _mm3(pw, _stack3(pw))
            tm = tm + pw + _mm3(tm, _stack3(pw))
        eg = jnp.exp(gc)
        rhs_u = v * bcol
        rhs_w = k * bcol * eg
        solved = _mm3(tm, _stack3(jnp.concatenate([rhs_u, rhs_w], axis=1)))
        u = rhs_u + solved[:, :B_DV]
        w = rhs_w + solved[:, B_DV:]
        qk = _dot_nt3(q, k) * decay
        q_dec = q * eg
        kd_t = (k * jnp.exp(gl - gc)).T
        s = s_ref[h]
        outs = []
        for ci in range(2):
            rows = slice(ci * DELTA_CHUNK, (ci + 1) * DELTA_CHUNK)
            s3 = _stack3(s)
            v_new = u[rows] - _mm3(w[rows], s3)
            outs.append(_mm3(q_dec[rows], s3) + _mm(qk[rows, rows], v_new.astype(BF16), False))
            g_end = jnp.exp(glast[ci * DELTA_CHUNK:ci * DELTA_CHUNK + 1, SM_A + h:SM_A + h + 1])
            s = s * g_end + _mm3(kd_t[:, rows], _stack3(v_new))
        s_ref[h] = s
        o = jnp.concatenate(outs, axis=0)
        o = _rms_rows(o) * on_ref[...] * _silu(gate_ref[:, h * B_DV:(h + 1) * B_DV])
        o_ref[:, h * B_DV:(h + 1) * B_DV] = o

    @pl.when(i == pl.num_programs(0) - 1)
    def _():
        s_out_ref[...] = s_ref[...]


def _lane_slab(vals, lane0):
    return jnp.pad(vals, (lane0, LANES - lane0 - vals.shape[0])).reshape(1, LANES)


def _delta_prompt(qkv_b, small, gate_b, conv_w, a_log, dt_bias, o_norm_g):
    t = qkv_b.shape[0]
    n = DELTA_STEP
    return pl.pallas_call(
        _delta_kernel,
        grid=(t // n,),
        in_specs=[pl.BlockSpec((n, CONV_CH), lambda i: (i, 0)),
                  pl.BlockSpec((n, SMALL_W), lambda i: (i, 0)),
                  pl.BlockSpec((n, B_WIDTH), lambda i: (i, 0)),
                  _const_spec((CONV_W, CONV_CH)), _const_spec((1, LANES)), _const_spec((1, LANES)),
                  _const_spec((1, B_DV))],
        out_specs=[pl.BlockSpec((n, B_WIDTH), lambda i: (i, 0)), _const_spec((B_HEADS, B_DK, B_DV))],
        out_shape=[jax.ShapeDtypeStruct((t, B_WIDTH), F32), jax.ShapeDtypeStruct((B_HEADS, B_DK, B_DV), F32)],
        scratch_shapes=[pltpu.VMEM((CONV_HALO + n, CONV_CH), F32), pltpu.VMEM((B_HEADS, B_DK, B_DV), F32)],
        compiler_params=_cparams("arbitrary"),
        name="delta_prompt",
    )(qkv_b, small, gate_b, conv_w, _lane_slab(a_log, SM_A), _lane_slab(dt_bias, SM_A), o_norm_g.reshape(1, B_DV))


PAGE_ROWS = 128
PAGES_PER_STEP = 16
STRIPES_PER_PAGE = PAGE_ROWS // CMP_STRIDE
HALF = A_KV_HEADS * HEAD_DIM
ROW_W = N_KV_SLOTS * HALF
SUB = 8
N_PICK = SLC_TOPN - 2


def _dot_nt_exact(a, b):
    return lax.dot_general(a, b, (((1,), (1,)), ((), ())), preferred_element_type=F32, precision=HIGHEST)


def _row_slopes(k):
    row = lax.broadcasted_iota(jnp.int32, (SUB, 1), 0)
    s = jnp.zeros((SUB, 1), F32)
    for g in range(A_GROUP):
        s = jnp.where(row == g, _head_slope(k * A_GROUP + g), s)
    return s


def _head_rows(x, k, gq):
    rows = [x[:, (k * A_GROUP + g) * HEAD_DIM:(k * A_GROUP + g + 1) * HEAD_DIM] for g in range(A_GROUP)]
    q = jnp.concatenate(rows + [jnp.zeros((SUB - A_GROUP, HEAD_DIM), F32)], axis=0)
    return _rms_rows(q) * gq * HEAD_DIM ** -0.5


def _nsa_s1_kernel(pt_ref, *refs, nb, past):
    del pt_ref
    npg = PAGES_PER_STEP
    kpages, vpages = refs[:npg], refs[npg:2 * npg]
    q_ref, gq_ref, gk_ref, w_ref, bias_ref = refs[2 * npg:2 * npg + 5]
    qn_ref, oc_ref, idx_ref = refs[2 * npg + 5:2 * npg + 8]
    f1_ref, f2_ref = refs[2 * npg + 8:]
    c = pl.program_id(1)
    nsb = nb // RATIO
    rows_step = npg * STRIPES_PER_PAGE

    @pl.when(jnp.logical_and(pl.program_id(0) == 0, c == 0))
    def _():
        for s in range(2):
            f2_ref[s, pl.ds(nb, SUB), :] = jnp.zeros((SUB, HALF), F32)

    row0 = pl.multiple_of(c * rows_step, rows_step)
    for s, pages in enumerate((kpages, vpages)):
        acc = jnp.zeros((rows_step, 2 * HALF), F32)
        for j in range(CMP_STRIDE):
            x = jnp.concatenate([p[0, pl.ds(j, STRIPES_PER_PAGE, stride=CMP_STRIDE), :] for p in pages], axis=0)
            acc = acc + jnp.dot(_split3(x), w_ref[s, j], preferred_element_type=F32)
        f1_ref[s, pl.ds(row0, rows_step), :] = acc[:, :HALF]
        f2_ref[s, pl.ds(row0, rows_step), :] = acc[:, HALF:]

    @pl.when(c == pl.num_programs(1) - 1)
    def _():
        lane = lax.broadcasted_iota(jnp.int32, (SUB, nb), 1)
        shift = nsb.bit_length() - 1
        cpos = (((lane & (nsb - 1)) * RATIO + (lane >> shift)) * CMP_STRIDE) + (CMP_BLOCK - 1)
        dist = past - cpos
        valid = dist >= 0
        distf = dist.astype(F32)
        row = lax.broadcasted_iota(jnp.int32, (SUB, nb), 0)
        kc = [[], []]
        vc = []
        for r in range(RATIO):
            kb = f1_ref[0, pl.ds(r, nsb, stride=RATIO), :] + f2_ref[0, pl.ds(r + 1, nsb, stride=RATIO), :] + bias_ref[0]
            vc.append(f1_ref[1, pl.ds(r, nsb, stride=RATIO), :] + f2_ref[1, pl.ds(r + 1, nsb, stride=RATIO), :]
                      + bias_ref[1])
            for k in range(A_KV_HEADS):
                kc[k].append(_rms_rows(kb[:, k * HEAD_DIM:(k + 1) * HEAD_DIM]) * gk_ref[...])
        vc = jnp.concatenate(vc, axis=0)
        qn, oc, imps = [], [], []
        for k in range(A_KV_HEADS):
            q = _head_rows(q_ref[0], k, gq_ref[...])
            qn.append(q)
            s = _dot_nt_exact(q, jnp.concatenate(kc[k], axis=0))
            s = jnp.where(valid, s - _row_slopes(k) * distf, NEG)
            e = jnp.exp(s - jnp.max(s, axis=-1, keepdims=True))
            p = e / jnp.sum(e, axis=-1, keepdims=True)
            p = jnp.where(jnp.logical_and(row < A_GROUP, past >= CMP_BLOCK - 1), p, 0.0)
            oc.append(_mm(p, vc[:, k * HEAD_DIM:(k + 1) * HEAD_DIM], True))
            pg = jnp.broadcast_to(jnp.sum(p, axis=0, keepdims=True), (SUB, nb))
            last = pg[:, 3 * nsb:4 * nsb]
            lane_b = lax.broadcasted_iota(jnp.int32, (SUB, nsb), 1)
            prev = jnp.where(lane_b == 0, 0.0, pltpu.roll(last, 1, axis=1))
            imps.append(pg[:, 0:nsb] + pg[:, nsb:2 * nsb] + pg[:, 2 * nsb:3 * nsb] + last + prev)
        qn_ref[0] = jnp.concatenate(qn, axis=1)
        oc_ref[0] = jnp.concatenate(oc, axis=1)
        lane_s = lax.broadcasted_iota(jnp.int32, (SUB, nsb), 1)
        row_s = lax.broadcasted_iota(jnp.int32, (SUB, nsb), 0)
        x = jnp.zeros((SUB, nsb), F32)
        for k in range(A_KV_HEADS):
            x = jnp.where(row_s == k, imps[k], x)
        x = jnp.where(lane_s == 0, -jnp.inf, x)
        out_lane = lax.broadcasted_iota(jnp.int32, (SUB, LANES), 1)
        picks = jnp.zeros((SUB, LANES), jnp.int32)
        for i in range(N_PICK):
            m = jnp.max(x, axis=-1, keepdims=True)
            first = jnp.min(jnp.where(x == m, lane_s, nsb), axis=-1, keepdims=True)
            picks = jnp.where(out_lane == i + 1, first, picks)
            x = jnp.where(lane_s == first, -jnp.inf, x)
        idx_ref[0] = picks


def _nsa_sample_cmp(q_raw, cache3, page_table, gq, gk, wcat, bias, past):
    db = q_raw.shape[0]
    n_pages = page_table.shape[1]
    nb = past // CMP_STRIDE
    nch = n_pages // PAGES_PER_STEP

    def page_spec(p, col):
        return pl.BlockSpec((1, PAGE_ROWS, HALF),
                            lambda b, c, pt: (pt[jnp.minimum(b, db - 1) * n_pages + c * PAGES_PER_STEP + p], 0, col))

    cst = lambda shape: pl.BlockSpec(shape, lambda b, c, pt: (0,) * len(shape))
    per_b = lambda: pl.BlockSpec((1, SUB, LANES), lambda b, c, pt: (b, 0, 0))
    grid_spec = pltpu.PrefetchScalarGridSpec(
        num_scalar_prefetch=1,
        grid=(db, nch),
        in_specs=([page_spec(p, 0) for p in range(PAGES_PER_STEP)] + [page_spec(p, 1) for p in range(PAGES_PER_STEP)]
                  + [pl.BlockSpec((1, 1, A_WIDTH), lambda b, c, pt: (b, 0, 0)),
                     cst((1, HEAD_DIM)), cst((1, HEAD_DIM)), cst(wcat.shape), cst(bias.shape)]),
        out_specs=[per_b(), per_b(), per_b()],
        scratch_shapes=[pltpu.VMEM((2, nb + SUB, HALF), F32), pltpu.VMEM((2, nb + SUB, HALF), F32)],
    )
    return pl.pallas_call(
        functools.partial(_nsa_s1_kernel, nb=nb, past=past),
        grid_spec=grid_spec,
        out_shape=[jax.ShapeDtypeStruct((db, SUB, LANES), F32), jax.ShapeDtypeStruct((db, SUB, LANES), F32),
                   jax.ShapeDtypeStruct((db, SUB, LANES), jnp.int32)],
        compiler_params=_cparams("arbitrary", "arbitrary"),
        name="nsa_sample_compressed",
    )(page_table.reshape(-1), *([cache3] * (2 * PAGES_PER_STEP)), q_raw.reshape(db, 1, A_WIDTH), gq, gk, wcat, bias)


def _nsa_s2_kernel(pt_ref, pick_ref, *refs, past):
    del pt_ref
    nsel = SLC_TOPN - 1
    blocks = refs[:A_KV_HEADS * nsel]
    (win_ref, kvn_ref, winn_ref, qn_ref, oc_ref, small_ref, gs_ref, gw_ref) = refs[A_KV_HEADS * nsel:-1]
    o_ref = refs[-1]
    b = pl.program_id(0)
    slane = lax.broadcasted_iota(jnp.int32, (1, nsel * SLC_BLOCK), 1)
    sblk = slane >> (SLC_BLOCK.bit_length() - 1)
    gates = jax.nn.sigmoid(small_ref[0])
    row = lax.broadcasted_iota(jnp.int32, (SUB, 1), 0)
    wlane = lax.broadcasted_iota(jnp.int32, (1, WINDOW), 1)
    wdist = (WINDOW - wlane).astype(F32)
    outs = []
    for k in range(A_KV_HEADS):
        lanes = slice(k * HEAD_DIM, (k + 1) * HEAD_DIM)
        q = qn_ref[0][:, lanes]
        slopes = _row_slopes(k)

        def attend(keys, vals, dist, k_new, v_new):
            s = _dot_nt_exact(q, keys) - slopes * dist
            s_new = jnp.sum(q * k_new, axis=-1, keepdims=True)
            m = jnp.maximum(jnp.max(s, axis=-1, keepdims=True), s_new)
            e, e_new = jnp.exp(s - m), jnp.exp(s_new - m)
            den = jnp.sum(e, axis=-1, keepdims=True) + e_new
            return (_mm(e, vals, True) + e_new * v_new) / den

        ks = jnp.concatenate([_rms_rows(blocks[k * nsel + i][0][:, lanes]) * gs_ref[...] for i in range(nsel)], axis=0)
        vs = jnp.concatenate([blocks[k * nsel + i][0][:, HALF + k * HEAD_DIM:HALF + (k + 1) * HEAD_DIM]
                              for i in range(nsel)], axis=0)
        kvn = kvn_ref[0]
        k_new = _rms_rows(kvn[:, 2 * HALF + k * HEAD_DIM:2 * HALF + (k + 1) * HEAD_DIM]) * gs_ref[...]
        v_new = kvn[:, 3 * HALF + k * HEAD_DIM:3 * HALF + (k + 1) * HEAD_DIM]
        blk_id = jnp.zeros((1, nsel * SLC_BLOCK), jnp.int32)
        for i in range(nsel):
            blk_id = jnp.where(sblk == i, pick_ref[(b * A_KV_HEADS + k) * nsel + i], blk_id)
        sdist = (past - (blk_id * SLC_BLOCK + (slane & (SLC_BLOCK - 1)))).astype(F32)
        o_s = attend(ks, vs, sdist, k_new, v_new)

        win = win_ref[0]
        kw = _rms_rows(win[:, lanes]) * gw_ref[...]
        vw = win[:, HALF + k * HEAD_DIM:HALF + (k + 1) * HEAD_DIM]
        winn = winn_ref[0]
        o_w = attend(kw, vw, wdist, _rms_rows(winn[:, lanes]) * gw_ref[...],
                     winn[:, HALF + k * HEAD_DIM:HALF + (k + 1) * HEAD_DIM])

        gcol = [jnp.zeros((SUB, 1), F32) for _ in range(3)]
        for g in range(A_GROUP):
            h = k * A_GROUP + g
            for br in range(3):
                gcol[br] = jnp.where(row == g, gates[:, 3 * h + br:3 * h + br + 1], gcol[br])
        outs.append(gcol[0] * oc_ref[0][:, lanes] + gcol[1] * o_s + gcol[2] * o_w)
    o_ref[0] = jnp.concatenate(outs, axis=1)


def _nsa_sample_sel(cache_half, page_table, picks, cache_win, kv_new, win_new, qn, oc, small, gs, gw, past):
    db = qn.shape[0]
    n_pages = page_table.shape[1]
    nsel = SLC_TOPN - 1
    per_page = PAGE_ROWS // SLC_BLOCK

    def block_spec(k, i):
        def imap(b, pt, pk):
            bb = jnp.minimum(b, db - 1)
            j = jnp.clip(pk[(bb * A_KV_HEADS + k) * nsel + i], 0, n_pages * per_page - 1)
            return (pt[bb * n_pages + j // per_page] * per_page + j % per_page, 0, 1)
        return pl.BlockSpec((1, SLC_BLOCK, 2 * HALF), imap)

    cst = lambda shape: pl.BlockSpec(shape, lambda b, pt, pk: (0,) * len(shape))
    per_b = lambda a: pl.BlockSpec((1,) + a.shape[1:], lambda b, pt, pk: (b,) + (0,) * (a.ndim - 1))
    grid_spec = pltpu.PrefetchScalarGridSpec(
        num_scalar_prefetch=2,
        grid=(db,),
        in_specs=([block_spec(k, i) for k in range(A_KV_HEADS) for i in range(nsel)]
                  + [per_b(cache_win), per_b(kv_new), per_b(win_new), per_b(qn), per_b(oc), per_b(small),
                     cst((1, HEAD_DIM)), cst((1, HEAD_DIM))]),
        out_specs=pl.BlockSpec((1, SUB, LANES), lambda b, pt, pk: (b, 0, 0)),
    )
    return pl.pallas_call(
        functools.partial(_nsa_s2_kernel, past=past),
        grid_spec=grid_spec,
        out_shape=jax.ShapeDtypeStruct((db, SUB, LANES), F32),
        compiler_params=_cparams("arbitrary"),
        name="nsa_sample_selected",
    )(page_table.reshape(-1), picks.reshape(-1), *([cache_half] * (A_KV_HEADS * nsel)),
      cache_win, kv_new, win_new, qn, oc, small, gs, gw)


def _nsa_sample(q_raw, kv_new, win_new, small, cache_kv, page_table, cache_win, q_norm_g, k_norm_g, w_cmp, pe_cmp):
    db = q_raw.shape[0]
    n_phys = cache_kv.shape[0]
    past = page_table.shape[1] * PAGE_ROWS
    assert cache_kv.shape[1] == PAGE_ROWS and past % SLC_BLOCK == 0 and past >= WINDOW == cache_win.shape[1]
    eye_k = jnp.eye(A_KV_HEADS, dtype=F32)
    w = jnp.einsum('sjde,kl->sjkdle', w_cmp, eye_k).reshape(2, CMP_BLOCK, HALF, HALF)
    w = jnp.concatenate([w[:, :CMP_STRIDE], w[:, CMP_STRIDE:]], axis=-1)
    hi = w.astype(BF16)
    lo = (w - hi.astype(F32)).astype(BF16)
    wcat = jnp.concatenate([hi, hi, lo], axis=2)
    with jax.default_matmul_precision("highest"):
        bias = jnp.einsum('sjd,sjde->se', pe_cmp, w_cmp)
    bias = jnp.tile(bias[:, None, :], (1, 1, A_KV_HEADS))
    cache3 = cache_kv.reshape(n_phys, PAGE_ROWS, ROW_W)
    qn, oc, picks = _nsa_sample_cmp(q_raw, cache3, page_table, q_norm_g.reshape(1, HEAD_DIM),
                                    k_norm_g[0].reshape(1, HEAD_DIM), wcat, bias, past)
    picks = picks[:, :A_KV_HEADS, :SLC_TOPN - 1]
    cache_half = cache_kv.reshape(n_phys * (PAGE_ROWS // SLC_BLOCK), SLC_BLOCK, ROW_W)
    o = _nsa_sample_sel(cache_half, page_table, picks, cache_win.reshape(db, WINDOW, 2 * HALF),
                        kv_new.reshape(db, 1, ROW_W), win_new.reshape(db, 1, 2 * HALF), qn, oc,
                        small.reshape(db, 1, SMALL_W), k_norm_g[1].reshape(1, HEAD_DIM),
                        k_norm_g[2].reshape(1, HEAD_DIM), past)
    o = o[:, :A_GROUP].reshape(db, A_GROUP, A_KV_HEADS, HEAD_DIM).transpose(0, 2, 1, 3)
    return o.reshape(db, A_WIDTH)


def _delta_step_kernel(conv_ref, x_ref, small_ref, gate_ref, cw_ref, alog_ref, dtb_ref, on_ref, s_ref, o_ref, s_out_ref):
    acc = conv_ref[0][0:1, :] * cw_ref[0:1, :]
    for j in range(1, CONV_W - 1):
        acc = acc + conv_ref[0][j:j + 1, :] * cw_ref[j:j + 1, :]
    c = _silu(acc + x_ref[0] * cw_ref[CONV_W - 1:CONV_W, :])
    sm = small_ref[0]
    g_all = -jnp.exp(alog_ref[...]) * _softplus(sm + dtb_ref[...])
    beta_all = jax.nn.sigmoid(sm)
    nk = B_HEADS * B_DK

    def as_columns(row):
        return jnp.broadcast_to(row, (row.shape[1], row.shape[1])).T

    for h in range(B_HEADS):
        q = _l2_rows(c[:, h * B_DK:(h + 1) * B_DK]) * B_DK ** -0.5
        k = _l2_rows(c[:, nk + h * B_DK:nk + (h + 1) * B_DK])
        v = c[:, 2 * nk + h * B_DV:2 * nk + (h + 1) * B_DV]
        kc, qc = as_columns(k), as_columns(q)
        s = s_ref[0, h] * jnp.exp(g_all[:, SM_A + h:SM_A + h + 1])
        u = beta_all[:, SM_B + h:SM_B + h + 1] * (v - jnp.sum(kc * s, axis=0, keepdims=True))
        s = s + kc * u
        s_out_ref[0, h] = s
        o = jnp.sum(qc * s, axis=0, keepdims=True)
        o = _rms_rows(o) * on_ref[...] * _silu(gate_ref[0][:, h * B_DV:(h + 1) * B_DV])
        o_ref[0, :, h * B_DV:(h + 1) * B_DV] = o


def _delta_step(state_conv, qkv_new, small, gate_b, conv_w, a_log, dt_bias, o_norm_g, state):
    db = qkv_new.shape[0]
    per_b = lambda *s: pl.BlockSpec((1,) + s, lambda b: (b,) + (0,) * len(s))
    return pl.pallas_call(
        _delta_step_kernel,
        grid=(db,),
        in_specs=[per_b(CONV_W - 1, CONV_CH), per_b(1, CONV_CH), per_b(1, SMALL_W), per_b(1, B_WIDTH),
                  _const_spec((CONV_W, CONV_CH)), _const_spec((1, LANES)), _const_spec((1, LANES)),
                  _const_spec((1, B_DV)), per_b(B_HEADS, B_DK, B_DV)],
        out_specs=[per_b(1, B_WIDTH), per_b(B_HEADS, B_DK, B_DV)],
        out_shape=[jax.ShapeDtypeStruct((db, 1, B_WIDTH), F32),
                   jax.ShapeDtypeStruct((db, B_HEADS, B_DK, B_DV), F32)],
        compiler_params=_cparams("arbitrary"),
        name="delta_step",
    )(state_conv, qkv_new.reshape(db, 1, CONV_CH), small.reshape(db, 1, SMALL_W), gate_b.reshape(db, 1, B_WIDTH),
      conv_w, _lane_slab(a_log, SM_A), _lane_slab(dt_bias, SM_A), o_norm_g.reshape(1, B_DV), state)


def _jx_rmsnorm(x, g):
    y = x * lax.rsqrt(jnp.mean(x * x, axis=-1, keepdims=True) + EPS)
    return y * g


def _jx_l2norm(x):
    return x * lax.rsqrt(jnp.sum(x * x, axis=-1, keepdims=True) + EPS)


def _jx_slopes():
    h = jnp.arange(1, A_HEADS + 1, dtype=F32)
    return jnp.exp2(-8.0 * h / A_HEADS)


def _jx_compress(rows, w, pe):
    B, L, K, D = rows.shape
    r = rows.reshape(B, L // CMP_STRIDE, CMP_STRIDE, K, D)
    first = jnp.einsum('bnjkd,jde->bnke', r, w[:CMP_STRIDE])
    second = jnp.einsum('bnjkd,jde->bnke', r, w[CMP_STRIDE:])
    bias = jnp.einsum('jd,jde->e', pe, w)
    return first[:, :-1] + second[:, 1:] + bias


def _jx_nsa_keys(rows, k_norm_g, w_cmp, pe_cmp):
    B, L = rows.shape[:2]
    nsb = L // SLC_BLOCK
    kc = _jx_rmsnorm(_jx_compress(rows[:, :, 0], w_cmp[0], pe_cmp[0]), k_norm_g[0])
    vc = _jx_compress(rows[:, :, 1], w_cmp[1], pe_cmp[1])
    cpos = jnp.arange(kc.shape[1], dtype=jnp.int32) * CMP_STRIDE + (CMP_BLOCK - 1)

    def blocks(r):
        return r.reshape(B, nsb, SLC_BLOCK, A_KV_HEADS, HEAD_DIM).transpose(0, 3, 1, 2, 4)

    ks = blocks(_jx_rmsnorm(rows[:, :, 2], k_norm_g[1]))
    vs = blocks(rows[:, :, 3])
    return kc, vc, cpos, ks, vs


def _jx_nsa_attend(q, qpos, kc, vc, cpos, ks, vs, kw, vw, wpos, gates):
    B, Q = q.shape[:2]
    qg = q.reshape(B, Q, A_KV_HEADS, A_GROUP, HEAD_DIM)
    sl = _jx_slopes().reshape(A_KV_HEADS, A_GROUP)[None, :, :, None, None]
    dist_c = qpos[:, None] - cpos[None, :]
    valid_c = dist_c >= 0
    s_c = jnp.einsum('bqkgd,bnkd->bkgqn', qg, kc).astype(F32)
    s_c = jnp.where(valid_c, s_c - sl * dist_c.astype(F32), NEG)
    p_c = jax.nn.softmax(s_c, axis=-1) * jnp.any(valid_c, axis=-1)[:, None].astype(F32)
    o_c = jnp.einsum('bkgqn,bnkd->bqkgd', p_c, vc)
    nsb = ks.shape[2]
    nc = kc.shape[1]
    pg = jnp.pad(p_c.sum(axis=2), ((0, 0), (0, 0), (0, 0), (0, RATIO * nsb - nc)))
    pg = pg.reshape(B, A_KV_HEADS, Q, nsb, RATIO)
    imp = pg.sum(-1) + jnp.pad(pg[..., :-1, RATIO - 1], ((0, 0), (0, 0), (0, 0), (1, 0)))
    jidx = jnp.arange(nsb, dtype=jnp.int32)
    cur = qpos // SLC_BLOCK
    forced = (jidx[None, :] == cur[:, None]) | (jidx[None, :] == 0)
    avail = jidx[None, :] <= cur[:, None]
    imp = jnp.where(forced, BIG, jnp.where(avail, imp, NEG))
    _, sel = lax.top_k(imp, min(SLC_TOPN, nsb))
    bi = jnp.arange(B)[:, None, None, None]
    ki = jnp.arange(A_KV_HEADS)[None, :, None, None]
    kg = ks[bi, ki, sel]
    vg = vs[bi, ki, sel]
    spos = sel[..., None] * SLC_BLOCK + jnp.arange(SLC_BLOCK, dtype=jnp.int32)
    dist_s = (qpos[:, None, None] - spos)[:, :, None]
    s_s = jnp.einsum('bqkgd,bkqnsd->bkgqns', qg, kg).astype(F32)
    s_s = jnp.where(dist_s >= 0, s_s - sl[..., None] * dist_s.astype(F32), NEG)
    shp = s_s.shape
    p_s = jax.nn.softmax(s_s.reshape(shp[:4] + (-1,)), axis=-1).reshape(shp)
    o_s = jnp.einsum('bkgqns,bkqnsd->bqkgd', p_s, vg)
    dist_w = qpos[:, None] - wpos[None, :]
    valid_w = (dist_w >= 0) & (dist_w <= WINDOW) & (wpos >= 0)[None, :]
    s_w = jnp.einsum('bqkgd,bnkd->bkgqn', qg, kw).astype(F32)
    s_w = jnp.where(valid_w, s_w - sl * dist_w.astype(F32), NEG)
    p_w = jax.nn.softmax(s_w, axis=-1)
    o_w = jnp.einsum('bkgqn,bnkd->bqkgd', p_w, vw)
    gr = gates.reshape(B, Q, A_KV_HEADS, A_GROUP, 3)
    o = gr[..., 0:1] * o_c + gr[..., 1:2] * o_s + gr[..., 2:3] * o_w
    return o.reshape(B, Q, A_WIDTH)


def _jx_nsa_prompt(q, gates, kv_rows, win_rows, lp):
    B, T = q.shape[:2]
    kc, vc, cpos, ks, vs = _jx_nsa_keys(kv_rows, lp['k_norm_g'], lp['w_cmp'], lp['pe_cmp'])
    kw = _jx_rmsnorm(win_rows[:, :, 0], lp['k_norm_g'][2])
    pad = ((0, 0), (WINDOW, 0), (0, 0), (0, 0))
    kw_pad = jnp.pad(kw, pad)
    vw_pad = jnp.pad(win_rows[:, :, 1], pad)

    def block(i):
        q0 = i * Q_BLOCK
        qb = lax.dynamic_slice_in_dim(q, q0, Q_BLOCK, axis=1)
        gb = lax.dynamic_slice_in_dim(gates, q0, Q_BLOCK, axis=1)
        qpos = q0 + jnp.arange(Q_BLOCK, dtype=jnp.int32)
        kwb = lax.dynamic_slice_in_dim(kw_pad, q0, WINDOW + Q_BLOCK, axis=1)
        vwb = lax.dynamic_slice_in_dim(vw_pad, q0, WINDOW + Q_BLOCK, axis=1)
        wpos = q0 - WINDOW + jnp.arange(WINDOW + Q_BLOCK, dtype=jnp.int32)
        return _jx_nsa_attend(qb, qpos, kc, vc, cpos, ks, vs, kwb, vwb, wpos, gb)

    o = lax.map(block, jnp.arange(T // Q_BLOCK, dtype=jnp.int32))
    return o.transpose(1, 0, 2, 3).reshape(B, T, A_WIDTH)


def _jx_nsa_sample(q, gates, kv_new, win_new, cache_kv, page_table, cache_win, lp):
    DB, S = q.shape[:2]
    P = page_table.shape[1] * cache_kv.shape[1]
    past = cache_kv[page_table].reshape(DB, P, N_KV_SLOTS, A_KV_HEADS, HEAD_DIM)
    rows = jnp.concatenate([past, kv_new], axis=1)
    L = P + S
    Lp = -(-L // SLC_BLOCK) * SLC_BLOCK
    rows = jnp.pad(rows, ((0, 0), (0, Lp - L), (0, 0), (0, 0), (0, 0)))
    kc, vc, cpos, ks, vs = _jx_nsa_keys(rows, lp['k_norm_g'], lp['w_cmp'], lp['pe_cmp'])
    nwb = cache_win.shape[1]
    win = jnp.concatenate([cache_win, win_new], axis=1)
    wpos = P - nwb + jnp.arange(nwb + S, dtype=jnp.int32)
    kw = _jx_rmsnorm(win[:, :, 0], lp['k_norm_g'][2])
    qpos = P + jnp.arange(S, dtype=jnp.int32)
    o = _jx_nsa_attend(q, qpos, kc, vc, cpos, ks, vs, kw, win[:, :, 1], wpos, gates)
    return o, win[:, S:]


def _jx_causal_conv(xpad, w):
    T = xpad.shape[1] - (CONV_W - 1)
    acc = xpad[:, 0:T] * w[0]
    for j in range(1, CONV_W):
        acc = acc + xpad[:, j:j + T] * w[j]
    return jax.nn.silu(acc)


def _jx_delta_inputs(qkv, a_b, b_b, a_log, dt_bias):
    B, T, _ = qkv.shape
    nk = B_HEADS * B_DK
    q = _jx_l2norm(qkv[..., :nk].reshape(B, T, B_HEADS, B_DK)) * B_DK ** -0.5
    k = _jx_l2norm(qkv[..., nk:2 * nk].reshape(B, T, B_HEADS, B_DK))
    v = qkv[..., 2 * nk:].reshape(B, T, B_HEADS, B_DV)
    g = -jnp.exp(a_log) * jax.nn.softplus(a_b + dt_bias)
    beta = jax.nn.sigmoid(b_b)
    return q, k, v, g, beta


def _jx_delta_chunked(q, k, v, g, beta):
    B, T, H, _ = q.shape
    C = DELTA_CHUNK
    N = T // C

    def chunks(x):
        return jnp.moveaxis(x.reshape((B, N, C) + x.shape[2:]), 2, 3)

    qc, kc, vc, gc, bc = chunks(q), chunks(k), chunks(v), chunks(g), chunks(beta)
    G = jnp.cumsum(gc, axis=-1)
    i = jnp.arange(C)
    incl = i[:, None] >= i[None, :]
    strict = i[:, None] > i[None, :]
    decay = jnp.exp(jnp.where(incl, G[..., :, None] - G[..., None, :], -jnp.inf))
    kk = jnp.einsum('bnhid,bnhjd->bnhij', kc, kc)
    a_mat = jnp.where(strict, kk * decay * bc[..., :, None], 0.0) + jnp.eye(C, dtype=F32)
    eg = jnp.exp(G)[..., None]
    rhs = jnp.concatenate([vc * bc[..., None], kc * bc[..., None] * eg], axis=-1)
    sol = lax.linalg.triangular_solve(a_mat, rhs, left_side=True, lower=True, unit_diagonal=True)
    u_c, w_c = sol[..., :B_DV], sol[..., B_DV:]
    qk = jnp.einsum('bnhid,bnhjd->bnhij', qc, kc) * decay
    q_dec = qc * eg
    k_dec = kc * jnp.exp(G[..., -1:] - G)[..., None]
    g_last = jnp.exp(G[..., -1])

    def step(s, inp):
        u_i, w_i, qk_i, qd_i, kd_i, gl_i = inp
        v_new = u_i - jnp.einsum('bhck,bhkv->bhcv', w_i, s)
        o = jnp.einsum('bhck,bhkv->bhcv', qd_i, s) + jnp.einsum('bhij,bhjv->bhiv', qk_i, v_new)
        s = s * gl_i[..., None, None] + jnp.einsum('bhck,bhcv->bhkv', kd_i, v_new)
        return s, o

    xs = (jnp.moveaxis(u_c, 1, 0), jnp.moveaxis(w_c, 1, 0), jnp.moveaxis(qk, 1, 0),
          jnp.moveaxis(q_dec, 1, 0), jnp.moveaxis(k_dec, 1, 0), jnp.moveaxis(g_last, 1, 0))
    s0 = jnp.zeros((B, H, B_DK, B_DV), F32)
    s_fin, o = lax.scan(step, s0, xs)
    o = jnp.moveaxis(jnp.moveaxis(o, 0, 1), 3, 2).reshape(B, T, H, B_DV)
    return o, s_fin


def _jx_delta_recurrent(s0, q, k, v, g, beta):
    def step(s, inp):
        q_t, k_t, v_t, g_t, b_t = inp
        s = s * jnp.exp(g_t)[..., None, None]
        u = b_t[..., None] * (v_t - jnp.einsum('bhk,bhkv->bhv', k_t, s))
        s = s + jnp.einsum('bhk,bhv->bhkv', k_t, u)
        return s, jnp.einsum('bhk,bhkv->bhv', q_t, s)

    xs = (jnp.moveaxis(q, 1, 0), jnp.moveaxis(k, 1, 0), jnp.moveaxis(v, 1, 0),
          jnp.moveaxis(g, 1, 0), jnp.moveaxis(beta, 1, 0))
    s_fin, o = lax.scan(step, s0, xs)
    return jnp.moveaxis(o, 0, 1), s_fin


def _jx_delta_output(o, gate_b, o_norm_g):
    B, T = o.shape[:2]
    gt = gate_b.reshape(B, T, B_HEADS, B_DV)
    y = _jx_rmsnorm(o, o_norm_g) * jax.nn.silu(gt)
    return y.reshape(B, T, B_WIDTH)


def _rearranged_w_in(w_in):
    o = IN_OFFSETS
    q_a, kv_a, g_a, qkv_b, a_b, b_b, gate_b, merge = (
        w_in[:, :o[0]], w_in[:, o[0]:o[1]], w_in[:, o[1]:o[2]], w_in[:, o[2]:o[3]],
        w_in[:, o[3]:o[4]], w_in[:, o[4]:o[5]], w_in[:, o[5]:o[6]], w_in[:, o[6]:])
    small = jnp.concatenate([g_a, a_b, b_b], axis=1)
    small = jnp.pad(small, ((0, 0), (0, SMALL_W - small.shape[1])))
    return jnp.concatenate([q_a, kv_a, qkv_b, gate_b, merge, small], axis=1)


def kernel(x_prompt, x_sample, cache_nsa_kv, page_table, cache_win_kv, state_conv, state_delta,
           c_prompt, c_sample, norm1_g, norm2_g, w_ada, b_ada, w_in, q_norm_g, k_norm_g, w_cmp, pe_cmp,
           conv_w, a_log, dt_bias, o_norm_g, w_proj_a, w_proj_b, w_out, w_router, b_router,
           w_exp_gu, w_exp_down, w_sh_gu, w_sh_down):
    T = x_prompt.shape[1]
    DB = x_sample.shape[0]
    assert x_prompt.shape[0] == 1 and x_sample.shape[1] == 1 and w_in.shape[0] == 1

    w_main_f = _rearranged_w_in(w_in[0])
    w_main = w_main_f.astype(BF16)
    wa, wb, wout = w_proj_a[0].astype(BF16), w_proj_b[0].astype(BF16), w_out[0].astype(BF16)
    wr_f = jnp.pad(w_router[0], ((0, 0), (0, LANES - N_EXPERTS)))
    wr = wr_f.astype(BF16)
    br = jnp.pad(b_router[0], (0, LANES - N_EXPERTS)).reshape(1, LANES)
    wgu = w_exp_gu[0].astype(BF16)
    wd2 = w_exp_down[0].astype(BF16).reshape(N_EXPERTS // 2, 2 * D_EXPERT, D_MODEL)
    wsgu, wsd = w_sh_gu[0].astype(BF16), w_sh_down[0].astype(BF16)
    g1, g2 = norm1_g[0].reshape(1, -1), norm2_g[0].reshape(1, -1)

    c_all = jnp.concatenate([c_prompt, jnp.zeros((7, D_MODEL), F32), c_sample], axis=0)
    mod = _adaln(c_all, w_ada[0], b_ada[0])
    mp = [mod[0:1, i * D_MODEL:(i + 1) * D_MODEL] for i in range(6)]
    ms = [mod[8:8 + DB, i * D_MODEL:(i + 1) * D_MODEL] for i in range(6)]

    xp = x_prompt.reshape(T, D_MODEL)
    q_raw, kv, win, qkv_b, gate_b, msig, small = _in_proj(xp, g1, mp[1], mp[0], w_main, 256)
    o_a = _nsa_prompt(q_raw, kv, win, small, q_norm_g[0], k_norm_g[0], w_cmp[0], pe_cmp[0])
    o_b, s_fin_p = _delta_prompt(qkv_b, small, gate_b, conv_w[0], a_log[0], dt_bias[0], o_norm_g[0])
    x1, h2, scores = _merge(xp, o_a, o_b, msig, mp[2], g2, mp[4], mp[3], wa, wb, wout, wr, 256)
    y_prompt = _moe(h2, scores, br, x1, mp[5], wgu, wd2, wsgu, wsd, 1024).reshape(1, T, D_MODEL)
    kv_prompt = kv.reshape(1, 1, T, N_KV_SLOTS, A_KV_HEADS, HEAD_DIM)
    win_prompt = win[T - min(WINDOW, T):].reshape(1, 1, -1, 2, A_KV_HEADS, HEAD_DIM)
    conv_prompt = qkv_b[T - (CONV_W - 1):].reshape(1, 1, CONV_W - 1, CONV_CH)
    delta_prompt = s_fin_p.reshape(1, 1, B_HEADS, B_DK, B_DV)

    xs = x_sample.reshape(DB, D_MODEL)
    q_raw, kv, win, qkv_b, gate_b, msig, small = _in_proj(xs, g1, ms[1], ms[0], w_main_f, DB)
    o_a = _nsa_sample(q_raw, kv, win, small, cache_nsa_kv[0], page_table, cache_win_kv[0],
                      q_norm_g[0], k_norm_g[0], w_cmp[0], pe_cmp[0])
    o_b, s_fin_s = _delta_step(state_conv[0], qkv_b, small, gate_b, conv_w[0], a_log[0], dt_bias[0], o_norm_g[0],
                               state_delta[0])
    x1, h2, scores = _merge(xs, o_a, o_b.reshape(DB, B_WIDTH), msig, ms[2], g2, ms[4], ms[3],
                            w_proj_a[0], w_proj_b[0], w_out[0], wr_f, DB)
    y_sample = _moe(h2, scores, br, x1, ms[5], wgu, wd2, wsgu, wsd, DB).reshape(DB, 1, D_MODEL)
    kv_sample = kv.reshape(1, DB, 1, N_KV_SLOTS, A_KV_HEADS, HEAD_DIM)
    win_sample = jnp.concatenate([cache_win_kv[0][:, 1:], win.reshape(DB, 1, 2, A_KV_HEADS, HEAD_DIM)], axis=1)[None]
    conv_sample = jnp.concatenate([state_conv[0][:, 1:], qkv_b.reshape(DB, 1, CONV_CH)], axis=1)[None]

    return (y_prompt, y_sample, kv_prompt, win_prompt, conv_prompt, delta_prompt,
            kv_sample, win_sample, conv_sample, s_fin_s[None])
```

```python
import functools
import math

import jax
import jax.numpy as jnp
import numpy as np
from jax import lax
from jax.experimental import pallas as pl
from jax.experimental.pallas import tpu as pltpu

F32 = jnp.float32
BF16 = jnp.bfloat16
HIGHEST = lax.Precision.HIGHEST

D_MODEL = 1024
A_HEADS = 8
A_KV_HEADS = 2
A_GROUP = A_HEADS // A_KV_HEADS
HEAD_DIM = 64
CMP_STRIDE = 16
CMP_BLOCK = 32
SLC_BLOCK = 64
RATIO = SLC_BLOCK // CMP_STRIDE
SLC_TOPN = 16
WINDOW = 512
Q_BLOCK = 128
N_KV_SLOTS = 4
B_HEADS = 4
B_DK = 128
B_DV = 128
CONV_W = 4
DELTA_CHUNK = 64
N_EXPERTS = 64
TOP_K = 6
D_EXPERT = 128
D_SHARED = 128
ROUTED_SCALE = 2.5
EPS = 1e-6
NEG = -1e30
BIG = 1e30

A_WIDTH = A_HEADS * HEAD_DIM
B_WIDTH = B_HEADS * B_DV
CONV_CH = 2 * B_HEADS * B_DK + B_HEADS * B_DV
KV_WIDTH = 6 * A_KV_HEADS * HEAD_DIM
IN_SPLITS = (A_WIDTH, KV_WIDTH, 3 * A_HEADS, CONV_CH, B_HEADS, B_HEADS, B_WIDTH, 2 * D_MODEL)
IN_OFFSETS = tuple(int(v) for v in np.cumsum(IN_SPLITS)[:-1])

LANES = 128
SMALL_W = LANES
C_Q = 0
C_KV = C_Q + A_WIDTH
C_QKVB = C_KV + KV_WIDTH
C_GATEB = C_QKVB + CONV_CH
C_MERGE = C_GATEB + B_WIDTH
C_SMALL = C_MERGE + 2 * D_MODEL
W_MAIN = C_SMALL + SMALL_W

VMEM_LIMIT = 56 * 1024 * 1024


def _cparams(*sem):
    return pltpu.CompilerParams(dimension_semantics=sem, vmem_limit_bytes=VMEM_LIMIT)


def _const_spec(shape, single=False):
    nd = len(shape)
    if single:
        return pl.BlockSpec(shape, lambda *_: (0,) * nd, pipeline_mode=pl.Buffered(1))
    return pl.BlockSpec(shape, lambda *_: (0,) * nd)


def _row_spec(tm, width, rows):
    if rows == 1:
        return pl.BlockSpec((1, width), lambda i: (0, 0))
    return pl.BlockSpec((tm, width), lambda i: (i, 0))


def _silu(x):
    return x * jax.nn.sigmoid(x)


def _rms_rows(x):
    return x * lax.rsqrt(jnp.mean(x * x, axis=-1, keepdims=True) + EPS)


def _mm(a, b, exact):
    if exact:
        return jnp.dot(a.astype(F32), b, preferred_element_type=F32, precision=HIGHEST)
    return jnp.dot(a.astype(BF16), b, preferred_element_type=F32)


def _split3(x):
    hi = x.astype(BF16)
    lo = (x - hi.astype(F32)).astype(BF16)
    return jnp.concatenate([hi, lo, hi], axis=1)


def _stack3(x):
    hi = x.astype(BF16)
    lo = (x - hi.astype(F32)).astype(BF16)
    return jnp.concatenate([hi, hi, lo], axis=0)


def _mm3(a, b3):
    return jnp.dot(_split3(a), b3, preferred_element_type=F32)


def _ada_kernel(c_ref, w_ref, b_ref, o_ref):
    o_ref[...] = _mm(_silu(c_ref[...]), w_ref[...], True) + b_ref[...]


def _adaln(c_all, w_ada, b_ada):
    rows = c_all.shape[0]
    tn = 1024
    return pl.pallas_call(
        _ada_kernel,
        grid=(6 * D_MODEL // tn,),
        in_specs=[pl.BlockSpec((rows, D_MODEL), lambda j: (0, 0)),
                  pl.BlockSpec((D_MODEL, tn), lambda j: (0, j)),
                  pl.BlockSpec((1, tn), lambda j: (0, j))],
        out_specs=pl.BlockSpec((rows, tn), lambda j: (0, j)),
        out_shape=jax.ShapeDtypeStruct((rows, 6 * D_MODEL), F32),
        compiler_params=_cparams("arbitrary"),
        name="adaln",
    )(c_all, w_ada, b_ada.reshape(1, -1))


def _in_kernel(x_ref, g_ref, sc_ref, sh_ref, w_ref,
               q_ref, kv_ref, win_ref, qkvb_ref, gateb_ref, merge_ref, small_ref, *, exact):
    h = _rms_rows(x_ref[...]) * g_ref[...]
    h = h * (1.0 + sc_ref[...]) + sh_ref[...]
    if not exact:
        h = h.astype(BF16)

    def proj(c0, width):
        return _mm(h, w_ref[:, c0:c0 + width], exact)

    q_ref[...] = proj(C_Q, A_WIDTH)
    kv_ref[...] = proj(C_KV, N_KV_SLOTS * A_KV_HEADS * HEAD_DIM)
    win_ref[...] = proj(C_KV + N_KV_SLOTS * A_KV_HEADS * HEAD_DIM, 2 * A_KV_HEADS * HEAD_DIM)
    qkvb_ref[...] = proj(C_QKVB, CONV_CH)
    gateb_ref[...] = proj(C_GATEB, B_WIDTH)
    merge_ref[...] = jax.nn.sigmoid(proj(C_MERGE, 2 * D_MODEL))
    small_ref[...] = proj(C_SMALL, SMALL_W)


def _in_proj(x, g1, sc, sh, w_main, tm):
    t = x.shape[0]
    widths = (A_WIDTH, 4 * A_KV_HEADS * HEAD_DIM, 2 * A_KV_HEADS * HEAD_DIM, CONV_CH, B_WIDTH,
              2 * D_MODEL, SMALL_W)
    return pl.pallas_call(
        functools.partial(_in_kernel, exact=w_main.dtype == F32),
        grid=(t // tm,),
        in_specs=[pl.BlockSpec((tm, D_MODEL), lambda i: (i, 0)),
                  _const_spec((1, D_MODEL)),
                  _row_spec(tm, D_MODEL, sc.shape[0]),
                  _row_spec(tm, D_MODEL, sh.shape[0]),
                  _const_spec((D_MODEL, W_MAIN), single=True)],
        out_specs=[pl.BlockSpec((tm, w), lambda i: (i, 0)) for w in widths],
        out_shape=[jax.ShapeDtypeStruct((t, w), F32) for w in widths],
        compiler_params=_cparams("arbitrary"),
        name="in_proj",
    )(x, g1, sc, sh, w_main)


def _merge_kernel(x_ref, oa_ref, ob_ref, msig_ref, gt1_ref, g2_ref, sc2_ref, sh2_ref,
                  wa_ref, wb_ref, wout_ref, wr_ref, x1_ref, h2_ref, score_ref, *, exact):
    pa = _mm(oa_ref[...], wa_ref[...], exact)
    pb = _mm(ob_ref[...], wb_ref[...], exact)
    m = msig_ref[:, :D_MODEL] * pa + msig_ref[:, D_MODEL:] * pb
    y = _mm(m, wout_ref[...], exact)
    x1 = x_ref[...] + gt1_ref[...] * y
    x1_ref[...] = x1
    h2 = _rms_rows(x1) * g2_ref[...]
    h2 = h2 * (1.0 + sc2_ref[...]) + sh2_ref[...]
    h2_ref[...] = h2.astype(BF16)
    score_ref[...] = jax.nn.sigmoid(_mm(h2, wr_ref[...], exact))


def _merge(x, o_a, o_b, msig, gt1, g2, sc2, sh2, wa, wb, wout, wr, tm):
    t = x.shape[0]
    return pl.pallas_call(
        functools.partial(_merge_kernel, exact=wout.dtype == F32),
        grid=(t // tm,),
        in_specs=[pl.BlockSpec((tm, D_MODEL), lambda i: (i, 0)),
                  pl.BlockSpec((tm, A_WIDTH), lambda i: (i, 0)),
                  pl.BlockSpec((tm, B_WIDTH), lambda i: (i, 0)),
                  pl.BlockSpec((tm, 2 * D_MODEL), lambda i: (i, 0)),
                  _row_spec(tm, D_MODEL, gt1.shape[0]),
                  _const_spec((1, D_MODEL)),
                  _row_spec(tm, D_MODEL, sc2.shape[0]),
                  _row_spec(tm, D_MODEL, sh2.shape[0]),
                  _const_spec((A_WIDTH, D_MODEL)),
                  _const_spec((B_WIDTH, D_MODEL)),
                  _const_spec((D_MODEL, D_MODEL)),
                  _const_spec((D_MODEL, LANES))],
        out_specs=[pl.BlockSpec((tm, D_MODEL), lambda i: (i, 0)),
                   pl.BlockSpec((tm, D_MODEL), lambda i: (i, 0)),
                   pl.BlockSpec((tm, LANES), lambda i: (i, 0))],
        out_shape=[jax.ShapeDtypeStruct((t, D_MODEL), F32),
                   jax.ShapeDtypeStruct((t, D_MODEL), BF16),
                   jax.ShapeDtypeStruct((t, LANES), F32)],
        compiler_params=_cparams("arbitrary"),
        name="merge_out",
    )(x, o_a, o_b, msig, gt1, g2, sc2, sh2, wa, wb, wout, wr)


def _route(scores, bias):
    lane = lax.broadcasted_iota(jnp.int32, scores.shape, 1)
    live = lane < N_EXPERTS
    v = jnp.where(live, scores + bias, -jnp.inf)
    sel = jnp.zeros(scores.shape, jnp.bool_)
    for _ in range(TOP_K):
        m = jnp.max(v, axis=-1, keepdims=True)
        first = jnp.min(jnp.where(v == m, lane, LANES), axis=-1, keepdims=True)
        hit = lane == first
        sel = jnp.logical_or(sel, hit)
        v = jnp.where(hit, -jnp.inf, v)
    picked = jnp.where(sel, scores, 0.0)
    return picked / jnp.sum(picked, axis=-1, keepdims=True) * ROUTED_SCALE


def _moe_kernel(h2_ref, score_ref, bias_ref, x1_ref, gt2_ref, wgu_ref, wd_ref, wsgu_ref, wsd_ref,
                out_ref, gate_ref, acc_ref):
    p = pl.program_id(1)
    h2 = h2_ref[...]

    @pl.when(p == 0)
    def _():
        gate = _route(score_ref[...], bias_ref[...])
        hi = gate.astype(BF16)
        lo = (gate - hi.astype(F32)).astype(BF16)
        gate_ref[...] = jnp.concatenate([hi, lo], axis=-1)
        s = jnp.dot(h2, wsgu_ref[...], preferred_element_type=F32)
        sact = _silu(s[:, :D_SHARED]) * s[:, D_SHARED:]
        acc_ref[...] = jnp.dot(sact.astype(BF16), wsd_ref[...], preferred_element_type=F32)

    row = lax.broadcasted_iota(jnp.int32, (2 * LANES, 2 * D_EXPERT), 0) % LANES
    col = lax.broadcasted_iota(jnp.int32, (2 * LANES, 2 * D_EXPERT), 1) // D_EXPERT
    onehot = jnp.where(row == 2 * p + col, 1.0, 0.0).astype(BF16)
    gsel = jnp.dot(gate_ref[...], onehot, preferred_element_type=F32)

    acts = []
    for e in range(2):
        au = jnp.dot(h2, wgu_ref[e], preferred_element_type=F32)
        acts.append(_silu(au[:, :D_EXPERT]) * au[:, D_EXPERT:])
    act = (jnp.concatenate(acts, axis=-1) * gsel).astype(BF16)
    acc_ref[...] += jnp.dot(act, wd_ref[0], preferred_element_type=F32)

    @pl.when(p == pl.num_programs(1) - 1)
    def _():
        out_ref[...] = x1_ref[...] + gt2_ref[...] * acc_ref[...]


def _moe(h2, scores, bias, x1, gt2, wgu, wd2, wsgu, wsd, tm):
    t = h2.shape[0]
    npairs = N_EXPERTS // 2
    return pl.pallas_call(
        _moe_kernel,
        grid=(t // tm, npairs),
        in_specs=[pl.BlockSpec((tm, D_MODEL), lambda i, p: (i, 0)),
                  pl.BlockSpec((tm, LANES), lambda i, p: (i, 0)),
                  pl.BlockSpec((1, LANES), lambda i, p: (0, 0)),
                  pl.BlockSpec((tm, D_MODEL), lambda i, p: (i, 0)),
                  (pl.BlockSpec((1, D_MODEL), lambda i, p: (0, 0)) if gt2.shape[0] == 1
                   else pl.BlockSpec((tm, D_MODEL), lambda i, p: (i, 0))),
                  pl.BlockSpec((2, D_MODEL, 2 * D_EXPERT), lambda i, p: (p, 0, 0)),
                  pl.BlockSpec((1, 2 * D_EXPERT, D_MODEL), lambda i, p: (p, 0, 0)),
                  pl.BlockSpec((D_MODEL, 2 * D_SHARED), lambda i, p: (0, 0)),
                  pl.BlockSpec((D_SHARED, D_MODEL), lambda i, p: (0, 0))],
        out_specs=pl.BlockSpec((tm, D_MODEL), lambda i, p: (i, 0)),
        out_shape=jax.ShapeDtypeStruct((t, D_MODEL), F32),
        scratch_shapes=[pltpu.VMEM((tm, 2 * LANES), BF16), pltpu.VMEM((tm, D_MODEL), F32)],
        compiler_params=_cparams("parallel", "arbitrary"),
        name="moe",
    )(h2, scores, bias, x1, gt2, wgu, wd2, wsgu, wsd)


SEL_TILE = 512
SEL_TILE_BLOCKS = SEL_TILE // SLC_BLOCK
WIN_TILES = WINDOW // Q_BLOCK + 1
CMP_ROWS = CMP_STRIDE * N_KV_SLOTS * A_KV_HEADS * HEAD_DIM
CMP_COLS = 2 * A_KV_HEADS * HEAD_DIM


def _head_slope(h):
    return float(2.0 ** (-8.0 * (h + 1) / A_HEADS))


def _compress_kernel(x_ref, w1_ref, w2_ref, pe1_ref, pe2_ref, gk_ref, kc_ref, vct_ref, f1_ref, f2_ref, vc_ref,
                     *, nb, tile):
    i = pl.program_id(0)
    nsb = nb // RATIO

    half = A_KV_HEADS * HEAD_DIM

    @pl.when(i == 0)
    def _():
        for s in range(2):
            f2_ref[s, pl.ds(nb, 8), :] = jnp.zeros((8, half), F32)

    x = x_ref[...].astype(BF16)
    row0 = pl.multiple_of(i * tile, tile)
    first = jnp.dot(x, w1_ref[...], preferred_element_type=F32)
    second = jnp.dot(x, w2_ref[...], preferred_element_type=F32)
    for s in range(2):
        f1_ref[s, pl.ds(row0, tile), :] = first[:, s * half:(s + 1) * half]
        f2_ref[s, pl.ds(row0, tile), :] = second[:, s * half:(s + 1) * half]

    @pl.when(i == pl.num_programs(0) - 1)
    def _():
        bias = (jnp.dot(pe1_ref[...], w1_ref[...], preferred_element_type=F32)
                + jnp.dot(pe2_ref[...], w2_ref[...], preferred_element_type=F32))[0:1]
        for r in range(RATIO):
            blk = [f1_ref[s, pl.ds(r, nsb, stride=RATIO), :] + f2_ref[s, pl.ds(r + 1, nsb, stride=RATIO), :]
                   + bias[:, s * half:(s + 1) * half] for s in range(2)]
            for k in range(A_KV_HEADS):
                kc = _rms_rows(blk[0][:, k * HEAD_DIM:(k + 1) * HEAD_DIM]) * gk_ref[...]
                kc_ref[k, pl.ds(r * nsb, nsb), :] = kc.astype(BF16)
            vc_ref[pl.ds(r * nsb, nsb), :] = blk[1]
        vct = vc_ref[...].T
        for k in range(A_KV_HEADS):
            vct_ref[k] = vct[k * HEAD_DIM:(k + 1) * HEAD_DIM].astype(BF16)


def _nsa_compress(kv_rows, w1, w2, pe1, pe2, gk):
    t = kv_rows.shape[0]
    nb = t // CMP_STRIDE
    tile = min(256, nb)
    x16 = kv_rows.reshape(nb, CMP_ROWS)
    return pl.pallas_call(
        functools.partial(_compress_kernel, nb=nb, tile=tile),
        grid=(nb // tile,),
        in_specs=[pl.BlockSpec((tile, CMP_ROWS), lambda i: (i, 0)),
                  _const_spec((CMP_ROWS, CMP_COLS)), _const_spec((CMP_ROWS, CMP_COLS)),
                  _const_spec((16, CMP_ROWS)), _const_spec((16, CMP_ROWS)), _const_spec((1, HEAD_DIM))],
        out_specs=[_const_spec((A_KV_HEADS, nb, HEAD_DIM)), _const_spec((A_KV_HEADS, HEAD_DIM, nb))],
        out_shape=[jax.ShapeDtypeStruct((A_KV_HEADS, nb, HEAD_DIM), BF16),
                   jax.ShapeDtypeStruct((A_KV_HEADS, HEAD_DIM, nb), BF16)],
        scratch_shapes=[pltpu.VMEM((2, nb + 8, CMP_COLS // 2), F32), pltpu.VMEM((2, nb + 8, CMP_COLS // 2), F32),
                        pltpu.VMEM((nb, A_KV_HEADS * HEAD_DIM), F32)],
        compiler_params=_cparams("arbitrary"),
        name="nsa_compress",
    )(x16, w1, w2, pe1, pe2, gk)


def _nsa_rows_kernel(kv_ref, win_ref, gs_ref, gw_ref, ks_ref, vst_ref, kw_ref, vwt_ref):
    half = A_KV_HEADS * HEAD_DIM
    kv = kv_ref[...]
    win = win_ref[...]
    vst = kv[:, 3 * half:4 * half].T
    vwt = win[:, half:2 * half].T
    for k in range(A_KV_HEADS):
        lo, hi = k * HEAD_DIM, (k + 1) * HEAD_DIM
        ks_ref[k, 0] = (_rms_rows(kv[:, 2 * half + lo:2 * half + hi]) * gs_ref[...]).astype(BF16)
        kw = (_rms_rows(win[:, lo:hi]) * gw_ref[...]).astype(BF16)
        for j in range(SEL_TILE // Q_BLOCK):
            kw_ref[k, j] = kw[j * Q_BLOCK:(j + 1) * Q_BLOCK]
            vwt_ref[k, j] = vwt[lo:hi, j * Q_BLOCK:(j + 1) * Q_BLOCK].astype(BF16)
        vst_ref[k, 0] = vst[lo:hi].astype(BF16)


def _nsa_rows(kv_rows, win_rows, gs, gw):
    t = kv_rows.shape[0]
    nt = t // SEL_TILE
    per = SEL_TILE // Q_BLOCK
    return pl.pallas_call(
        _nsa_rows_kernel,
        grid=(nt,),
        in_specs=[pl.BlockSpec((SEL_TILE, kv_rows.shape[1]), lambda i: (i, 0)),
                  pl.BlockSpec((SEL_TILE, win_rows.shape[1]), lambda i: (i, 0)),
                  _const_spec((1, HEAD_DIM)), _const_spec((1, HEAD_DIM))],
        out_specs=[pl.BlockSpec((A_KV_HEADS, 1, SEL_TILE, HEAD_DIM), lambda i: (0, i, 0, 0)),
                   pl.BlockSpec((A_KV_HEADS, 1, HEAD_DIM, SEL_TILE), lambda i: (0, i, 0, 0)),
                   pl.BlockSpec((A_KV_HEADS, per, Q_BLOCK, HEAD_DIM), lambda i: (0, i, 0, 0)),
                   pl.BlockSpec((A_KV_HEADS, per, HEAD_DIM, Q_BLOCK), lambda i: (0, i, 0, 0))],
        out_shape=[jax.ShapeDtypeStruct((A_KV_HEADS, nt, SEL_TILE, HEAD_DIM), BF16),
                   jax.ShapeDtypeStruct((A_KV_HEADS, nt, HEAD_DIM, SEL_TILE), BF16),
                   jax.ShapeDtypeStruct((A_KV_HEADS, nt * per, Q_BLOCK, HEAD_DIM), BF16),
                   jax.ShapeDtypeStruct((A_KV_HEADS, nt * per, HEAD_DIM, Q_BLOCK), BF16)],
        compiler_params=_cparams("arbitrary"),
        name="nsa_rows",
    )(kv_rows, win_rows, gs, gw)


def _nsa_cmp_kernel(q_ref, gq_ref, kc_ref, vct_ref, qt_ref, oct_ref, sel_ref, any_ref, *, nb):
    nsb = nb // RATIO
    shift = nsb.bit_length() - 1
    q0 = pl.program_id(0) * Q_BLOCK
    qt_full = q_ref[...].T
    crow = lax.broadcasted_iota(jnp.int32, (nb, Q_BLOCK), 0)
    cpos = (((crow & (nsb - 1)) * RATIO + (crow >> shift)) * CMP_STRIDE) + (CMP_BLOCK - 1)
    qpos = q0 + lax.broadcasted_iota(jnp.int32, (nb, Q_BLOCK), 1)
    dist = qpos - cpos
    valid = dist >= 0
    distf = dist.astype(F32)
    qpos_row = q0 + lax.broadcasted_iota(jnp.int32, (1, Q_BLOCK), 1)
    any_valid = jnp.where(qpos_row >= CMP_BLOCK - 1, 1.0, 0.0)
    jrow = lax.broadcasted_iota(jnp.int32, (nsb, Q_BLOCK), 0)
    cur = (q0 + lax.broadcasted_iota(jnp.int32, (nsb, Q_BLOCK), 1)) >> (SLC_BLOCK.bit_length() - 1)
    forced = jnp.logical_or(jrow == cur, jrow == 0)
    avail = jrow <= cur

    for k in range(A_KV_HEADS):
        slabs = []
        for g in range(A_GROUP):
            h = k * A_GROUP + g
            slab = qt_full[h * HEAD_DIM:(h + 1) * HEAD_DIM]
            inv = lax.rsqrt(jnp.mean(slab * slab, axis=0, keepdims=True) + EPS)
            slabs.append(slab * inv * gq_ref[...] * HEAD_DIM ** -0.5)
        qt = jnp.concatenate(slabs, axis=1).astype(BF16)
        qt_ref[0, k] = qt
        st = jnp.dot(kc_ref[k], qt, preferred_element_type=F32)
        pg = jnp.zeros((nb, Q_BLOCK), F32)
        ps = []
        for g in range(A_GROUP):
            s = st[:, g * Q_BLOCK:(g + 1) * Q_BLOCK]
            s = jnp.where(valid, s - _head_slope(k * A_GROUP + g) * distf, NEG)
            e = jnp.exp(s - jnp.max(s, axis=0, keepdims=True))
            p = e / jnp.sum(e, axis=0, keepdims=True) * any_valid
            pg = pg + p
            ps.append(p.astype(BF16))
        oct_ref[0, k] = jnp.dot(vct_ref[k], jnp.concatenate(ps, axis=1), preferred_element_type=F32)
        last = pg[3 * nsb:4 * nsb]
        prev = jnp.where(jrow == 0, 0.0, pltpu.roll(last, 1, axis=0))
        imp = pg[0:nsb] + pg[nsb:2 * nsb] + pg[2 * nsb:3 * nsb] + last + prev
        x = jnp.where(forced, BIG, jnp.where(avail, imp, NEG))
        picked = jnp.zeros((nsb, Q_BLOCK), jnp.bool_)
        for _ in range(SLC_TOPN):
            m = jnp.max(x, axis=0, keepdims=True)
            first = jnp.min(jnp.where(x == m, jrow, nsb), axis=0, keepdims=True)
            hit = jrow == first
            picked = jnp.logical_or(picked, hit)
            x = jnp.where(hit, -jnp.inf, x)
        sel = jnp.where(jnp.logical_and(picked, avail), 1.0, 0.0)
        sel_ref[0, k] = sel
        any_ref[0, k] = jnp.max(sel, axis=1, keepdims=True)


def _nsa_cmp(q_raw, gq_col, kc, vct):
    t = q_raw.shape[0]
    nqb = t // Q_BLOCK
    nb = kc.shape[1]
    nsb = nb // RATIO
    shp = (nqb, A_KV_HEADS, HEAD_DIM, A_GROUP * Q_BLOCK)
    blk = lambda *s: pl.BlockSpec((1,) + s, lambda i: (i,) + (0,) * len(s))
    return pl.pallas_call(
        functools.partial(_nsa_cmp_kernel, nb=nb),
        grid=(nqb,),
        in_specs=[pl.BlockSpec((Q_BLOCK, A_WIDTH), lambda i: (i, 0)),
                  _const_spec((HEAD_DIM, 1)),
                  _const_spec((A_KV_HEADS, nb, HEAD_DIM)),
                  _const_spec((A_KV_HEADS, HEAD_DIM, nb))],
        out_specs=[blk(*shp[1:]), blk(*shp[1:]), blk(A_KV_HEADS, nsb, Q_BLOCK), blk(A_KV_HEADS, nsb, 1)],
        out_shape=[jax.ShapeDtypeStruct(shp, BF16), jax.ShapeDtypeStruct(shp, F32),
                   jax.ShapeDtypeStruct((nqb, A_KV_HEADS, nsb, Q_BLOCK), F32),
                   jax.ShapeDtypeStruct((nqb, A_KV_HEADS, nsb, 1), F32)],
        compiler_params=_cparams("arbitrary"),
        name="nsa_compressed",
    )(q_raw, gq_col, kc, vct)


def _nsa_sel_kernel(tiles_ref, counts_ref, qt_ref, oct_ref, sel_ref, small_ref, ks_ref, vst_ref, kw_ref, vwt_ref,
                    o_ref, *, ntiles):
    qb = pl.program_id(0)
    q0 = qb * Q_BLOCK
    gates_t = jax.nn.sigmoid(small_ref[...]).T
    qpos = q0 + lax.broadcasted_iota(jnp.int32, (SEL_TILE, Q_BLOCK), 1)
    krow = lax.broadcasted_iota(jnp.int32, (SEL_TILE, Q_BLOCK), 0)
    wrow = lax.broadcasted_iota(jnp.int32, (WIN_TILES * Q_BLOCK, Q_BLOCK), 0)
    wdist = (q0 + lax.broadcasted_iota(jnp.int32, (WIN_TILES * Q_BLOCK, Q_BLOCK), 1)) - (q0 - WINDOW + wrow)
    wvalid = jnp.logical_and(jnp.logical_and(wdist >= 0, wdist <= WINDOW), q0 - WINDOW + wrow >= 0)
    wdistf = wdist.astype(F32)
    outs = []
    for k in range(A_KV_HEADS):
        qt = qt_ref[0, k]

        def tile_step(i, carry, k=k, qt=qt):
            m_prev, l_prev, acc = carry
            t = tiles_ref[(qb * A_KV_HEADS + k) * ntiles + i]
            st = jnp.dot(ks_ref[k, t], qt, preferred_element_type=F32)
            mrows = sel_ref[0, k, pl.ds(pl.multiple_of(t * SEL_TILE_BLOCKS, SEL_TILE_BLOCKS), SEL_TILE_BLOCKS), :]
            picked = jnp.concatenate(
                [jnp.broadcast_to(mrows[b:b + 1], (SLC_BLOCK, Q_BLOCK)) for b in range(SEL_TILE_BLOCKS)], axis=0)
            dist = qpos - (t * SEL_TILE + krow)
            valid = jnp.logical_and(picked > 0.5, dist >= 0)
            distf = dist.astype(F32)
            ms, ls, ps, alphas = [], [], [], []
            for g in range(A_GROUP):
                lanes = slice(g * Q_BLOCK, (g + 1) * Q_BLOCK)
                s = jnp.where(valid, st[:, lanes] - _head_slope(k * A_GROUP + g) * distf, NEG)
                m_new = jnp.maximum(m_prev[:, lanes], jnp.max(s, axis=0, keepdims=True))
                alpha = jnp.exp(m_prev[:, lanes] - m_new)
                p = jnp.where(valid, jnp.exp(s - m_new), 0.0)
                ms.append(m_new)
                ls.append(alpha * l_prev[:, lanes] + jnp.sum(p, axis=0, keepdims=True))
                alphas.append(alpha)
                ps.append(p.astype(BF16))
            pv = jnp.dot(vst_ref[k, t], jnp.concatenate(ps, axis=1), preferred_element_type=F32)
            return (jnp.concatenate(ms, axis=1), jnp.concatenate(ls, axis=1),
                    jnp.concatenate(alphas, axis=1) * acc + pv)

        init = (jnp.full((1, A_GROUP * Q_BLOCK), NEG, F32), jnp.zeros((1, A_GROUP * Q_BLOCK), F32),
                jnp.zeros((HEAD_DIM, A_GROUP * Q_BLOCK), F32))
        _, l_fin, acc = lax.fori_loop(0, counts_ref[qb * A_KV_HEADS + k], tile_step, init)
        os_t = acc / l_fin

        wk, wv = [], []
        for j in range(WIN_TILES):
            ti = jnp.maximum(qb - (WIN_TILES - 1) + j, 0)
            wk.append(kw_ref[k, ti])
            wv.append(vwt_ref[k, ti])
        sw = jnp.dot(jnp.concatenate(wk, axis=0), qt, preferred_element_type=F32)
        pw = []
        for g in range(A_GROUP):
            s = jnp.where(wvalid, sw[:, g * Q_BLOCK:(g + 1) * Q_BLOCK] - _head_slope(k * A_GROUP + g) * wdistf, NEG)
            e = jnp.exp(s - jnp.max(s, axis=0, keepdims=True))
            pw.append((e / jnp.sum(e, axis=0, keepdims=True)).astype(BF16))
        ow_t = jnp.dot(jnp.concatenate(wv, axis=1), jnp.concatenate(pw, axis=1), preferred_element_type=F32)

        oc_t = oct_ref[0, k]
        for g in range(A_GROUP):
            h = k * A_GROUP + g
            lanes = slice(g * Q_BLOCK, (g + 1) * Q_BLOCK)
            outs.append(gates_t[3 * h:3 * h + 1] * oc_t[:, lanes] + gates_t[3 * h + 1:3 * h + 2] * os_t[:, lanes]
                        + gates_t[3 * h + 2:3 * h + 3] * ow_t[:, lanes])
    o_ref[...] = jnp.concatenate(outs, axis=0).T


def _nsa_sel(tiles, counts, qt, oct, sel, small, ks, vst, kw, vwt):
    nqb = qt.shape[0]
    t = nqb * Q_BLOCK
    ntiles = ks.shape[1]
    nsb = sel.shape[2]
    shp = (A_KV_HEADS, HEAD_DIM, A_GROUP * Q_BLOCK)
    full = lambda a: pl.BlockSpec(a.shape, lambda i, *_: (0,) * a.ndim)
    grid_spec = pltpu.PrefetchScalarGridSpec(
        num_scalar_prefetch=2,
        grid=(nqb,),
        in_specs=[pl.BlockSpec((1,) + shp, lambda i, *_: (i, 0, 0, 0)),
                  pl.BlockSpec((1,) + shp, lambda i, *_: (i, 0, 0, 0)),
                  pl.BlockSpec((1, A_KV_HEADS, nsb, Q_BLOCK), lambda i, *_: (i, 0, 0, 0)),
                  pl.BlockSpec((Q_BLOCK, SMALL_W), lambda i, *_: (i, 0)),
                  full(ks), full(vst), full(kw), full(vwt)],
        out_specs=pl.BlockSpec((Q_BLOCK, A_WIDTH), lambda i, *_: (i, 0)),
    )
    return pl.pallas_call(
        functools.partial(_nsa_sel_kernel, ntiles=ntiles),
        grid_spec=grid_spec,
        out_shape=jax.ShapeDtypeStruct((t, A_WIDTH), F32),
        compiler_params=_cparams("arbitrary"),
        name="nsa_selected",
    )(tiles, counts, qt, oct, sel, small, ks, vst, kw, vwt)


def _cmp_weights(w_cmp, pe_cmp):
    eye_k = jnp.eye(A_KV_HEADS, dtype=F32)
    ws, pes = [], []
    for half in range(2):
        w = w_cmp[:, half * CMP_STRIDE:(half + 1) * CMP_STRIDE]
        full = jnp.einsum('sjde,ts,kl->jtkdsle', w, jnp.eye(N_KV_SLOTS, 2, dtype=F32), eye_k)
        ws.append(full.reshape(CMP_ROWS, CMP_COLS).astype(BF16))
        pe = pe_cmp[:, half * CMP_STRIDE:(half + 1) * CMP_STRIDE]
        pe = jnp.pad(pe, ((0, N_KV_SLOTS - 2), (0, 0), (0, 0)))
        pe = jnp.broadcast_to(pe.transpose(1, 0, 2)[:, :, None, :], (CMP_STRIDE, N_KV_SLOTS, A_KV_HEADS, HEAD_DIM))
        pes.append(jnp.broadcast_to(pe.reshape(1, CMP_ROWS), (16, CMP_ROWS)).astype(BF16))
    return ws[0], ws[1], pes[0], pes[1]


def _nsa_prompt(q_raw, kv_rows, win_rows, small, q_norm_g, k_norm_g, w_cmp, pe_cmp):
    t = q_raw.shape[0]
    w1, w2, pe1, pe2 = _cmp_weights(w_cmp, pe_cmp)
    kc, vct = _nsa_compress(kv_rows, w1, w2, pe1, pe2, k_norm_g[0].reshape(1, HEAD_DIM))
    ks, vst, kw, vwt = _nsa_rows(kv_rows, win_rows, k_norm_g[1].reshape(1, HEAD_DIM), k_norm_g[2].reshape(1, HEAD_DIM))
    qt, oct, sel, bany = _nsa_cmp(q_raw, q_norm_g.reshape(HEAD_DIM, 1), kc, vct)
    nqb, ntiles = t // Q_BLOCK, t // SEL_TILE
    flags = bany.reshape(nqb, A_KV_HEADS, ntiles, SEL_TILE_BLOCKS).max(-1) > 0.5
    order = jnp.argsort(jnp.logical_not(flags), axis=-1, stable=True).astype(jnp.int32)
    counts = flags.sum(-1).astype(jnp.int32)
    return _nsa_sel(order.reshape(-1), counts.reshape(-1), qt, oct, sel, small, ks, vst, kw, vwt)


DELTA_STEP = 2 * DELTA_CHUNK
CONV_HALO = 8
SM_A = 3 * A_HEADS
SM_B = SM_A + B_HEADS


def _softplus(x):
    return jnp.maximum(x, 0.0) + jnp.log(1.0 + jnp.exp(-jnp.abs(x)))


def _l2_rows(x):
    return x * lax.rsqrt(jnp.sum(x * x, axis=-1, keepdims=True) + EPS)


def _dot_nt3(a, b):
    hi = b.astype(BF16)
    lo = (b - hi.astype(F32)).astype(BF16)
    return lax.dot_general(_split3(a), jnp.concatenate([hi, hi, lo], axis=1), (((1,), (1,)), ((), ())),
                           preferred_element_type=F32)


def _delta_kernel(x_ref, small_ref, gate_ref, cw_ref, alog_ref, dtb_ref, on_ref, o_ref, s_out_ref, xbuf, s_ref):
    i = pl.program_id(0)
    n = DELTA_STEP

    @pl.when(i == 0)
    def _():
        xbuf[0:CONV_HALO, :] = jnp.zeros((CONV_HALO, CONV_CH), F32)
        s_ref[...] = jnp.zeros(s_ref.shape, F32)

    x = x_ref[...]
    xbuf[CONV_HALO:CONV_HALO + n, :] = x
    base = CONV_HALO - (CONV_W - 1)
    acc = xbuf[pl.ds(base, n), :] * cw_ref[0:1, :]
    for j in range(1, CONV_W):
        acc = acc + xbuf[pl.ds(base + j, n), :] * cw_ref[j:j + 1, :]
    c = _silu(acc)
    xbuf[0:CONV_HALO, :] = x[n - CONV_HALO:n, :]

    sm = small_ref[...]
    g_all = -jnp.exp(alog_ref[...]) * _softplus(sm + dtb_ref[...])
    beta_all = jax.nn.sigmoid(sm)
    r = lax.broadcasted_iota(jnp.int32, (n, n), 0)
    cc = lax.broadcasted_iota(jnp.int32, (n, n), 1)
    same = (r >= DELTA_CHUNK) == (cc >= DELTA_CHUNK)
    incl = jnp.logical_and(same, r >= cc)
    strict = jnp.logical_and(same, r > cc)
    gcum = jnp.dot(jnp.where(incl, 1.0, 0.0), g_all, preferred_element_type=F32, precision=HIGHEST)
    glast = jnp.dot(jnp.where(same, 1.0, 0.0), g_all, preferred_element_type=F32, precision=HIGHEST)
    gcum_t = gcum.T

    nk = B_HEADS * B_DK
    for h in range(B_HEADS):
        gc = gcum[:, SM_A + h:SM_A + h + 1]
        gl = glast[:, SM_A + h:SM_A + h + 1]
        decay = jnp.exp(jnp.where(incl, gc - gcum_t[SM_A + h:SM_A + h + 1, :], -jnp.inf))
        bcol = beta_all[:, SM_B + h:SM_B + h + 1]
        q = _l2_rows(c[:, h * B_DK:(h + 1) * B_DK]) * B_DK ** -0.5
        k = _l2_rows(c[:, nk + h * B_DK:nk + (h + 1) * B_DK])
        v = c[:, 2 * nk + h * B_DV:2 * nk + (h + 1) * B_DV]
        pw = -jnp.where(strict, _dot_nt3(k, k) * decay * bcol, 0.0)
        tm = pw
        for _ in range(DELTA_CHUNK.bit_length() - 2):
            pw = _mm3(pw, _stack3(pw))
            tm = tm + pw + _mm3(tm, _stack3(pw))
        eg = jnp.exp(gc)
        rhs_u = v * bcol
        rhs_w = k * bcol * eg
        solved = _mm3(tm, _stack3(jnp.concatenate([rhs_u, rhs_w], axis=1)))
        u = rhs_u + solved[:, :B_DV]
        w = rhs_w + solved[:, B_DV:]
        qk = _dot_nt3(q, k) * decay
        q_dec = q * eg
        kd_t = (k * jnp.exp(gl - gc)).T
        s = s_ref[h]
        outs = []
        for ci in range(2):
            rows = slice(ci * DELTA_CHUNK, (ci + 1) * DELTA_CHUNK)
            s3 = _stack3(s)
            v_new = u[rows] - _mm3(w[rows], s3)
            outs.append(_mm3(q_dec[rows], s3) + _mm(qk[rows, rows], v_new.astype(BF16), False))
            g_end = jnp.exp(glast[ci * DELTA_CHUNK:ci * DELTA_CHUNK + 1, SM_A + h:SM_A + h + 1])
            s = s * g_end + _mm3(kd_t[:, rows], _stack3(v_new))
        s_ref[h] = s
        o = jnp.concatenate(outs, axis=0)
        o = _rms_rows(o) * on_ref[...] * _silu(gate_ref[:, h * B_DV:(h + 1) * B_DV])
        o_ref[:, h * B_DV:(h + 1) * B_DV] = o

    @pl.when(i == pl.num_programs(0) - 1)
    def _():
        s_out_ref[...] = s_ref[...]


def _lane_slab(vals, lane0):
    return jnp.pad(vals, (lane0, LANES - lane0 - vals.shape[0])).reshape(1, LANES)


def _delta_prompt(qkv_b, small, gate_b, conv_w, a_log, dt_bias, o_norm_g):
    t = qkv_b.shape[0]
    n = DELTA_STEP
    return pl.pallas_call(
        _delta_kernel,
        grid=(t // n,),
        in_specs=[pl.BlockSpec((n, CONV_CH), lambda i: (i, 0)),
                  pl.BlockSpec((n, SMALL_W), lambda i: (i, 0)),
                  pl.BlockSpec((n, B_WIDTH), lambda i: (i, 0)),
                  _const_spec((CONV_W, CONV_CH)), _const_spec((1, LANES)), _const_spec((1, LANES)),
                  _const_spec((1, B_DV))],
        out_specs=[pl.BlockSpec((n, B_WIDTH), lambda i: (i, 0)), _const_spec((B_HEADS, B_DK, B_DV))],
        out_shape=[jax.ShapeDtypeStruct((t, B_WIDTH), F32), jax.ShapeDtypeStruct((B_HEADS, B_DK, B_DV), F32)],
        scratch_shapes=[pltpu.VMEM((CONV_HALO + n, CONV_CH), F32), pltpu.VMEM((B_HEADS, B_DK, B_DV), F32)],
        compiler_params=_cparams("arbitrary"),
        name="delta_prompt",
    )(qkv_b, small, gate_b, conv_w, _lane_slab(a_log, SM_A), _lane_slab(dt_bias, SM_A), o_norm_g.reshape(1, B_DV))


PAGE_ROWS = 128
PAGES_PER_STEP = 16
STRIPES_PER_PAGE = PAGE_ROWS // CMP_STRIDE
HALF = A_KV_HEADS * HEAD_DIM
ROW_W = N_KV_SLOTS * HALF
SUB = 8
N_PICK = SLC_TOPN - 2


def _dot_nt_exact(a, b):
    return lax.dot_general(a, b, (((1,), (1,)), ((), ())), preferred_element_type=F32, precision=HIGHEST)


def _row_slopes(k):
    row = lax.broadcasted_iota(jnp.int32, (SUB, 1), 0)
    s = jnp.zeros((SUB, 1), F32)
    for g in range(A_GROUP):
        s = jnp.where(row == g, _head_slope(k * A_GROUP + g), s)
    return s


def _head_rows(x, k, gq):
    rows = [x[:, (k * A_GROUP + g) * HEAD_DIM:(k * A_GROUP + g + 1) * HEAD_DIM] for g in range(A_GROUP)]
    q = jnp.concatenate(rows + [jnp.zeros((SUB - A_GROUP, HEAD_DIM), F32)], axis=0)
    return _rms_rows(q) * gq * HEAD_DIM ** -0.5


def _nsa_s1_kernel(pt_ref, *refs, nb, past):
    del pt_ref
    npg = PAGES_PER_STEP
    kpages, vpages = refs[:npg], refs[npg:2 * npg]
    q_ref, gq_ref, gk_ref, w_ref, bias_ref = refs[2 * npg:2 * npg + 5]
    qn_ref, oc_ref, idx_ref = refs[2 * npg + 5:2 * npg + 8]
    f1_ref, f2_ref, rows_ref = refs[2 * npg + 8:]
    c = pl.program_id(1)
    nsb = nb // RATIO
    rows_step = npg * STRIPES_PER_PAGE

    @pl.when(jnp.logical_and(pl.program_id(0) == 0, c == 0))
    def _():
        for s in range(2):
            f2_ref[s, pl.ds(nb, SUB), :] = jnp.zeros((SUB, HALF), F32)

    row0 = pl.multiple_of(c * rows_step, rows_step)
    for s, pages in enumerate((kpages, vpages)):
        for i, p in enumerate(pages):
            rows_ref[i] = p[0].T
        acc = jnp.zeros((rows_step, 2 * HALF), F32)
        for j in range(CMP_STRIDE):
            x = jnp.concatenate([rows_ref[i, pl.ds(j, STRIPES_PER_PAGE, stride=CMP_STRIDE), :] for i in range(npg)],
                                axis=0)
            acc = acc + jnp.dot(_split3(x), w_ref[s, j], preferred_element_type=F32)
        f1_ref[s, pl.ds(row0, rows_step), :] = acc[:, :HALF]
        f2_ref[s, pl.ds(row0, rows_step), :] = acc[:, HALF:]

    @pl.when(c == pl.num_programs(1) - 1)
    def _():
        lane = lax.broadcasted_iota(jnp.int32, (SUB, nb), 1)
        shift = nsb.bit_length() - 1
        cpos = (((lane & (nsb - 1)) * RATIO + (lane >> shift)) * CMP_STRIDE) + (CMP_BLOCK - 1)
        dist = past - cpos
        valid = dist >= 0
        distf = dist.astype(F32)
        row = lax.broadcasted_iota(jnp.int32, (SUB, nb), 0)
        kc = [[], []]
        vc = []
        for r in range(RATIO):
            kb = f1_ref[0, pl.ds(r, nsb, stride=RATIO), :] + f2_ref[0, pl.ds(r + 1, nsb, stride=RATIO), :] + bias_ref[0]
            vc.append(f1_ref[1, pl.ds(r, nsb, stride=RATIO), :] + f2_ref[1, pl.ds(r + 1, nsb, stride=RATIO), :]
                      + bias_ref[1])
            for k in range(A_KV_HEADS):
                kc[k].append(_rms_rows(kb[:, k * HEAD_DIM:(k + 1) * HEAD_DIM]) * gk_ref[...])
        vc = jnp.concatenate(vc, axis=0)
        qn, oc, imps = [], [], []
        for k in range(A_KV_HEADS):
            q = _head_rows(q_ref[0], k, gq_ref[...])
            qn.append(q)
            s = _dot_nt_exact(q, jnp.concatenate(kc[k], axis=0))
            s = jnp.where(valid, s - _row_slopes(k) * distf, NEG)
            e = jnp.exp(s - jnp.max(s, axis=-1, keepdims=True))
            p = e / jnp.sum(e, axis=-1, keepdims=True)
            p = jnp.where(jnp.logical_and(row < A_GROUP, past >= CMP_BLOCK - 1), p, 0.0)
            oc.append(_mm(p, vc[:, k * HEAD_DIM:(k + 1) * HEAD_DIM], True))
            pg = jnp.broadcast_to(jnp.sum(p, axis=0, keepdims=True), (SUB, nb))
            last = pg[:, 3 * nsb:4 * nsb]
            lane_b = lax.broadcasted_iota(jnp.int32, (SUB, nsb), 1)
            prev = jnp.where(lane_b == 0, 0.0, pltpu.roll(last, 1, axis=1))
            imps.append(pg[:, 0:nsb] + pg[:, nsb:2 * nsb] + pg[:, 2 * nsb:3 * nsb] + last + prev)
        qn_ref[0] = jnp.concatenate(qn, axis=1)
        oc_ref[0] = jnp.concatenate(oc, axis=1)
        lane_s = lax.broadcasted_iota(jnp.int32, (SUB, nsb), 1)
        row_s = lax.broadcasted_iota(jnp.int32, (SUB, nsb), 0)
        x = jnp.zeros((SUB, nsb), F32)
        for k in range(A_KV_HEADS):
            x = jnp.where(row_s == k, imps[k], x)
        x = jnp.where(lane_s == 0, -jnp.inf, x)
        out_lane = lax.broadcasted_iota(jnp.int32, (SUB, LANES), 1)
        picks = jnp.zeros((SUB, LANES), jnp.int32)
        for i in range(N_PICK):
            m = jnp.max(x, axis=-1, keepdims=True)
            first = jnp.min(jnp.where(x == m, lane_s, nsb), axis=-1, keepdims=True)
            picks = jnp.where(out_lane == i + 1, first, picks)
            x = jnp.where(lane_s == first, -jnp.inf, x)
        idx_ref[0] = picks


def _nsa_sample_cmp(q_raw, cache_t, page_table, gq, gk, wcat, bias, past):
    db = q_raw.shape[0]
    n_pages = page_table.shape[1]
    nb = past // CMP_STRIDE
    nch = n_pages // PAGES_PER_STEP

    def page_spec(p, slot):
        return pl.BlockSpec((1, HALF, PAGE_ROWS),
                            lambda b, c, pt: (pt[jnp.minimum(b, db - 1) * n_pages + c * PAGES_PER_STEP + p], slot, 0))

    cst = lambda shape: pl.BlockSpec(shape, lambda b, c, pt: (0,) * len(shape))
    per_b = lambda: pl.BlockSpec((1, SUB, LANES), lambda b, c, pt: (b, 0, 0))
    grid_spec = pltpu.PrefetchScalarGridSpec(
        num_scalar_prefetch=1,
        grid=(db, nch),
        in_specs=([page_spec(p, 0) for p in range(PAGES_PER_STEP)] + [page_spec(p, 1) for p in range(PAGES_PER_STEP)]
                  + [pl.BlockSpec((1, 1, A_WIDTH), lambda b, c, pt: (b, 0, 0)),
                     cst((1, HEAD_DIM)), cst((1, HEAD_DIM)), cst(wcat.shape), cst(bias.shape)]),
        out_specs=[per_b(), per_b(), per_b()],
        scratch_shapes=[pltpu.VMEM((2, nb + SUB, HALF), F32), pltpu.VMEM((2, nb + SUB, HALF), F32),
                        pltpu.VMEM((PAGES_PER_STEP, PAGE_ROWS, HALF), F32)],
    )
    return pl.pallas_call(
        functools.partial(_nsa_s1_kernel, nb=nb, past=past),
        grid_spec=grid_spec,
        out_shape=[jax.ShapeDtypeStruct((db, SUB, LANES), F32), jax.ShapeDtypeStruct((db, SUB, LANES), F32),
                   jax.ShapeDtypeStruct((db, SUB, LANES), jnp.int32)],
        compiler_params=_cparams("arbitrary", "arbitrary"),
        name="nsa_sample_compressed",
    )(page_table.reshape(-1), *([cache_t] * (2 * PAGES_PER_STEP)), q_raw.reshape(db, 1, A_WIDTH), gq, gk, wcat, bias)


def _nsa_s2_kernel(pt_ref, pick_ref, *refs, past):
    del pt_ref
    nsel = SLC_TOPN - 1
    blocks = refs[:A_KV_HEADS * nsel]
    (win_ref, kvn_ref, winn_ref, qn_ref, oc_ref, small_ref, gs_ref, gsc_ref, gw_ref) = refs[A_KV_HEADS * nsel:-1]
    o_ref = refs[-1]
    b = pl.program_id(0)
    slane = lax.broadcasted_iota(jnp.int32, (1, nsel * PAGE_ROWS), 1)
    spage = slane >> (PAGE_ROWS.bit_length() - 1)
    srow = slane & (PAGE_ROWS - 1)
    gates = jax.nn.sigmoid(small_ref[0])
    row = lax.broadcasted_iota(jnp.int32, (SUB, 1), 0)
    wlane = lax.broadcasted_iota(jnp.int32, (1, WINDOW), 1)
    wdist = (WINDOW - wlane).astype(F32)
    outs = []
    for k in range(A_KV_HEADS):
        lanes = slice(k * HEAD_DIM, (k + 1) * HEAD_DIM)
        q = qn_ref[0][:, lanes]
        slopes = _row_slopes(k)

        def finish(s, s_new, pv, v_new):
            m = jnp.maximum(jnp.max(s, axis=-1, keepdims=True), s_new)
            e, e_new = jnp.exp(s - m), jnp.exp(s_new - m)
            den = jnp.sum(e, axis=-1, keepdims=True) + e_new
            return (pv(e) + e_new * v_new) / den

        kts, vts = [], []
        for i in range(nsel):
            page = blocks[k * nsel + i][0]
            kt = page[lanes]
            kts.append(kt * lax.rsqrt(jnp.mean(kt * kt, axis=0, keepdims=True) + EPS) * gsc_ref[...])
            vts.append(page[HALF + k * HEAD_DIM:HALF + (k + 1) * HEAD_DIM])
        kt_all, vt_all = jnp.concatenate(kts, axis=1), jnp.concatenate(vts, axis=1)
        blk_id = jnp.zeros((1, nsel * PAGE_ROWS), jnp.int32)
        for i in range(nsel):
            blk_id = jnp.where(spage == i, pick_ref[(b * A_KV_HEADS + k) * nsel + i], blk_id)
        per_page = PAGE_ROWS // SLC_BLOCK
        in_block = (srow >> (SLC_BLOCK.bit_length() - 1)) == (blk_id & (per_page - 1))
        sdist = (past - (blk_id * SLC_BLOCK + (srow & (SLC_BLOCK - 1)))).astype(F32)
        kvn = kvn_ref[0]
        k_new = _rms_rows(kvn[:, 2 * HALF + k * HEAD_DIM:2 * HALF + (k + 1) * HEAD_DIM]) * gs_ref[...]
        v_new = kvn[:, 3 * HALF + k * HEAD_DIM:3 * HALF + (k + 1) * HEAD_DIM]
        s_sel = jnp.where(in_block, _mm(q, kt_all, True) - slopes * sdist, NEG)
        o_s = finish(s_sel, jnp.sum(q * k_new, axis=-1, keepdims=True),
                     lambda e: _dot_nt_exact(e, vt_all), v_new)

        win = win_ref[0]
        kw = _rms_rows(win[:, lanes]) * gw_ref[...]
        vw = win[:, HALF + k * HEAD_DIM:HALF + (k + 1) * HEAD_DIM]
        winn = winn_ref[0]
        kw_new = _rms_rows(winn[:, lanes]) * gw_ref[...]
        o_w = finish(_dot_nt_exact(q, kw) - slopes * wdist, jnp.sum(q * kw_new, axis=-1, keepdims=True),
                     lambda e: _mm(e, vw, True), winn[:, HALF + k * HEAD_DIM:HALF + (k + 1) * HEAD_DIM])

        gcol = [jnp.zeros((SUB, 1), F32) for _ in range(3)]
        for g in range(A_GROUP):
            h = k * A_GROUP + g
            for br in range(3):
                gcol[br] = jnp.where(row == g, gates[:, 3 * h + br:3 * h + br + 1], gcol[br])
        outs.append(gcol[0] * oc_ref[0][:, lanes] + gcol[1] * o_s + gcol[2] * o_w)
    o_ref[0] = jnp.concatenate(outs, axis=1)


def _nsa_sample_sel(cache_t, page_table, picks, cache_win, kv_new, win_new, qn, oc, small, gs, gw, past):
    db = qn.shape[0]
    n_pages = page_table.shape[1]
    nsel = SLC_TOPN - 1
    per_page = PAGE_ROWS // SLC_BLOCK

    def block_spec(k, i):
        def imap(b, pt, pk):
            bb = jnp.minimum(b, db - 1)
            j = jnp.clip(pk[(bb * A_KV_HEADS + k) * nsel + i], 0, n_pages * per_page - 1)
            return (pt[bb * n_pages + j // per_page], 1, 0)
        return pl.BlockSpec((1, 2 * HALF, PAGE_ROWS), imap)

    cst = lambda shape: pl.BlockSpec(shape, lambda b, pt, pk: (0,) * len(shape))
    per_b = lambda a: pl.BlockSpec((1,) + a.shape[1:], lambda b, pt, pk: (b,) + (0,) * (a.ndim - 1))
    grid_spec = pltpu.PrefetchScalarGridSpec(
        num_scalar_prefetch=2,
        grid=(db,),
        in_specs=([block_spec(k, i) for k in range(A_KV_HEADS) for i in range(nsel)]
                  + [per_b(cache_win), per_b(kv_new), per_b(win_new), per_b(qn), per_b(oc), per_b(small),
                     cst((1, HEAD_DIM)), cst((HEAD_DIM, 1)), cst((1, HEAD_DIM))]),
        out_specs=pl.BlockSpec((1, SUB, LANES), lambda b, pt, pk: (b, 0, 0)),
    )
    return pl.pallas_call(
        functools.partial(_nsa_s2_kernel, past=past),
        grid_spec=grid_spec,
        out_shape=jax.ShapeDtypeStruct((db, SUB, LANES), F32),
        compiler_params=_cparams("arbitrary"),
        name="nsa_sample_selected",
    )(page_table.reshape(-1), picks.reshape(-1), *([cache_t] * (A_KV_HEADS * nsel)),
      cache_win, kv_new, win_new, qn, oc, small, gs, gs.reshape(HEAD_DIM, 1), gw)


def _nsa_sample(q_raw, kv_new, win_new, small, cache_kv, page_table, cache_win, q_norm_g, k_norm_g, w_cmp, pe_cmp):
    db = q_raw.shape[0]
    n_phys = cache_kv.shape[0]
    past = page_table.shape[1] * PAGE_ROWS
    assert cache_kv.shape[1] == PAGE_ROWS and past % SLC_BLOCK == 0 and past >= WINDOW == cache_win.shape[1]
    eye_k = jnp.eye(A_KV_HEADS, dtype=F32)
    w = jnp.einsum('sjde,kl->sjkdle', w_cmp, eye_k).reshape(2, CMP_BLOCK, HALF, HALF)
    w = jnp.concatenate([w[:, :CMP_STRIDE], w[:, CMP_STRIDE:]], axis=-1)
    hi = w.astype(BF16)
    lo = (w - hi.astype(F32)).astype(BF16)
    wcat = jnp.concatenate([hi, hi, lo], axis=2)
    with jax.default_matmul_precision("highest"):
        bias = jnp.einsum('sjd,sjde->se', pe_cmp, w_cmp)
    bias = jnp.tile(bias[:, None, :], (1, 1, A_KV_HEADS))
    cache_t = cache_kv.reshape(n_phys, PAGE_ROWS, ROW_W).transpose(0, 2, 1)
    qn, oc, picks = _nsa_sample_cmp(q_raw, cache_t, page_table, q_norm_g.reshape(1, HEAD_DIM),
                                    k_norm_g[0].reshape(1, HEAD_DIM), wcat, bias, past)
    picks = picks[:, :A_KV_HEADS, :SLC_TOPN - 1]
    o = _nsa_sample_sel(cache_t, page_table, picks, cache_win.reshape(db, WINDOW, 2 * HALF),
                        kv_new.reshape(db, 1, ROW_W), win_new.reshape(db, 1, 2 * HALF), qn, oc,
                        small.reshape(db, 1, SMALL_W), k_norm_g[1].reshape(1, HEAD_DIM),
                        k_norm_g[2].reshape(1, HEAD_DIM), past)
    o = o[:, :A_GROUP].reshape(db, A_GROUP, A_KV_HEADS, HEAD_DIM).transpose(0, 2, 1, 3)
    return o.reshape(db, A_WIDTH)


def _delta_step_kernel(conv_ref, x_ref, small_ref, gate_ref, cw_ref, alog_ref, dtb_ref, on_ref, s_ref, o_ref, s_out_ref):
    acc = conv_ref[0][0:1, :] * cw_ref[0:1, :]
    for j in range(1, CONV_W - 1):
        acc = acc + conv_ref[0][j:j + 1, :] * cw_ref[j:j + 1, :]
    c = _silu(acc + x_ref[0] * cw_ref[CONV_W - 1:CONV_W, :])
    sm = small_ref[0]
    g_all = -jnp.exp(alog_ref[...]) * _softplus(sm + dtb_ref[...])
    beta_all = jax.nn.sigmoid(sm)
    nk = B_HEADS * B_DK

    def as_columns(row):
        return jnp.broadcast_to(row, (row.shape[1], row.shape[1])).T

    for h in range(B_HEADS):
        q = _l2_rows(c[:, h * B_DK:(h + 1) * B_DK]) * B_DK ** -0.5
        k = _l2_rows(c[:, nk + h * B_DK:nk + (h + 1) * B_DK])
        v = c[:, 2 * nk + h * B_DV:2 * nk + (h + 1) * B_DV]
        kc, qc = as_columns(k), as_columns(q)
        s = s_ref[0, h] * jnp.exp(g_all[:, SM_A + h:SM_A + h + 1])
        u = beta_all[:, SM_B + h:SM_B + h + 1] * (v - jnp.sum(kc * s, axis=0, keepdims=True))
        s = s + kc * u
        s_out_ref[0, h] = s
        o = jnp.sum(qc * s, axis=0, keepdims=True)
        o = _rms_rows(o) * on_ref[...] * _silu(gate_ref[0][:, h * B_DV:(h + 1) * B_DV])
        o_ref[0, :, h * B_DV:(h + 1) * B_DV] = o


def _delta_step(state_conv, qkv_new, small, gate_b, conv_w, a_log, dt_bias, o_norm_g, state):
    db = qkv_new.shape[0]
    per_b = lambda *s: pl.BlockSpec((1,) + s, lambda b: (b,) + (0,) * len(s))
    return pl.pallas_call(
        _delta_step_kernel,
        grid=(db,),
        in_specs=[per_b(CONV_W - 1, CONV_CH), per_b(1, CONV_CH), per_b(1, SMALL_W), per_b(1, B_WIDTH),
                  _const_spec((CONV_W, CONV_CH)), _const_spec((1, LANES)), _const_spec((1, LANES)),
                  _const_spec((1, B_DV)), per_b(B_HEADS, B_DK, B_DV)],
        out_specs=[per_b(1, B_WIDTH), per_b(B_HEADS, B_DK, B_DV)],
        out_shape=[jax.ShapeDtypeStruct((db, 1, B_WIDTH), F32),
                   jax.ShapeDtypeStruct((db, B_HEADS, B_DK, B_DV), F32)],
        compiler_params=_cparams("arbitrary"),
        name="delta_step",
    )(state_conv, qkv_new.reshape(db, 1, CONV_CH), small.reshape(db, 1, SMALL_W), gate_b.reshape(db, 1, B_WIDTH),
      conv_w, _lane_slab(a_log, SM_A), _lane_slab(dt_bias, SM_A), o_norm_g.reshape(1, B_DV), state)


def _jx_rmsnorm(x, g):
    y = x * lax.rsqrt(jnp.mean(x * x, axis=-1, keepdims=True) + EPS)
    return y * g


def _jx_l2norm(x):
    return x * lax.rsqrt(jnp.sum(x * x, axis=-1, keepdims=True) + EPS)


def _jx_slopes():
    h = jnp.arange(1, A_HEADS + 1, dtype=F32)
    return jnp.exp2(-8.0 * h / A_HEADS)


def _jx_compress(rows, w, pe):
    B, L, K, D = rows.shape
    r = rows.reshape(B, L // CMP_STRIDE, CMP_STRIDE, K, D)
    first = jnp.einsum('bnjkd,jde->bnke', r, w[:CMP_STRIDE])
    second = jnp.einsum('bnjkd,jde->bnke', r, w[CMP_STRIDE:])
    bias = jnp.einsum('jd,jde->e', pe, w)
    return first[:, :-1] + second[:, 1:] + bias


def _jx_nsa_keys(rows, k_norm_g, w_cmp, pe_cmp):
    B, L = rows.shape[:2]
    nsb = L // SLC_BLOCK
    kc = _jx_rmsnorm(_jx_compress(rows[:, :, 0], w_cmp[0], pe_cmp[0]), k_norm_g[0])
    vc = _jx_compress(rows[:, :, 1], w_cmp[1], pe_cmp[1])
    cpos = jnp.arange(kc.shape[1], dtype=jnp.int32) * CMP_STRIDE + (CMP_BLOCK - 1)

    def blocks(r):
        return r.reshape(B, nsb, SLC_BLOCK, A_KV_HEADS, HEAD_DIM).transpose(0, 3, 1, 2, 4)

    ks = blocks(_jx_rmsnorm(rows[:, :, 2], k_norm_g[1]))
    vs = blocks(rows[:, :, 3])
    return kc, vc, cpos, ks, vs


def _jx_nsa_attend(q, qpos, kc, vc, cpos, ks, vs, kw, vw, wpos, gates):
    B, Q = q.shape[:2]
    qg = q.reshape(B, Q, A_KV_HEADS, A_GROUP, HEAD_DIM)
    sl = _jx_slopes().reshape(A_KV_HEADS, A_GROUP)[None, :, :, None, None]
    dist_c = qpos[:, None] - cpos[None, :]
    valid_c = dist_c >= 0
    s_c = jnp.einsum('bqkgd,bnkd->bkgqn', qg, kc).astype(F32)
    s_c = jnp.where(valid_c, s_c - sl * dist_c.astype(F32), NEG)
    p_c = jax.nn.softmax(s_c, axis=-1) * jnp.any(valid_c, axis=-1)[:, None].astype(F32)
    o_c = jnp.einsum('bkgqn,bnkd->bqkgd', p_c, vc)
    nsb = ks.shape[2]
    nc = kc.shape[1]
    pg = jnp.pad(p_c.sum(axis=2), ((0, 0), (0, 0), (0, 0), (0, RATIO * nsb - nc)))
    pg = pg.reshape(B, A_KV_HEADS, Q, nsb, RATIO)
    imp = pg.sum(-1) + jnp.pad(pg[..., :-1, RATIO - 1], ((0, 0), (0, 0), (0, 0), (1, 0)))
    jidx = jnp.arange(nsb, dtype=jnp.int32)
    cur = qpos // SLC_BLOCK
    forced = (jidx[None, :] == cur[:, None]) | (jidx[None, :] == 0)
    avail = jidx[None, :] <= cur[:, None]
    imp = jnp.where(forced, BIG, jnp.where(avail, imp, NEG))
    _, sel = lax.top_k(imp, min(SLC_TOPN, nsb))
    bi = jnp.arange(B)[:, None, None, None]
    ki = jnp.arange(A_KV_HEADS)[None, :, None, None]
    kg = ks[bi, ki, sel]
    vg = vs[bi, ki, sel]
    spos = sel[..., None] * SLC_BLOCK + jnp.arange(SLC_BLOCK, dtype=jnp.int32)
    dist_s = (qpos[:, None, None] - spos)[:, :, None]
    s_s = jnp.einsum('bqkgd,bkqnsd->bkgqns', qg, kg).astype(F32)
    s_s = jnp.where(dist_s >= 0, s_s - sl[..., None] * dist_s.astype(F32), NEG)
    shp = s_s.shape
    p_s = jax.nn.softmax(s_s.reshape(shp[:4] + (-1,)), axis=-1).reshape(shp)
    o_s = jnp.einsum('bkgqns,bkqnsd->bqkgd', p_s, vg)
    dist_w = qpos[:, None] - wpos[None, :]
    valid_w = (dist_w >= 0) & (dist_w <= WINDOW) & (wpos >= 0)[None, :]
    s_w = jnp.einsum('bqkgd,bnkd->bkgqn', qg, kw).astype(F32)
    s_w = jnp.where(valid_w, s_w - sl * dist_w.astype(F32), NEG)
    p_w = jax.nn.softmax(s_w, axis=-1)
    o_w = jnp.einsum('bkgqn,bnkd->bqkgd', p_w, vw)
    gr = gates.reshape(B, Q, A_KV_HEADS, A_GROUP, 3)
    o = gr[..., 0:1] * o_c + gr[..., 1:2] * o_s + gr[..., 2:3] * o_w
    return o.reshape(B, Q, A_WIDTH)


def _jx_nsa_prompt(q, gates, kv_rows, win_rows, lp):
    B, T = q.shape[:2]
    kc, vc, cpos, ks, vs = _jx_nsa_keys(kv_rows, lp['k_norm_g'], lp['w_cmp'], lp['pe_cmp'])
    kw = _jx_rmsnorm(win_rows[:, :, 0], lp['k_norm_g'][2])
    pad = ((0, 0), (WINDOW, 0), (0, 0), (0, 0))
    kw_pad = jnp.pad(kw, pad)
    vw_pad = jnp.pad(win_rows[:, :, 1], pad)

    def block(i):
        q0 = i * Q_BLOCK
        qb = lax.dynamic_slice_in_dim(q, q0, Q_BLOCK, axis=1)
        gb = lax.dynamic_slice_in_dim(gates, q0, Q_BLOCK, axis=1)
        qpos = q0 + jnp.arange(Q_BLOCK, dtype=jnp.int32)
        kwb = lax.dynamic_slice_in_dim(kw_pad, q0, WINDOW + Q_BLOCK, axis=1)
        vwb = lax.dynamic_slice_in_dim(vw_pad, q0, WINDOW + Q_BLOCK, axis=1)
        wpos = q0 - WINDOW + jnp.arange(WINDOW + Q_BLOCK, dtype=jnp.int32)
        return _jx_nsa_attend(qb, qpos, kc, vc, cpos, ks, vs, kwb, vwb, wpos, gb)

    o = lax.map(block, jnp.arange(T // Q_BLOCK, dtype=jnp.int32))
    return o.transpose(1, 0, 2, 3).reshape(B, T, A_WIDTH)


def _jx_nsa_sample(q, gates, kv_new, win_new, cache_kv, page_table, cache_win, lp):
    DB, S = q.shape[:2]
    P = page_table.shape[1] * cache_kv.shape[1]
    past = cache_kv[page_table].reshape(DB, P, N_KV_SLOTS, A_KV_HEADS, HEAD_DIM)
    rows = jnp.concatenate([past, kv_new], axis=1)
    L = P + S
    Lp = -(-L // SLC_BLOCK) * SLC_BLOCK
    rows = jnp.pad(rows, ((0, 0), (0, Lp - L), (0, 0), (0, 0), (0, 0)))
    kc, vc, cpos, ks, vs = _jx_nsa_keys(rows, lp['k_norm_g'], lp['w_cmp'], lp['pe_cmp'])
    nwb = cache_win.shape[1]
    win = jnp.concatenate([cache_win, win_new], axis=1)
    wpos = P - nwb + jnp.arange(nwb + S, dtype=jnp.int32)
    kw = _jx_rmsnorm(win[:, :, 0], lp['k_norm_g'][2])
    qpos = P + jnp.arange(S, dtype=jnp.int32)
    o = _jx_nsa_attend(q, qpos, kc, vc, cpos, ks, vs, kw, win[:, :, 1], wpos, gates)
    return o, win[:, S:]


def _jx_causal_conv(xpad, w):
    T = xpad.shape[1] - (CONV_W - 1)
    acc = xpad[:, 0:T] * w[0]
    for j in range(1, CONV_W):
        acc = acc + xpad[:, j:j + T] * w[j]
    return jax.nn.silu(acc)


def _jx_delta_inputs(qkv, a_b, b_b, a_log, dt_bias):
    B, T, _ = qkv.shape
    nk = B_HEADS * B_DK
    q = _jx_l2norm(qkv[..., :nk].reshape(B, T, B_HEADS, B_DK)) * B_DK ** -0.5
    k = _jx_l2norm(qkv[..., nk:2 * nk].reshape(B, T, B_HEADS, B_DK))
    v = qkv[..., 2 * nk:].reshape(B, T, B_HEADS, B_DV)
    g = -jnp.exp(a_log) * jax.nn.softplus(a_b + dt_bias)
    beta = jax.nn.sigmoid(b_b)
    return q, k, v, g, beta


def _jx_delta_chunked(q, k, v, g, beta):
    B, T, H, _ = q.shape
    C = DELTA_CHUNK
    N = T // C

    def chunks(x):
        return jnp.moveaxis(x.reshape((B, N, C) + x.shape[2:]), 2, 3)

    qc, kc, vc, gc, bc = chunks(q), chunks(k), chunks(v), chunks(g), chunks(beta)
    G = jnp.cumsum(gc, axis=-1)
    i = jnp.arange(C)
    incl = i[:, None] >= i[None, :]
    strict = i[:, None] > i[None, :]
    decay = jnp.exp(jnp.where(incl, G[..., :, None] - G[..., None, :], -jnp.inf))
    kk = jnp.einsum('bnhid,bnhjd->bnhij', kc, kc)
    a_mat = jnp.where(strict, kk * decay * bc[..., :, None], 0.0) + jnp.eye(C, dtype=F32)
    eg = jnp.exp(G)[..., None]
    rhs = jnp.concatenate([vc * bc[..., None], kc * bc[..., None] * eg], axis=-1)
    sol = lax.linalg.triangular_solve(a_mat, rhs, left_side=True, lower=True, unit_diagonal=True)
    u_c, w_c = sol[..., :B_DV], sol[..., B_DV:]
    qk = jnp.einsum('bnhid,bnhjd->bnhij', qc, kc) * decay
    q_dec = qc * eg
    k_dec = kc * jnp.exp(G[..., -1:] - G)[..., None]
    g_last = jnp.exp(G[..., -1])

    def step(s, inp):
        u_i, w_i, qk_i, qd_i, kd_i, gl_i = inp
        v_new = u_i - jnp.einsum('bhck,bhkv->bhcv', w_i, s)
        o = jnp.einsum('bhck,bhkv->bhcv', qd_i, s) + jnp.einsum('bhij,bhjv->bhiv', qk_i, v_new)
        s = s * gl_i[..., None, None] + jnp.einsum('bhck,bhcv->bhkv', kd_i, v_new)
        return s, o

    xs = (jnp.moveaxis(u_c, 1, 0), jnp.moveaxis(w_c, 1, 0), jnp.moveaxis(qk, 1, 0),
          jnp.moveaxis(q_dec, 1, 0), jnp.moveaxis(k_dec, 1, 0), jnp.moveaxis(g_last, 1, 0))
    s0 = jnp.zeros((B, H, B_DK, B_DV), F32)
    s_fin, o = lax.scan(step, s0, xs)
    o = jnp.moveaxis(jnp.moveaxis(o, 0, 1), 3, 2).reshape(B, T, H, B_DV)
    return o, s_fin


def _jx_delta_recurrent(s0, q, k, v, g, beta):
    def step(s, inp):
        q_t, k_t, v_t, g_t, b_t = inp
        s = s * jnp.exp(g_t)[..., None, None]
        u = b_t[..., None] * (v_t - jnp.einsum('bhk,bhkv->bhv', k_t, s))
        s = s + jnp.einsum('bhk,bhv->bhkv', k_t, u)
        return s, jnp.einsum('bhk,bhkv->bhv', q_t, s)

    xs = (jnp.moveaxis(q, 1, 0), jnp.moveaxis(k, 1, 0), jnp.moveaxis(v, 1, 0),
          jnp.moveaxis(g, 1, 0), jnp.moveaxis(beta, 1, 0))
    s_fin, o = lax.scan(step, s0, xs)
    return jnp.moveaxis(o, 0, 1), s_fin


def _jx_delta_output(o, gate_b, o_norm_g):
    B, T = o.shape[:2]
    gt = gate_b.reshape(B, T, B_HEADS, B_DV)
    y = _jx_rmsnorm(o, o_norm_g) * jax.nn.silu(gt)
    return y.reshape(B, T, B_WIDTH)


def _rearranged_w_in(w_in):
    o = IN_OFFSETS
    q_a, kv_a, g_a, qkv_b, a_b, b_b, gate_b, merge = (
        w_in[:, :o[0]], w_in[:, o[0]:o[1]], w_in[:, o[1]:o[2]], w_in[:, o[2]:o[3]],
        w_in[:, o[3]:o[4]], w_in[:, o[4]:o[5]], w_in[:, o[5]:o[6]], w_in[:, o[6]:])
    small = jnp.concatenate([g_a, a_b, b_b], axis=1)
    small = jnp.pad(small, ((0, 0), (0, SMALL_W - small.shape[1])))
    return jnp.concatenate([q_a, kv_a, qkv_b, gate_b, merge, small], axis=1)


def kernel(x_prompt, x_sample, cache_nsa_kv, page_table, cache_win_kv, state_conv, state_delta,
           c_prompt, c_sample, norm1_g, norm2_g, w_ada, b_ada, w_in, q_norm_g, k_norm_g, w_cmp, pe_cmp,
           conv_w, a_log, dt_bias, o_norm_g, w_proj_a, w_proj_b, w_out, w_router, b_router,
           w_exp_gu, w_exp_down, w_sh_gu, w_sh_down):
    T = x_prompt.shape[1]
    DB = x_sample.shape[0]
    assert x_prompt.shape[0] == 1 and x_sample.shape[1] == 1 and w_in.shape[0] == 1

    w_main_f = _rearranged_w_in(w_in[0])
    w_main = w_main_f.astype(BF16)
    wa, wb, wout = w_proj_a[0].astype(BF16), w_proj_b[0].astype(BF16), w_out[0].astype(BF16)
    wr_f = jnp.pad(w_router[0], ((0, 0), (0, LANES - N_EXPERTS)))
    wr = wr_f.astype(BF16)
    br = jnp.pad(b_router[0], (0, LANES - N_EXPERTS)).reshape(1, LANES)
    wgu = w_exp_gu[0].astype(BF16)
    wd2 = w_exp_down[0].astype(BF16).reshape(N_EXPERTS // 2, 2 * D_EXPERT, D_MODEL)
    wsgu, wsd = w_sh_gu[0].astype(BF16), w_sh_down[0].astype(BF16)
    g1, g2 = norm1_g[0].reshape(1, -1), norm2_g[0].reshape(1, -1)

    c_all = jnp.concatenate([c_prompt, jnp.zeros((7, D_MODEL), F32), c_sample], axis=0)
    mod = _adaln(c_all, w_ada[0], b_ada[0])
    mp = [mod[0:1, i * D_MODEL:(i + 1) * D_MODEL] for i in range(6)]
    ms = [mod[8:8 + DB, i * D_MODEL:(i + 1) * D_MODEL] for i in range(6)]

    xp = x_prompt.reshape(T, D_MODEL)
    q_raw, kv, win, qkv_b, gate_b, msig, small = _in_proj(xp, g1, mp[1], mp[0], w_main, 256)
    o_a = _nsa_prompt(q_raw, kv, win, small, q_norm_g[0], k_norm_g[0], w_cmp[0], pe_cmp[0])
    o_b, s_fin_p = _delta_prompt(qkv_b, small, gate_b, conv_w[0], a_log[0], dt_bias[0], o_norm_g[0])
    x1, h2, scores = _merge(xp, o_a, o_b, msig, mp[2], g2, mp[4], mp[3], wa, wb, wout, wr, 256)
    y_prompt = _moe(h2, scores, br, x1, mp[5], wgu, wd2, wsgu, wsd, 1024).reshape(1, T, D_MODEL)
    kv_prompt = kv.reshape(1, 1, T, N_KV_SLOTS, A_KV_HEADS, HEAD_DIM)
    win_prompt = win[T - min(WINDOW, T):].reshape(1, 1, -1, 2, A_KV_HEADS, HEAD_DIM)
    conv_prompt = qkv_b[T - (CONV_W - 1):].reshape(1, 1, CONV_W - 1, CONV_CH)
    delta_prompt = s_fin_p.reshape(1, 1, B_HEADS, B_DK, B_DV)

    xs = x_sample.reshape(DB, D_MODEL)
    q_raw, kv, win, qkv_b, gate_b, msig, small = _in_proj(xs, g1, ms[1], ms[0], w_main_f, DB)
    o_a = _nsa_sample(q_raw, kv, win, small, cache_nsa_kv[0], page_table, cache_win_kv[0],
                      q_norm_g[0], k_norm_g[0], w_cmp[0], pe_cmp[0])
    o_b, s_fin_s = _delta_step(state_conv[0], qkv_b, small, gate_b, conv_w[0], a_log[0], dt_bias[0], o_norm_g[0],
                               state_delta[0])
    x1, h2, scores = _merge(xs, o_a, o_b.reshape(DB, B_WIDTH), msig, ms[2], g2, ms[4], ms[3],
                            w_proj_a[0], w_proj_b[0], w_out[0], wr_f, DB)
    y_sample = _moe(h2, scores, br, x1, ms[5], wgu, wd2, wsgu, wsd, DB).reshape(DB, 1, D_MODEL)
    kv_sample = kv.reshape(1, DB, 1, N_KV_SLOTS, A_KV_HEADS, HEAD_DIM)
    win_sample = jnp.concatenate([cache_win_kv[0][:, 1:], win.reshape(DB, 1, 2, A_KV_HEADS, HEAD_DIM)], axis=1)[None]
    conv_sample = jnp.concatenate([state_conv[0][:, 1:], qkv_b.reshape(DB, 1, CONV_CH)], axis=1)[None]

    return (y_prompt, y_sample, kv_prompt, win_prompt, conv_prompt, delta_prompt,
            kv_sample, win_sample, conv_sample, s_fin_s[None])
```

```python
import functools

import jax
import jax.numpy as jnp
import numpy as np
from jax import lax
from jax.experimental import pallas as pl
from jax.experimental.pallas import tpu as pltpu

F32 = jnp.float32
BF16 = jnp.bfloat16
HIGHEST = lax.Precision.HIGHEST

D_MODEL = 1024
A_HEADS = 8
A_KV_HEADS = 2
A_GROUP = A_HEADS // A_KV_HEADS
HEAD_DIM = 64
CMP_STRIDE = 16
CMP_BLOCK = 32
SLC_BLOCK = 64
RATIO = SLC_BLOCK // CMP_STRIDE
SLC_TOPN = 16
WINDOW = 512
Q_BLOCK = 128
N_KV_SLOTS = 4
B_HEADS = 4
B_DK = 128
B_DV = 128
CONV_W = 4
DELTA_CHUNK = 64
N_EXPERTS = 64
TOP_K = 6
D_EXPERT = 128
D_SHARED = 128
ROUTED_SCALE = 2.5
EPS = 1e-6
NEG = -1e30
BIG = 1e30

A_WIDTH = A_HEADS * HEAD_DIM
B_WIDTH = B_HEADS * B_DV
CONV_CH = 2 * B_HEADS * B_DK + B_HEADS * B_DV
KV_WIDTH = 6 * A_KV_HEADS * HEAD_DIM
IN_SPLITS = (A_WIDTH, KV_WIDTH, 3 * A_HEADS, CONV_CH, B_HEADS, B_HEADS, B_WIDTH, 2 * D_MODEL)
IN_OFFSETS = tuple(int(v) for v in np.cumsum(IN_SPLITS)[:-1])

LANES = 128
SMALL_W = LANES
C_Q = 0
C_KV = C_Q + A_WIDTH
C_QKVB = C_KV + KV_WIDTH
C_GATEB = C_QKVB + CONV_CH
C_MERGE = C_GATEB + B_WIDTH
C_SMALL = C_MERGE + 2 * D_MODEL
W_MAIN = C_SMALL + SMALL_W

VMEM_LIMIT = 56 * 1024 * 1024


def _cparams(*sem):
    return pltpu.CompilerParams(dimension_semantics=sem, vmem_limit_bytes=VMEM_LIMIT)


def _const_spec(shape, single=False):
    nd = len(shape)
    if single:
        return pl.BlockSpec(shape, lambda *_: (0,) * nd, pipeline_mode=pl.Buffered(1))
    return pl.BlockSpec(shape, lambda *_: (0,) * nd)


def _row_spec(tm, width, rows):
    if rows == 1:
        return pl.BlockSpec((1, width), lambda i: (0, 0))
    return pl.BlockSpec((tm, width), lambda i: (i, 0))


def _silu(x):
    return x * jax.nn.sigmoid(x)


def _rms_rows(x):
    return x * lax.rsqrt(jnp.mean(x * x, axis=-1, keepdims=True) + EPS)


def _mm(a, b, exact):
    if exact:
        return jnp.dot(a.astype(F32), b, preferred_element_type=F32, precision=HIGHEST)
    return jnp.dot(a.astype(BF16), b, preferred_element_type=F32)


def _split3(x):
    hi = x.astype(BF16)
    lo = (x - hi.astype(F32)).astype(BF16)
    return jnp.concatenate([hi, lo, hi], axis=1)


def _stack3(x):
    hi = x.astype(BF16)
    lo = (x - hi.astype(F32)).astype(BF16)
    return jnp.concatenate([hi, hi, lo], axis=0)


def _mm3(a, b3):
    return jnp.dot(_split3(a), b3, preferred_element_type=F32)


def _ada_kernel(c_ref, w_ref, b_ref, o_ref):
    o_ref[...] = _mm(_silu(c_ref[...]), w_ref[...], True) + b_ref[...]


def _adaln(c_all, w_ada, b_ada):
    rows = c_all.shape[0]
    tn = 1024
    return pl.pallas_call(
        _ada_kernel,
        grid=(6 * D_MODEL // tn,),
        in_specs=[pl.BlockSpec((rows, D_MODEL), lambda j: (0, 0)),
                  pl.BlockSpec((D_MODEL, tn), lambda j: (0, j)),
                  pl.BlockSpec((1, tn), lambda j: (0, j))],
        out_specs=pl.BlockSpec((rows, tn), lambda j: (0, j)),
        out_shape=jax.ShapeDtypeStruct((rows, 6 * D_MODEL), F32),
        compiler_params=_cparams("arbitrary"),
        name="adaln",
    )(c_all, w_ada, b_ada.reshape(1, -1))


def _in_kernel(x_ref, g_ref, sc_ref, sh_ref, w_ref,
               q_ref, kv_ref, win_ref, qkvb_ref, gateb_ref, merge_ref, small_ref, *, exact):
    h = _rms_rows(x_ref[...]) * g_ref[...]
    h = h * (1.0 + sc_ref[...]) + sh_ref[...]
    if not exact:
        h = h.astype(BF16)

    def proj(c0, width):
        return _mm(h, w_ref[:, c0:c0 + width], exact)

    q_ref[...] = proj(C_Q, A_WIDTH)
    kv_ref[...] = proj(C_KV, N_KV_SLOTS * A_KV_HEADS * HEAD_DIM)
    win_ref[...] = proj(C_KV + N_KV_SLOTS * A_KV_HEADS * HEAD_DIM, 2 * A_KV_HEADS * HEAD_DIM)
    qkvb_ref[...] = proj(C_QKVB, CONV_CH)
    gateb_ref[...] = proj(C_GATEB, B_WIDTH)
    merge_ref[...] = jax.nn.sigmoid(proj(C_MERGE, 2 * D_MODEL))
    small_ref[...] = proj(C_SMALL, SMALL_W)


def _in_proj(x, g1, sc, sh, w_main, tm):
    t = x.shape[0]
    widths = (A_WIDTH, 4 * A_KV_HEADS * HEAD_DIM, 2 * A_KV_HEADS * HEAD_DIM, CONV_CH, B_WIDTH,
              2 * D_MODEL, SMALL_W)
    return pl.pallas_call(
        functools.partial(_in_kernel, exact=w_main.dtype == F32),
        grid=(t // tm,),
        in_specs=[pl.BlockSpec((tm, D_MODEL), lambda i: (i, 0)),
                  _const_spec((1, D_MODEL)),
                  _row_spec(tm, D_MODEL, sc.shape[0]),
                  _row_spec(tm, D_MODEL, sh.shape[0]),
                  _const_spec((D_MODEL, W_MAIN), single=True)],
        out_specs=[pl.BlockSpec((tm, w), lambda i: (i, 0)) for w in widths],
        out_shape=[jax.ShapeDtypeStruct((t, w), F32) for w in widths],
        compiler_params=_cparams("arbitrary"),
        name="in_proj",
    )(x, g1, sc, sh, w_main)


def _merge_kernel(x_ref, oa_ref, ob_ref, msig_ref, gt1_ref, g2_ref, sc2_ref, sh2_ref,
                  wa_ref, wb_ref, wout_ref, wr_ref, x1_ref, h2_ref, score_ref, *, exact):
    pa = _mm(oa_ref[...], wa_ref[...], exact)
    pb = _mm(ob_ref[...], wb_ref[...], exact)
    m = msig_ref[:, :D_MODEL] * pa + msig_ref[:, D_MODEL:] * pb
    y = _mm(m, wout_ref[...], exact)
    x1 = x_ref[...] + gt1_ref[...] * y
    x1_ref[...] = x1
    h2 = _rms_rows(x1) * g2_ref[...]
    h2 = h2 * (1.0 + sc2_ref[...]) + sh2_ref[...]
    h2_ref[...] = h2.astype(BF16)
    score_ref[...] = jax.nn.sigmoid(_mm(h2, wr_ref[...], exact))


def _merge(x, o_a, o_b, msig, gt1, g2, sc2, sh2, wa, wb, wout, wr, tm):
    t = x.shape[0]
    return pl.pallas_call(
        functools.partial(_merge_kernel, exact=wout.dtype == F32),
        grid=(t // tm,),
        in_specs=[pl.BlockSpec((tm, D_MODEL), lambda i: (i, 0)),
                  pl.BlockSpec((tm, A_WIDTH), lambda i: (i, 0)),
                  pl.BlockSpec((tm, B_WIDTH), lambda i: (i, 0)),
                  pl.BlockSpec((tm, 2 * D_MODEL), lambda i: (i, 0)),
                  _row_spec(tm, D_MODEL, gt1.shape[0]),
                  _const_spec((1, D_MODEL)),
                  _row_spec(tm, D_MODEL, sc2.shape[0]),
                  _row_spec(tm, D_MODEL, sh2.shape[0]),
                  _const_spec((A_WIDTH, D_MODEL)),
                  _const_spec((B_WIDTH, D_MODEL)),
                  _const_spec((D_MODEL, D_MODEL)),
                  _const_spec((D_MODEL, LANES))],
        out_specs=[pl.BlockSpec((tm, D_MODEL), lambda i: (i, 0)),
                   pl.BlockSpec((tm, D_MODEL), lambda i: (i, 0)),
                   pl.BlockSpec((tm, LANES), lambda i: (i, 0))],
        out_shape=[jax.ShapeDtypeStruct((t, D_MODEL), F32),
                   jax.ShapeDtypeStruct((t, D_MODEL), BF16),
                   jax.ShapeDtypeStruct((t, LANES), F32)],
        compiler_params=_cparams("arbitrary"),
        name="merge_out",
    )(x, o_a, o_b, msig, gt1, g2, sc2, sh2, wa, wb, wout, wr)


def _route(scores, bias):
    lane = lax.broadcasted_iota(jnp.int32, scores.shape, 1)
    live = lane < N_EXPERTS
    v = jnp.where(live, scores + bias, -jnp.inf)
    sel = jnp.zeros(scores.shape, jnp.bool_)
    for _ in range(TOP_K):
        m = jnp.max(v, axis=-1, keepdims=True)
        first = jnp.min(jnp.where(v == m, lane, LANES), axis=-1, keepdims=True)
        hit = lane == first
        sel = jnp.logical_or(sel, hit)
        v = jnp.where(hit, -jnp.inf, v)
    picked = jnp.where(sel, scores, 0.0)
    return picked / jnp.sum(picked, axis=-1, keepdims=True) * ROUTED_SCALE


def _moe_kernel(h2_ref, score_ref, bias_ref, x1_ref, gt2_ref, wgu_ref, wd_ref, wsgu_ref, wsd_ref,
                out_ref, gate_ref, acc_ref):
    p = pl.program_id(1)
    h2 = h2_ref[...]

    @pl.when(p == 0)
    def _():
        gate = _route(score_ref[...], bias_ref[...])
        hi = gate.astype(BF16)
        lo = (gate - hi.astype(F32)).astype(BF16)
        gate_ref[...] = jnp.concatenate([hi, lo], axis=-1)
        s = jnp.dot(h2, wsgu_ref[...], preferred_element_type=F32)
        sact = _silu(s[:, :D_SHARED]) * s[:, D_SHARED:]
        acc_ref[...] = jnp.dot(sact.astype(BF16), wsd_ref[...], preferred_element_type=F32)

    row = lax.broadcasted_iota(jnp.int32, (2 * LANES, 2 * D_EXPERT), 0) % LANES
    col = lax.broadcasted_iota(jnp.int32, (2 * LANES, 2 * D_EXPERT), 1) // D_EXPERT
    onehot = jnp.where(row == 2 * p + col, 1.0, 0.0).astype(BF16)
    gsel = jnp.dot(gate_ref[...], onehot, preferred_element_type=F32)

    acts = []
    for e in range(2):
        au = jnp.dot(h2, wgu_ref[e], preferred_element_type=F32)
        acts.append(_silu(au[:, :D_EXPERT]) * au[:, D_EXPERT:])
    act = (jnp.concatenate(acts, axis=-1) * gsel).astype(BF16)
    acc_ref[...] += jnp.dot(act, wd_ref[0], preferred_element_type=F32)

    @pl.when(p == pl.num_programs(1) - 1)
    def _():
        out_ref[...] = x1_ref[...] + gt2_ref[...] * acc_ref[...]


def _moe(h2, scores, bias, x1, gt2, wgu, wd2, wsgu, wsd, tm):
    t = h2.shape[0]
    npairs = N_EXPERTS // 2
    return pl.pallas_call(
        _moe_kernel,
        grid=(t // tm, npairs),
        in_specs=[pl.BlockSpec((tm, D_MODEL), lambda i, p: (i, 0)),
                  pl.BlockSpec((tm, LANES), lambda i, p: (i, 0)),
                  pl.BlockSpec((1, LANES), lambda i, p: (0, 0)),
                  pl.BlockSpec((tm, D_MODEL), lambda i, p: (i, 0)),
                  (pl.BlockSpec((1, D_MODEL), lambda i, p: (0, 0)) if gt2.shape[0] == 1
                   else pl.BlockSpec((tm, D_MODEL), lambda i, p: (i, 0))),
                  pl.BlockSpec((2, D_MODEL, 2 * D_EXPERT), lambda i, p: (p, 0, 0)),
                  pl.BlockSpec((1, 2 * D_EXPERT, D_MODEL), lambda i, p: (p, 0, 0)),
                  pl.BlockSpec((D_MODEL, 2 * D_SHARED), lambda i, p: (0, 0)),
                  pl.BlockSpec((D_SHARED, D_MODEL), lambda i, p: (0, 0))],
        out_specs=pl.BlockSpec((tm, D_MODEL), lambda i, p: (i, 0)),
        out_shape=jax.ShapeDtypeStruct((t, D_MODEL), F32),
        scratch_shapes=[pltpu.VMEM((tm, 2 * LANES), BF16), pltpu.VMEM((tm, D_MODEL), F32)],
        compiler_params=_cparams("parallel", "arbitrary"),
        name="moe",
    )(h2, scores, bias, x1, gt2, wgu, wd2, wsgu, wsd)


SEL_TILE = 512
SEL_TILE_BLOCKS = SEL_TILE // SLC_BLOCK
WIN_TILES = WINDOW // Q_BLOCK + 1
CMP_ROWS = CMP_STRIDE * N_KV_SLOTS * A_KV_HEADS * HEAD_DIM
CMP_COLS = 2 * A_KV_HEADS * HEAD_DIM


def _head_slope(h):
    return float(2.0 ** (-8.0 * (h + 1) / A_HEADS))


def _compress_kernel(x_ref, w1_ref, w2_ref, pe1_ref, pe2_ref, gk_ref, kc_ref, vct_ref, f1_ref, f2_ref, vc_ref,
                     *, nb, tile):
    i = pl.program_id(0)
    nsb = nb // RATIO

    half = A_KV_HEADS * HEAD_DIM

    @pl.when(i == 0)
    def _():
        for s in range(2):
            f2_ref[s, pl.ds(nb, 8), :] = jnp.zeros((8, half), F32)

    x = x_ref[...].astype(BF16)
    row0 = pl.multiple_of(i * tile, tile)
    first = jnp.dot(x, w1_ref[...], preferred_element_type=F32)
    second = jnp.dot(x, w2_ref[...], preferred_element_type=F32)
    for s in range(2):
        f1_ref[s, pl.ds(row0, tile), :] = first[:, s * half:(s + 1) * half]
        f2_ref[s, pl.ds(row0, tile), :] = second[:, s * half:(s + 1) * half]

    @pl.when(i == pl.num_programs(0) - 1)
    def _():
        bias = (jnp.dot(pe1_ref[...], w1_ref[...], preferred_element_type=F32)
                + jnp.dot(pe2_ref[...], w2_ref[...], preferred_element_type=F32))[0:1]
        for r in range(RATIO):
            blk = [f1_ref[s, pl.ds(r, nsb, stride=RATIO), :] + f2_ref[s, pl.ds(r + 1, nsb, stride=RATIO), :]
                   + bias[:, s * half:(s + 1) * half] for s in range(2)]
            for k in range(A_KV_HEADS):
                kc = _rms_rows(blk[0][:, k * HEAD_DIM:(k + 1) * HEAD_DIM]) * gk_ref[...]
                kc_ref[k, pl.ds(r * nsb, nsb), :] = kc.astype(BF16)
            vc_ref[pl.ds(r * nsb, nsb), :] = blk[1]
        vct = vc_ref[...].T
        for k in range(A_KV_HEADS):
            vct_ref[k] = vct[k * HEAD_DIM:(k + 1) * HEAD_DIM].astype(BF16)


def _nsa_compress(kv_rows, w1, w2, pe1, pe2, gk):
    t = kv_rows.shape[0]
    nb = t // CMP_STRIDE
    tile = min(256, nb)
    x16 = kv_rows.reshape(nb, CMP_ROWS)
    return pl.pallas_call(
        functools.partial(_compress_kernel, nb=nb, tile=tile),
        grid=(nb // tile,),
        in_specs=[pl.BlockSpec((tile, CMP_ROWS), lambda i: (i, 0)),
                  _const_spec((CMP_ROWS, CMP_COLS)), _const_spec((CMP_ROWS, CMP_COLS)),
                  _const_spec((16, CMP_ROWS)), _const_spec((16, CMP_ROWS)), _const_spec((1, HEAD_DIM))],
        out_specs=[_const_spec((A_KV_HEADS, nb, HEAD_DIM)), _const_spec((A_KV_HEADS, HEAD_DIM, nb))],
        out_shape=[jax.ShapeDtypeStruct((A_KV_HEADS, nb, HEAD_DIM), BF16),
                   jax.ShapeDtypeStruct((A_KV_HEADS, HEAD_DIM, nb), BF16)],
        scratch_shapes=[pltpu.VMEM((2, nb + 8, CMP_COLS // 2), F32), pltpu.VMEM((2, nb + 8, CMP_COLS // 2), F32),
                        pltpu.VMEM((nb, A_KV_HEADS * HEAD_DIM), F32)],
        compiler_params=_cparams("arbitrary"),
        name="nsa_compress",
    )(x16, w1, w2, pe1, pe2, gk)


def _nsa_rows_kernel(kv_ref, win_ref, gs_ref, gw_ref, ks_ref, vst_ref, kw_ref, vwt_ref):
    half = A_KV_HEADS * HEAD_DIM
    kv = kv_ref[...]
    win = win_ref[...]
    vst = kv[:, 3 * half:4 * half].T
    vwt = win[:, half:2 * half].T
    for k in range(A_KV_HEADS):
        lo, hi = k * HEAD_DIM, (k + 1) * HEAD_DIM
        ks_ref[k, 0] = (_rms_rows(kv[:, 2 * half + lo:2 * half + hi]) * gs_ref[...]).astype(BF16)
        kw = (_rms_rows(win[:, lo:hi]) * gw_ref[...]).astype(BF16)
        for j in range(SEL_TILE // Q_BLOCK):
            kw_ref[k, j] = kw[j * Q_BLOCK:(j + 1) * Q_BLOCK]
            vwt_ref[k, j] = vwt[lo:hi, j * Q_BLOCK:(j + 1) * Q_BLOCK].astype(BF16)
        vst_ref[k, 0] = vst[lo:hi].astype(BF16)


def _nsa_rows(kv_rows, win_rows, gs, gw):
    t = kv_rows.shape[0]
    nt = t // SEL_TILE
    per = SEL_TILE // Q_BLOCK
    return pl.pallas_call(
        _nsa_rows_kernel,
        grid=(nt,),
        in_specs=[pl.BlockSpec((SEL_TILE, kv_rows.shape[1]), lambda i: (i, 0)),
                  pl.BlockSpec((SEL_TILE, win_rows.shape[1]), lambda i: (i, 0)),
                  _const_spec((1, HEAD_DIM)), _const_spec((1, HEAD_DIM))],
        out_specs=[pl.BlockSpec((A_KV_HEADS, 1, SEL_TILE, HEAD_DIM), lambda i: (0, i, 0, 0)),
                   pl.BlockSpec((A_KV_HEADS, 1, HEAD_DIM, SEL_TILE), lambda i: (0, i, 0, 0)),
                   pl.BlockSpec((A_KV_HEADS, per, Q_BLOCK, HEAD_DIM), lambda i: (0, i, 0, 0)),
                   pl.BlockSpec((A_KV_HEADS, per, HEAD_DIM, Q_BLOCK), lambda i: (0, i, 0, 0))],
        out_shape=[jax.ShapeDtypeStruct((A_KV_HEADS, nt, SEL_TILE, HEAD_DIM), BF16),
                   jax.ShapeDtypeStruct((A_KV_HEADS, nt, HEAD_DIM, SEL_TILE), BF16),
                   jax.ShapeDtypeStruct((A_KV_HEADS, nt * per, Q_BLOCK, HEAD_DIM), BF16),
                   jax.ShapeDtypeStruct((A_KV_HEADS, nt * per, HEAD_DIM, Q_BLOCK), BF16)],
        compiler_params=_cparams("arbitrary"),
        name="nsa_rows",
    )(kv_rows, win_rows, gs, gw)


def _nsa_cmp_kernel(q_ref, gq_ref, kc_ref, vct_ref, qt_ref, oct_ref, sel_ref, any_ref, *, nb):
    nsb = nb // RATIO
    shift = nsb.bit_length() - 1
    q0 = pl.program_id(0) * Q_BLOCK
    qt_full = q_ref[...].T
    crow = lax.broadcasted_iota(jnp.int32, (nb, Q_BLOCK), 0)
    cpos = (((crow & (nsb - 1)) * RATIO + (crow >> shift)) * CMP_STRIDE) + (CMP_BLOCK - 1)
    qpos = q0 + lax.broadcasted_iota(jnp.int32, (nb, Q_BLOCK), 1)
    dist = qpos - cpos
    valid = dist >= 0
    distf = dist.astype(F32)
    qpos_row = q0 + lax.broadcasted_iota(jnp.int32, (1, Q_BLOCK), 1)
    any_valid = jnp.where(qpos_row >= CMP_BLOCK - 1, 1.0, 0.0)
    jrow = lax.broadcasted_iota(jnp.int32, (nsb, Q_BLOCK), 0)
    cur = (q0 + lax.broadcasted_iota(jnp.int32, (nsb, Q_BLOCK), 1)) >> (SLC_BLOCK.bit_length() - 1)
    forced = jnp.logical_or(jrow == cur, jrow == 0)
    avail = jrow <= cur

    for k in range(A_KV_HEADS):
        slabs = []
        for g in range(A_GROUP):
            h = k * A_GROUP + g
            slab = qt_full[h * HEAD_DIM:(h + 1) * HEAD_DIM]
            inv = lax.rsqrt(jnp.mean(slab * slab, axis=0, keepdims=True) + EPS)
            slabs.append(slab * inv * gq_ref[...] * HEAD_DIM ** -0.5)
        qt = jnp.concatenate(slabs, axis=1).astype(BF16)
        qt_ref[0, k] = qt
        st = jnp.dot(kc_ref[k], qt, preferred_element_type=F32)
        pg = jnp.zeros((nb, Q_BLOCK), F32)
        ps = []
        for g in range(A_GROUP):
            s = st[:, g * Q_BLOCK:(g + 1) * Q_BLOCK]
            s = jnp.where(valid, s - _head_slope(k * A_GROUP + g) * distf, NEG)
            e = jnp.exp(s - jnp.max(s, axis=0, keepdims=True))
            p = e / jnp.sum(e, axis=0, keepdims=True) * any_valid
            pg = pg + p
            ps.append(p.astype(BF16))
        oct_ref[0, k] = jnp.dot(vct_ref[k], jnp.concatenate(ps, axis=1), preferred_element_type=F32)
        last = pg[3 * nsb:4 * nsb]
        prev = jnp.where(jrow == 0, 0.0, pltpu.roll(last, 1, axis=0))
        imp = pg[0:nsb] + pg[nsb:2 * nsb] + pg[2 * nsb:3 * nsb] + last + prev
        x = jnp.where(forced, BIG, jnp.where(avail, imp, NEG))
        picked = jnp.zeros((nsb, Q_BLOCK), jnp.bool_)
        for _ in range(SLC_TOPN):
            m = jnp.max(x, axis=0, keepdims=True)
            first = jnp.min(jnp.where(x == m, jrow, nsb), axis=0, keepdims=True)
            hit = jrow == first
            picked = jnp.logical_or(picked, hit)
            x = jnp.where(hit, -jnp.inf, x)
        sel = jnp.where(jnp.logical_and(picked, avail), 1.0, 0.0)
        sel_ref[0, k] = sel
        any_ref[0, k] = jnp.max(sel, axis=1, keepdims=True)


def _nsa_cmp(q_raw, gq_col, kc, vct):
    t = q_raw.shape[0]
    nqb = t // Q_BLOCK
    nb = kc.shape[1]
    nsb = nb // RATIO
    shp = (nqb, A_KV_HEADS, HEAD_DIM, A_GROUP * Q_BLOCK)
    blk = lambda *s: pl.BlockSpec((1,) + s, lambda i: (i,) + (0,) * len(s))
    return pl.pallas_call(
        functools.partial(_nsa_cmp_kernel, nb=nb),
        grid=(nqb,),
        in_specs=[pl.BlockSpec((Q_BLOCK, A_WIDTH), lambda i: (i, 0)),
                  _const_spec((HEAD_DIM, 1)),
                  _const_spec((A_KV_HEADS, nb, HEAD_DIM)),
                  _const_spec((A_KV_HEADS, HEAD_DIM, nb))],
        out_specs=[blk(*shp[1:]), blk(*shp[1:]), blk(A_KV_HEADS, nsb, Q_BLOCK), blk(A_KV_HEADS, nsb, 1)],
        out_shape=[jax.ShapeDtypeStruct(shp, BF16), jax.ShapeDtypeStruct(shp, F32),
                   jax.ShapeDtypeStruct((nqb, A_KV_HEADS, nsb, Q_BLOCK), F32),
                   jax.ShapeDtypeStruct((nqb, A_KV_HEADS, nsb, 1), F32)],
        compiler_params=_cparams("arbitrary"),
        name="nsa_compressed",
    )(q_raw, gq_col, kc, vct)


def _nsa_sel_kernel(tiles_ref, counts_ref, qt_ref, oct_ref, sel_ref, small_ref, ks_ref, vst_ref, kw_ref, vwt_ref,
                    o_ref, *, ntiles):
    qb = pl.program_id(0)
    q0 = qb * Q_BLOCK
    gates_t = jax.nn.sigmoid(small_ref[...]).T
    qpos = q0 + lax.broadcasted_iota(jnp.int32, (SEL_TILE, Q_BLOCK), 1)
    krow = lax.broadcasted_iota(jnp.int32, (SEL_TILE, Q_BLOCK), 0)
    wrow = lax.broadcasted_iota(jnp.int32, (WIN_TILES * Q_BLOCK, Q_BLOCK), 0)
    wdist = (q0 + lax.broadcasted_iota(jnp.int32, (WIN_TILES * Q_BLOCK, Q_BLOCK), 1)) - (q0 - WINDOW + wrow)
    wvalid = jnp.logical_and(jnp.logical_and(wdist >= 0, wdist <= WINDOW), q0 - WINDOW + wrow >= 0)
    wdistf = wdist.astype(F32)
    outs = []
    for k in range(A_KV_HEADS):
        qt = qt_ref[0, k]

        def tile_step(i, carry, k=k, qt=qt):
            m_prev, l_prev, acc = carry
            t = tiles_ref[(qb * A_KV_HEADS + k) * ntiles + i]
            st = jnp.dot(ks_ref[k, t], qt, preferred_element_type=F32)
            mrows = sel_ref[0, k, pl.ds(pl.multiple_of(t * SEL_TILE_BLOCKS, SEL_TILE_BLOCKS), SEL_TILE_BLOCKS), :]
            picked = jnp.concatenate(
                [jnp.broadcast_to(mrows[b:b + 1], (SLC_BLOCK, Q_BLOCK)) for b in range(SEL_TILE_BLOCKS)], axis=0)
            dist = qpos - (t * SEL_TILE + krow)
            valid = jnp.logical_and(picked > 0.5, dist >= 0)
            distf = dist.astype(F32)
            ms, ls, ps, alphas = [], [], [], []
            for g in range(A_GROUP):
                lanes = slice(g * Q_BLOCK, (g + 1) * Q_BLOCK)
                s = jnp.where(valid, st[:, lanes] - _head_slope(k * A_GROUP + g) * distf, NEG)
                m_new = jnp.maximum(m_prev[:, lanes], jnp.max(s, axis=0, keepdims=True))
                alpha = jnp.exp(m_prev[:, lanes] - m_new)
                p = jnp.where(valid, jnp.exp(s - m_new), 0.0)
                ms.append(m_new)
                ls.append(alpha * l_prev[:, lanes] + jnp.sum(p, axis=0, keepdims=True))
                alphas.append(alpha)
                ps.append(p.astype(BF16))
            pv = jnp.dot(vst_ref[k, t], jnp.concatenate(ps, axis=1), preferred_element_type=F32)
            return (jnp.concatenate(ms, axis=1), jnp.concatenate(ls, axis=1),
                    jnp.concatenate(alphas, axis=1) * acc + pv)

        init = (jnp.full((1, A_GROUP * Q_BLOCK), NEG, F32), jnp.zeros((1, A_GROUP * Q_BLOCK), F32),
                jnp.zeros((HEAD_DIM, A_GROUP * Q_BLOCK), F32))
        _, l_fin, acc = lax.fori_loop(0, counts_ref[qb * A_KV_HEADS + k], tile_step, init)
        os_t = acc / l_fin

        wk, wv = [], []
        for j in range(WIN_TILES):
            ti = jnp.maximum(qb - (WIN_TILES - 1) + j, 0)
            wk.append(kw_ref[k, ti])
            wv.append(vwt_ref[k, ti])
        sw = jnp.dot(jnp.concatenate(wk, axis=0), qt, preferred_element_type=F32)
        pw = []
        for g in range(A_GROUP):
            s = jnp.where(wvalid, sw[:, g * Q_BLOCK:(g + 1) * Q_BLOCK] - _head_slope(k * A_GROUP + g) * wdistf, NEG)
            e = jnp.exp(s - jnp.max(s, axis=0, keepdims=True))
            pw.append((e / jnp.sum(e, axis=0, keepdims=True)).astype(BF16))
        ow_t = jnp.dot(jnp.concatenate(wv, axis=1), jnp.concatenate(pw, axis=1), preferred_element_type=F32)

        oc_t = oct_ref[0, k]
        for g in range(A_GROUP):
            h = k * A_GROUP + g
            lanes = slice(g * Q_BLOCK, (g + 1) * Q_BLOCK)
            outs.append(gates_t[3 * h:3 * h + 1] * oc_t[:, lanes] + gates_t[3 * h + 1:3 * h + 2] * os_t[:, lanes]
                        + gates_t[3 * h + 2:3 * h + 3] * ow_t[:, lanes])
    o_ref[...] = jnp.concatenate(outs, axis=0).T


def _nsa_sel(tiles, counts, qt, oct, sel, small, ks, vst, kw, vwt):
    nqb = qt.shape[0]
    t = nqb * Q_BLOCK
    ntiles = ks.shape[1]
    nsb = sel.shape[2]
    shp = (A_KV_HEADS, HEAD_DIM, A_GROUP * Q_BLOCK)
    full = lambda a: pl.BlockSpec(a.shape, lambda i, *_: (0,) * a.ndim)
    grid_spec = pltpu.PrefetchScalarGridSpec(
        num_scalar_prefetch=2,
        grid=(nqb,),
        in_specs=[pl.BlockSpec((1,) + shp, lambda i, *_: (i, 0, 0, 0)),
                  pl.BlockSpec((1,) + shp, lambda i, *_: (i, 0, 0, 0)),
                  pl.BlockSpec((1, A_KV_HEADS, nsb, Q_BLOCK), lambda i, *_: (i, 0, 0, 0)),
                  pl.BlockSpec((Q_BLOCK, SMALL_W), lambda i, *_: (i, 0)),
                  full(ks), full(vst), full(kw), full(vwt)],
        out_specs=pl.BlockSpec((Q_BLOCK, A_WIDTH), lambda i, *_: (i, 0)),
    )
    return pl.pallas_call(
        functools.partial(_nsa_sel_kernel, ntiles=ntiles),
        grid_spec=grid_spec,
        out_shape=jax.ShapeDtypeStruct((t, A_WIDTH), F32),
        compiler_params=_cparams("arbitrary"),
        name="nsa_selected",
    )(tiles, counts, qt, oct, sel, small, ks, vst, kw, vwt)


def _cmp_weights(w_cmp, pe_cmp):
    eye_k = jnp.eye(A_KV_HEADS, dtype=F32)
    ws, pes = [], []
    for half in range(2):
        w = w_cmp[:, half * CMP_STRIDE:(half + 1) * CMP_STRIDE]
        full = jnp.einsum('sjde,ts,kl->jtkdsle', w, jnp.eye(N_KV_SLOTS, 2, dtype=F32), eye_k)
        ws.append(full.reshape(CMP_ROWS, CMP_COLS).astype(BF16))
        pe = pe_cmp[:, half * CMP_STRIDE:(half + 1) * CMP_STRIDE]
        pe = jnp.pad(pe, ((0, N_KV_SLOTS - 2), (0, 0), (0, 0)))
        pe = jnp.broadcast_to(pe.transpose(1, 0, 2)[:, :, None, :], (CMP_STRIDE, N_KV_SLOTS, A_KV_HEADS, HEAD_DIM))
        pes.append(jnp.broadcast_to(pe.reshape(1, CMP_ROWS), (16, CMP_ROWS)).astype(BF16))
    return ws[0], ws[1], pes[0], pes[1]


def _nsa_prompt(q_raw, kv_rows, win_rows, small, q_norm_g, k_norm_g, w_cmp, pe_cmp):
    t = q_raw.shape[0]
    w1, w2, pe1, pe2 = _cmp_weights(w_cmp, pe_cmp)
    kc, vct = _nsa_compress(kv_rows, w1, w2, pe1, pe2, k_norm_g[0].reshape(1, HEAD_DIM))
    ks, vst, kw, vwt = _nsa_rows(kv_rows, win_rows, k_norm_g[1].reshape(1, HEAD_DIM), k_norm_g[2].reshape(1, HEAD_DIM))
    qt, oct, sel, bany = _nsa_cmp(q_raw, q_norm_g.reshape(HEAD_DIM, 1), kc, vct)
    nqb, ntiles = t // Q_BLOCK, t // SEL_TILE
    flags = bany.reshape(nqb, A_KV_HEADS, ntiles, SEL_TILE_BLOCKS).max(-1) > 0.5
    order = jnp.argsort(jnp.logical_not(flags), axis=-1, stable=True).astype(jnp.int32)
    counts = flags.sum(-1).astype(jnp.int32)
    return _nsa_sel(order.reshape(-1), counts.reshape(-1), qt, oct, sel, small, ks, vst, kw, vwt)


DELTA_STEP = 2 * DELTA_CHUNK
CONV_HALO = 8
SM_A = 3 * A_HEADS
SM_B = SM_A + B_HEADS


def _softplus(x):
    return jnp.maximum(x, 0.0) + jnp.log(1.0 + jnp.exp(-jnp.abs(x)))


def _l2_rows(x):
    return x * lax.rsqrt(jnp.sum(x * x, axis=-1, keepdims=True) + EPS)


def _dot_nt3(a, b):
    hi = b.astype(BF16)
    lo = (b - hi.astype(F32)).astype(BF16)
    return lax.dot_general(_split3(a), jnp.concatenate([hi, hi, lo], axis=1), (((1,), (1,)), ((), ())),
                           preferred_element_type=F32)


def _delta_kernel(x_ref, small_ref, gate_ref, cw_ref, alog_ref, dtb_ref, on_ref, o_ref, s_out_ref, xbuf, s_ref):
    i = pl.program_id(0)
    n = DELTA_STEP

    @pl.when(i == 0)
    def _():
        xbuf[0:CONV_HALO, :] = jnp.zeros((CONV_HALO, CONV_CH), F32)
        s_ref[...] = jnp.zeros(s_ref.shape, F32)

    x = x_ref[...]
    xbuf[CONV_HALO:CONV_HALO + n, :] = x
    base = CONV_HALO - (CONV_W - 1)
    acc = xbuf[pl.ds(base, n), :] * cw_ref[0:1, :]
    for j in range(1, CONV_W):
        acc = acc + xbuf[pl.ds(base + j, n), :] * cw_ref[j:j + 1, :]
    c = _silu(acc)
    xbuf[0:CONV_HALO, :] = x[n - CONV_HALO:n, :]

    sm = small_ref[...]
    g_all = -jnp.exp(alog_ref[...]) * _softplus(sm + dtb_ref[...])
    beta_all = jax.nn.sigmoid(sm)
    r = lax.broadcasted_iota(jnp.int32, (n, n), 0)
    cc = lax.broadcasted_iota(jnp.int32, (n, n), 1)
    same = (r >= DELTA_CHUNK) == (cc >= DELTA_CHUNK)
    incl = jnp.logical_and(same, r >= cc)
    strict = jnp.logical_and(same, r > cc)
    gcum = jnp.dot(jnp.where(incl, 1.0, 0.0), g_all, preferred_element_type=F32, precision=HIGHEST)
    glast = jnp.dot(jnp.where(same, 1.0, 0.0), g_all, preferred_element_type=F32, precision=HIGHEST)
    gcum_t = gcum.T

    nk = B_HEADS * B_DK
    for h in range(B_HEADS):
        gc = gcum[:, SM_A + h:SM_A + h + 1]
        gl = glast[:, SM_A + h:SM_A + h + 1]
        decay = jnp.exp(jnp.where(incl, gc - gcum_t[SM_A + h:SM_A + h + 1, :], -jnp.inf))
        bcol = beta_all[:, SM_B + h:SM_B + h + 1]
        q = _l2_rows(c[:, h * B_DK:(h + 1) * B_DK]) * B_DK ** -0.5
        k = _l2_rows(c[:, nk + h * B_DK:nk + (h + 1) * B_DK])
        v = c[:, 2 * nk + h * B_DV:2 * nk + (h + 1) * B_DV]
        pw = -jnp.where(strict, _dot_nt3(k, k) * decay * bcol, 0.0)
        tm = pw
        for _ in range(DELTA_CHUNK.bit_length() - 2):
            pw = _mm3(pw, _stack3(pw))
            tm = tm + pw + _mm3(tm, _stack3(pw))
        eg = jnp.exp(gc)
        rhs_u = v * bcol
        rhs_w = k * bcol * eg
        solved = _mm3(tm, _stack3(jnp.concatenate([rhs_u, rhs_w], axis=1)))
        u = rhs_u + solved[:, :B_DV]
        w = rhs_w + solved[:, B_DV:]
        qk = _dot_nt3(q, k) * decay
        q_dec = q * eg
        kd_t = (k * jnp.exp(gl - gc)).T
        s = s_ref[h]
        outs = []
        for ci in range(2):
            rows = slice(ci * DELTA_CHUNK, (ci + 1) * DELTA_CHUNK)
            s3 = _stack3(s)
            v_new = u[rows] - _mm3(w[rows], s3)
            outs.append(_mm3(q_dec[rows], s3) + _mm(qk[rows, rows], v_new.astype(BF16), False))
            g_end = jnp.exp(glast[ci * DELTA_CHUNK:ci * DELTA_CHUNK + 1, SM_A + h:SM_A + h + 1])
            s = s * g_end + _mm3(kd_t[:, rows], _stack3(v_new))
        s_ref[h] = s
        o = jnp.concatenate(outs, axis=0)
        o = _rms_rows(o) * on_ref[...] * _silu(gate_ref[:, h * B_DV:(h + 1) * B_DV])
        o_ref[:, h * B_DV:(h + 1) * B_DV] = o

    @pl.when(i == pl.num_programs(0) - 1)
    def _():
        s_out_ref[...] = s_ref[...]


def _lane_slab(vals, lane0):
    return jnp.pad(vals, (lane0, LANES - lane0 - vals.shape[0])).reshape(1, LANES)


def _delta_prompt(qkv_b, small, gate_b, conv_w, a_log, dt_bias, o_norm_g):
    t = qkv_b.shape[0]
    n = DELTA_STEP
    return pl.pallas_call(
        _delta_kernel,
        grid=(t // n,),
        in_specs=[pl.BlockSpec((n, CONV_CH), lambda i: (i, 0)),
                  pl.BlockSpec((n, SMALL_W), lambda i: (i, 0)),
                  pl.BlockSpec((n, B_WIDTH), lambda i: (i, 0)),
                  _const_spec((CONV_W, CONV_CH)), _const_spec((1, LANES)), _const_spec((1, LANES)),
                  _const_spec((1, B_DV))],
        out_specs=[pl.BlockSpec((n, B_WIDTH), lambda i: (i, 0)), _const_spec((B_HEADS, B_DK, B_DV))],
        out_shape=[jax.ShapeDtypeStruct((t, B_WIDTH), F32), jax.ShapeDtypeStruct((B_HEADS, B_DK, B_DV), F32)],
        scratch_shapes=[pltpu.VMEM((CONV_HALO + n, CONV_CH), F32), pltpu.VMEM((B_HEADS, B_DK, B_DV), F32)],
        compiler_params=_cparams("arbitrary"),
        name="delta_prompt",
    )(qkv_b, small, gate_b, conv_w, _lane_slab(a_log, SM_A), _lane_slab(dt_bias, SM_A), o_norm_g.reshape(1, B_DV))


PAGE_ROWS = 128
PAGES_PER_STEP = 32
STRIPES_PER_PAGE = PAGE_ROWS // CMP_STRIDE
HALF = A_KV_HEADS * HEAD_DIM
ROW_W = N_KV_SLOTS * HALF
SUB = 8
N_PICK = SLC_TOPN - 2


def _dot_nt_exact(a, b):
    return lax.dot_general(a, b, (((1,), (1,)), ((), ())), preferred_element_type=F32, precision=HIGHEST)


def _row_slopes(k):
    row = lax.broadcasted_iota(jnp.int32, (SUB, 1), 0)
    s = jnp.zeros((SUB, 1), F32)
    for g in range(A_GROUP):
        s = jnp.where(row == g, _head_slope(k * A_GROUP + g), s)
    return s


def _head_rows(x, k, gq):
    rows = [x[:, (k * A_GROUP + g) * HEAD_DIM:(k * A_GROUP + g + 1) * HEAD_DIM] for g in range(A_GROUP)]
    q = jnp.concatenate(rows + [jnp.zeros((SUB - A_GROUP, HEAD_DIM), F32)], axis=0)
    return _rms_rows(q) * gq * HEAD_DIM ** -0.5


def _nsa_s1_kernel(pt_ref, *refs, nb, past):
    del pt_ref
    npg = PAGES_PER_STEP
    kpages, vpages = refs[:npg], refs[npg:2 * npg]
    q_ref, gq_ref, gk_ref, w_ref, bias_ref = refs[2 * npg:2 * npg + 5]
    qn_ref, oc_ref, idx_ref = refs[2 * npg + 5:2 * npg + 8]
    f1_ref, f2_ref, rows_ref = refs[2 * npg + 8:]
    c = pl.program_id(1)
    nsb = nb // RATIO
    rows_step = npg * STRIPES_PER_PAGE

    @pl.when(jnp.logical_and(pl.program_id(0) == 0, c == 0))
    def _():
        for s in range(2):
            f2_ref[s, pl.ds(nb, SUB), :] = jnp.zeros((SUB, HALF), F32)

    row0 = pl.multiple_of(c * rows_step, rows_step)
    for s, pages in enumerate((kpages, vpages)):
        for i, p in enumerate(pages):
            rows_ref[i] = p[0].T
        acc = jnp.zeros((rows_step, 2 * HALF), F32)
        for j in range(CMP_STRIDE):
            x = jnp.concatenate([rows_ref[i, pl.ds(j, STRIPES_PER_PAGE, stride=CMP_STRIDE), :] for i in range(npg)],
                                axis=0)
            acc = acc + jnp.dot(_split3(x), w_ref[s, j], preferred_element_type=F32)
        f1_ref[s, pl.ds(row0, rows_step), :] = acc[:, :HALF]
        f2_ref[s, pl.ds(row0, rows_step), :] = acc[:, HALF:]

    @pl.when(c == pl.num_programs(1) - 1)
    def _():
        lane = lax.broadcasted_iota(jnp.int32, (SUB, nb), 1)
        shift = nsb.bit_length() - 1
        cpos = (((lane & (nsb - 1)) * RATIO + (lane >> shift)) * CMP_STRIDE) + (CMP_BLOCK - 1)
        dist = past - cpos
        valid = dist >= 0
        distf = dist.astype(F32)
        row = lax.broadcasted_iota(jnp.int32, (SUB, nb), 0)
        kc = [[], []]
        vc = []
        for r in range(RATIO):
            kb = f1_ref[0, pl.ds(r, nsb, stride=RATIO), :] + f2_ref[0, pl.ds(r + 1, nsb, stride=RATIO), :] + bias_ref[0]
            vc.append(f1_ref[1, pl.ds(r, nsb, stride=RATIO), :] + f2_ref[1, pl.ds(r + 1, nsb, stride=RATIO), :]
                      + bias_ref[1])
            for k in range(A_KV_HEADS):
                kc[k].append(_rms_rows(kb[:, k * HEAD_DIM:(k + 1) * HEAD_DIM]) * gk_ref[...])
        vc = jnp.concatenate(vc, axis=0)
        qn, oc, imps = [], [], []
        for k in range(A_KV_HEADS):
            q = _head_rows(q_ref[0], k, gq_ref[...])
            qn.append(q)
            s = _dot_nt_exact(q, jnp.concatenate(kc[k], axis=0))
            s = jnp.where(valid, s - _row_slopes(k) * distf, NEG)
            e = jnp.exp(s - jnp.max(s, axis=-1, keepdims=True))
            p = e / jnp.sum(e, axis=-1, keepdims=True)
            p = jnp.where(jnp.logical_and(row < A_GROUP, past >= CMP_BLOCK - 1), p, 0.0)
            oc.append(_mm(p, vc[:, k * HEAD_DIM:(k + 1) * HEAD_DIM], True))
            pg = jnp.broadcast_to(jnp.sum(p, axis=0, keepdims=True), (SUB, nb))
            last = pg[:, 3 * nsb:4 * nsb]
            lane_b = lax.broadcasted_iota(jnp.int32, (SUB, nsb), 1)
            prev = jnp.where(lane_b == 0, 0.0, pltpu.roll(last, 1, axis=1))
            imps.append(pg[:, 0:nsb] + pg[:, nsb:2 * nsb] + pg[:, 2 * nsb:3 * nsb] + last + prev)
        qn_ref[0] = jnp.concatenate(qn, axis=1)
        oc_ref[0] = jnp.concatenate(oc, axis=1)
        lane_s = lax.broadcasted_iota(jnp.int32, (SUB, nsb), 1)
        row_s = lax.broadcasted_iota(jnp.int32, (SUB, nsb), 0)
        x = jnp.zeros((SUB, nsb), F32)
        for k in range(A_KV_HEADS):
            x = jnp.where(row_s == k, imps[k], x)
        x = jnp.where(lane_s == 0, -jnp.inf, x)
        out_lane = lax.broadcasted_iota(jnp.int32, (SUB, LANES), 1)
        picks = jnp.zeros((SUB, LANES), jnp.int32)
        for i in range(N_PICK):
            m = jnp.max(x, axis=-1, keepdims=True)
            first = jnp.min(jnp.where(x == m, lane_s, nsb), axis=-1, keepdims=True)
            picks = jnp.where(out_lane == i + 1, first, picks)
            x = jnp.where(lane_s == first, -jnp.inf, x)
        idx_ref[0] = picks


def _nsa_sample_cmp(q_raw, cache_t, page_table, gq, gk, wcat, bias, past):
    db = q_raw.shape[0]
    n_pages = page_table.shape[1]
    nb = past // CMP_STRIDE
    nch = n_pages // PAGES_PER_STEP

    def page_spec(p, slot):
        return pl.BlockSpec((1, HALF, PAGE_ROWS),
                            lambda b, c, pt: (pt[jnp.minimum(b, db - 1) * n_pages + c * PAGES_PER_STEP + p], slot, 0))

    cst = lambda shape: pl.BlockSpec(shape, lambda b, c, pt: (0,) * len(shape))
    per_b = lambda: pl.BlockSpec((1, SUB, LANES), lambda b, c, pt: (b, 0, 0))
    grid_spec = pltpu.PrefetchScalarGridSpec(
        num_scalar_prefetch=1,
        grid=(db, nch),
        in_specs=([page_spec(p, 0) for p in range(PAGES_PER_STEP)] + [page_spec(p, 1) for p in range(PAGES_PER_STEP)]
                  + [pl.BlockSpec((1, 1, A_WIDTH), lambda b, c, pt: (b, 0, 0)),
                     cst((1, HEAD_DIM)), cst((1, HEAD_DIM)), cst(wcat.shape), cst(bias.shape)]),
        out_specs=[per_b(), per_b(), per_b()],
        scratch_shapes=[pltpu.VMEM((2, nb + SUB, HALF), F32), pltpu.VMEM((2, nb + SUB, HALF), F32),
                        pltpu.VMEM((PAGES_PER_STEP, PAGE_ROWS, HALF), F32)],
    )
    return pl.pallas_call(
        functools.partial(_nsa_s1_kernel, nb=nb, past=past),
        grid_spec=grid_spec,
        out_shape=[jax.ShapeDtypeStruct((db, SUB, LANES), F32), jax.ShapeDtypeStruct((db, SUB, LANES), F32),
                   jax.ShapeDtypeStruct((db, SUB, LANES), jnp.int32)],
        compiler_params=_cparams("arbitrary", "arbitrary"),
        name="nsa_sample_compressed",
    )(page_table.reshape(-1), *([cache_t] * (2 * PAGES_PER_STEP)), q_raw.reshape(db, 1, A_WIDTH), gq, gk, wcat, bias)


def _nsa_s2_kernel(pt_ref, pick_ref, *refs, past):
    del pt_ref
    nsel = SLC_TOPN - 1
    blocks = refs[:A_KV_HEADS * nsel]
    (win_ref, kvn_ref, winn_ref, qn_ref, oc_ref, small_ref, gs_ref, gsc_ref, gw_ref) = refs[A_KV_HEADS * nsel:-1]
    o_ref = refs[-1]
    b = pl.program_id(0)
    slane = lax.broadcasted_iota(jnp.int32, (1, nsel * PAGE_ROWS), 1)
    spage = slane >> (PAGE_ROWS.bit_length() - 1)
    srow = slane & (PAGE_ROWS - 1)
    gates = jax.nn.sigmoid(small_ref[0])
    row = lax.broadcasted_iota(jnp.int32, (SUB, 1), 0)
    wlane = lax.broadcasted_iota(jnp.int32, (1, WINDOW), 1)
    wdist = (WINDOW - wlane).astype(F32)
    outs = []
    for k in range(A_KV_HEADS):
        lanes = slice(k * HEAD_DIM, (k + 1) * HEAD_DIM)
        q = qn_ref[0][:, lanes]
        slopes = _row_slopes(k)

        def finish(s, s_new, pv, v_new):
            m = jnp.maximum(jnp.max(s, axis=-1, keepdims=True), s_new)
            e, e_new = jnp.exp(s - m), jnp.exp(s_new - m)
            den = jnp.sum(e, axis=-1, keepdims=True) + e_new
            return (pv(e) + e_new * v_new) / den

        kts, vts = [], []
        for i in range(nsel):
            page = blocks[k * nsel + i][0]
            kt = page[lanes]
            kts.append(kt * lax.rsqrt(jnp.mean(kt * kt, axis=0, keepdims=True) + EPS) * gsc_ref[...])
            vts.append(page[HALF + k * HEAD_DIM:HALF + (k + 1) * HEAD_DIM])
        kt_all, vt_all = jnp.concatenate(kts, axis=1), jnp.concatenate(vts, axis=1)
        blk_id = jnp.zeros((1, nsel * PAGE_ROWS), jnp.int32)
        for i in range(nsel):
            blk_id = jnp.where(spage == i, pick_ref[(b * A_KV_HEADS + k) * nsel + i], blk_id)
        per_page = PAGE_ROWS // SLC_BLOCK
        in_block = (srow >> (SLC_BLOCK.bit_length() - 1)) == (blk_id & (per_page - 1))
        sdist = (past - (blk_id * SLC_BLOCK + (srow & (SLC_BLOCK - 1)))).astype(F32)
        kvn = kvn_ref[0]
        k_new = _rms_rows(kvn[:, 2 * HALF + k * HEAD_DIM:2 * HALF + (k + 1) * HEAD_DIM]) * gs_ref[...]
        v_new = kvn[:, 3 * HALF + k * HEAD_DIM:3 * HALF + (k + 1) * HEAD_DIM]
        s_sel = jnp.where(in_block, _mm(q, kt_all, True) - slopes * sdist, NEG)
        o_s = finish(s_sel, jnp.sum(q * k_new, axis=-1, keepdims=True),
                     lambda e: _dot_nt_exact(e, vt_all), v_new)

        win = win_ref[0]
        kw = _rms_rows(win[:, lanes]) * gw_ref[...]
        vw = win[:, HALF + k * HEAD_DIM:HALF + (k + 1) * HEAD_DIM]
        winn = winn_ref[0]
        kw_new = _rms_rows(winn[:, lanes]) * gw_ref[...]
        o_w = finish(_dot_nt_exact(q, kw) - slopes * wdist, jnp.sum(q * kw_new, axis=-1, keepdims=True),
                     lambda e: _mm(e, vw, True), winn[:, HALF + k * HEAD_DIM:HALF + (k + 1) * HEAD_DIM])

        gcol = [jnp.zeros((SUB, 1), F32) for _ in range(3)]
        for g in range(A_GROUP):
            h = k * A_GROUP + g
            for br in range(3):
                gcol[br] = jnp.where(row == g, gates[:, 3 * h + br:3 * h + br + 1], gcol[br])
        outs.append(gcol[0] * oc_ref[0][:, lanes] + gcol[1] * o_s + gcol[2] * o_w)
    o_ref[0] = jnp.concatenate(outs, axis=1)


def _nsa_sample_sel(cache_t, page_table, picks, cache_win, kv_new, win_new, qn, oc, small, gs, gw, past):
    db = qn.shape[0]
    n_pages = page_table.shape[1]
    nsel = SLC_TOPN - 1
    per_page = PAGE_ROWS // SLC_BLOCK

    def block_spec(k, i):
        def imap(b, pt, pk):
            bb = jnp.minimum(b, db - 1)
            j = jnp.clip(pk[(bb * A_KV_HEADS + k) * nsel + i], 0, n_pages * per_page - 1)
            return (pt[bb * n_pages + j // per_page], 1, 0)
        return pl.BlockSpec((1, 2 * HALF, PAGE_ROWS), imap)

    cst = lambda shape: pl.BlockSpec(shape, lambda b, pt, pk: (0,) * len(shape))
    per_b = lambda a: pl.BlockSpec((1,) + a.shape[1:], lambda b, pt, pk: (b,) + (0,) * (a.ndim - 1))
    grid_spec = pltpu.PrefetchScalarGridSpec(
        num_scalar_prefetch=2,
        grid=(db,),
        in_specs=([block_spec(k, i) for k in range(A_KV_HEADS) for i in range(nsel)]
                  + [per_b(cache_win), per_b(kv_new), per_b(win_new), per_b(qn), per_b(oc), per_b(small),
                     cst((1, HEAD_DIM)), cst((HEAD_DIM, 1)), cst((1, HEAD_DIM))]),
        out_specs=pl.BlockSpec((1, SUB, LANES), lambda b, pt, pk: (b, 0, 0)),
    )
    return pl.pallas_call(
        functools.partial(_nsa_s2_kernel, past=past),
        grid_spec=grid_spec,
        out_shape=jax.ShapeDtypeStruct((db, SUB, LANES), F32),
        compiler_params=_cparams("arbitrary"),
        name="nsa_sample_selected",
    )(page_table.reshape(-1), picks.reshape(-1), *([cache_t] * (A_KV_HEADS * nsel)),
      cache_win, kv_new, win_new, qn, oc, small, gs, gs.reshape(HEAD_DIM, 1), gw)


def _nsa_sample(q_raw, kv_new, win_new, small, cache_kv, page_table, cache_win, q_norm_g, k_norm_g, w_cmp, pe_cmp):
    db = q_raw.shape[0]
    n_phys = cache_kv.shape[0]
    past = page_table.shape[1] * PAGE_ROWS
    assert cache_kv.shape[1] == PAGE_ROWS and past % SLC_BLOCK == 0 and past >= WINDOW == cache_win.shape[1]
    eye_k = jnp.eye(A_KV_HEADS, dtype=F32)
    w = jnp.einsum('sjde,kl->sjkdle', w_cmp, eye_k).reshape(2, CMP_BLOCK, HALF, HALF)
    w = jnp.concatenate([w[:, :CMP_STRIDE], w[:, CMP_STRIDE:]], axis=-1)
    hi = w.astype(BF16)
    lo = (w - hi.astype(F32)).astype(BF16)
    wcat = jnp.concatenate([hi, hi, lo], axis=2)
    with jax.default_matmul_precision("highest"):
        bias = jnp.einsum('sjd,sjde->se', pe_cmp, w_cmp)
    bias = jnp.tile(bias[:, None, :], (1, 1, A_KV_HEADS))
    cache_t = cache_kv.reshape(n_phys, PAGE_ROWS, ROW_W).transpose(0, 2, 1)
    qn, oc, picks = _nsa_sample_cmp(q_raw, cache_t, page_table, q_norm_g.reshape(1, HEAD_DIM),
                                    k_norm_g[0].reshape(1, HEAD_DIM), wcat, bias, past)
    picks = picks[:, :A_KV_HEADS, :SLC_TOPN - 1]
    o = _nsa_sample_sel(cache_t, page_table, picks, cache_win.reshape(db, WINDOW, 2 * HALF),
                        kv_new.reshape(db, 1, ROW_W), win_new.reshape(db, 1, 2 * HALF), qn, oc,
                        small.reshape(db, 1, SMALL_W), k_norm_g[1].reshape(1, HEAD_DIM),
                        k_norm_g[2].reshape(1, HEAD_DIM), past)
    o = o[:, :A_GROUP].reshape(db, A_GROUP, A_KV_HEADS, HEAD_DIM).transpose(0, 2, 1, 3)
    return o.reshape(db, A_WIDTH)


def _delta_step_kernel(conv_ref, x_ref, small_ref, gate_ref, cw_ref, alog_ref, dtb_ref, on_ref, s_ref, o_ref, s_out_ref):
    acc = conv_ref[0][0:1, :] * cw_ref[0:1, :]
    for j in range(1, CONV_W - 1):
        acc = acc + conv_ref[0][j:j + 1, :] * cw_ref[j:j + 1, :]
    c = _silu(acc + x_ref[0] * cw_ref[CONV_W - 1:CONV_W, :])
    sm = small_ref[0]
    g_all = -jnp.exp(alog_ref[...]) * _softplus(sm + dtb_ref[...])
    beta_all = jax.nn.sigmoid(sm)
    nk = B_HEADS * B_DK

    def as_columns(row):
        return jnp.broadcast_to(row, (row.shape[1], row.shape[1])).T

    for h in range(B_HEADS):
        q = _l2_rows(c[:, h * B_DK:(h + 1) * B_DK]) * B_DK ** -0.5
        k = _l2_rows(c[:, nk + h * B_DK:nk + (h + 1) * B_DK])
        v = c[:, 2 * nk + h * B_DV:2 * nk + (h + 1) * B_DV]
        kc, qc = as_columns(k), as_columns(q)
        s = s_ref[0, h] * jnp.exp(g_all[:, SM_A + h:SM_A + h + 1])
        u = beta_all[:, SM_B + h:SM_B + h + 1] * (v - jnp.sum(kc * s, axis=0, keepdims=True))
        s = s + kc * u
        s_out_ref[0, h] = s
        o = jnp.sum(qc * s, axis=0, keepdims=True)
        o = _rms_rows(o) * on_ref[...] * _silu(gate_ref[0][:, h * B_DV:(h + 1) * B_DV])
        o_ref[0, :, h * B_DV:(h + 1) * B_DV] = o


def _delta_step(state_conv, qkv_new, small, gate_b, conv_w, a_log, dt_bias, o_norm_g, state):
    db = qkv_new.shape[0]
    per_b = lambda *s: pl.BlockSpec((1,) + s, lambda b: (b,) + (0,) * len(s))
    return pl.pallas_call(
        _delta_step_kernel,
        grid=(db,),
        in_specs=[per_b(CONV_W - 1, CONV_CH), per_b(1, CONV_CH), per_b(1, SMALL_W), per_b(1, B_WIDTH),
                  _const_spec((CONV_W, CONV_CH)), _const_spec((1, LANES)), _const_spec((1, LANES)),
                  _const_spec((1, B_DV)), per_b(B_HEADS, B_DK, B_DV)],
        out_specs=[per_b(1, B_WIDTH), per_b(B_HEADS, B_DK, B_DV)],
        out_shape=[jax.ShapeDtypeStruct((db, 1, B_WIDTH), F32),
                   jax.ShapeDtypeStruct((db, B_HEADS, B_DK, B_DV), F32)],
        compiler_params=_cparams("arbitrary"),
        name="delta_step",
    )(state_conv, qkv_new.reshape(db, 1, CONV_CH), small.reshape(db, 1, SMALL_W), gate_b.reshape(db, 1, B_WIDTH),
      conv_w, _lane_slab(a_log, SM_A), _lane_slab(dt_bias, SM_A), o_norm_g.reshape(1, B_DV), state)


def _rearranged_w_in(w_in):
    o = IN_OFFSETS
    q_a, kv_a, g_a, qkv_b, a_b, b_b, gate_b, merge = (
        w_in[:, :o[0]], w_in[:, o[0]:o[1]], w_in[:, o[1]:o[2]], w_in[:, o[2]:o[3]],
        w_in[:, o[3]:o[4]], w_in[:, o[4]:o[5]], w_in[:, o[5]:o[6]], w_in[:, o[6]:])
    small = jnp.concatenate([g_a, a_b, b_b], axis=1)
    small = jnp.pad(small, ((0, 0), (0, SMALL_W - small.shape[1])))
    return jnp.concatenate([q_a, kv_a, qkv_b, gate_b, merge, small], axis=1)


def kernel(x_prompt, x_sample, cache_nsa_kv, page_table, cache_win_kv, state_conv, state_delta,
           c_prompt, c_sample, norm1_g, norm2_g, w_ada, b_ada, w_in, q_norm_g, k_norm_g, w_cmp, pe_cmp,
           conv_w, a_log, dt_bias, o_norm_g, w_proj_a, w_proj_b, w_out, w_router, b_router,
           w_exp_gu, w_exp_down, w_sh_gu, w_sh_down):
    T = x_prompt.shape[1]
    DB = x_sample.shape[0]
    assert x_prompt.shape[0] == 1 and x_sample.shape[1] == 1 and w_in.shape[0] == 1

    w_main_f = _rearranged_w_in(w_in[0])
    w_main = w_main_f.astype(BF16)
    wa, wb, wout = w_proj_a[0].astype(BF16), w_proj_b[0].astype(BF16), w_out[0].astype(BF16)
    wr_f = jnp.pad(w_router[0], ((0, 0), (0, LANES - N_EXPERTS)))
    wr = wr_f.astype(BF16)
    br = jnp.pad(b_router[0], (0, LANES - N_EXPERTS)).reshape(1, LANES)
    wgu = w_exp_gu[0].astype(BF16)
    wd2 = w_exp_down[0].astype(BF16).reshape(N_EXPERTS // 2, 2 * D_EXPERT, D_MODEL)
    wsgu, wsd = w_sh_gu[0].astype(BF16), w_sh_down[0].astype(BF16)
    g1, g2 = norm1_g[0].reshape(1, -1), norm2_g[0].reshape(1, -1)

    c_all = jnp.concatenate([c_prompt, jnp.zeros((7, D_MODEL), F32), c_sample], axis=0)
    mod = _adaln(c_all, w_ada[0], b_ada[0])
    mp = [mod[0:1, i * D_MODEL:(i + 1) * D_MODEL] for i in range(6)]
    ms = [mod[8:8 + DB, i * D_MODEL:(i + 1) * D_MODEL] for i in range(6)]

    xp = x_prompt.reshape(T, D_MODEL)
    q_raw, kv, win, qkv_b, gate_b, msig, small = _in_proj(xp, g1, mp[1], mp[0], w_main, 256)
    o_a = _nsa_prompt(q_raw, kv, win, small, q_norm_g[0], k_norm_g[0], w_cmp[0], pe_cmp[0])
    o_b, s_fin_p = _delta_prompt(qkv_b, small, gate_b, conv_w[0], a_log[0], dt_bias[0], o_norm_g[0])
    x1, h2, scores = _merge(xp, o_a, o_b, msig, mp[2], g2, mp[4], mp[3], wa, wb, wout, wr, 256)
    y_prompt = _moe(h2, scores, br, x1, mp[5], wgu, wd2, wsgu, wsd, 1024).reshape(1, T, D_MODEL)
    kv_prompt = kv.reshape(1, 1, T, N_KV_SLOTS, A_KV_HEADS, HEAD_DIM)
    win_prompt = win[T - min(WINDOW, T):].reshape(1, 1, -1, 2, A_KV_HEADS, HEAD_DIM)
    conv_prompt = qkv_b[T - (CONV_W - 1):].reshape(1, 1, CONV_W - 1, CONV_CH)
    delta_prompt = s_fin_p.reshape(1, 1, B_HEADS, B_DK, B_DV)

    xs = x_sample.reshape(DB, D_MODEL)
    q_raw, kv, win, qkv_b, gate_b, msig, small = _in_proj(xs, g1, ms[1], ms[0], w_main_f, DB)
    o_a = _nsa_sample(q_raw, kv, win, small, cache_nsa_kv[0], page_table, cache_win_kv[0],
                      q_norm_g[0], k_norm_g[0], w_cmp[0], pe_cmp[0])
    o_b, s_fin_s = _delta_step(state_conv[0], qkv_b, small, gate_b, conv_w[0], a_log[0], dt_bias[0], o_norm_g[0],
                               state_delta[0])
    x1, h2, scores = _merge(xs, o_a, o_b.reshape(DB, B_WIDTH), msig, ms[2], g2, ms[4], ms[3],
                            w_proj_a[0], w_proj_b[0], w_out[0], wr_f, DB)
    y_sample = _moe(h2, scores, br, x1, ms[5], wgu, wd2, wsgu, wsd, DB).reshape(DB, 1, D_MODEL)
    kv_sample = kv.reshape(1, DB, 1, N_KV_SLOTS, A_KV_HEADS, HEAD_DIM)
    win_sample = jnp.concatenate([cache_win_kv[0][:, 1:], win.reshape(DB, 1, 2, A_KV_HEADS, HEAD_DIM)], axis=1)[None]
    conv_sample = jnp.concatenate([state_conv[0][:, 1:], qkv_b.reshape(DB, 1, CONV_CH)], axis=1)[None]

    return (y_prompt, y_sample, kv_prompt, win_prompt, conv_prompt, delta_prompt,
            kv_sample, win_sample, conv_sample, s_fin_s[None])
```

```python
import functools

import jax
import jax.numpy as jnp
import numpy as np
from jax import lax
from jax.experimental import pallas as pl
from jax.experimental.pallas import tpu as pltpu

F32 = jnp.float32
BF16 = jnp.bfloat16
HIGHEST = lax.Precision.HIGHEST

D_MODEL = 1024
A_HEADS = 8
A_KV_HEADS = 2
A_GROUP = A_HEADS // A_KV_HEADS
HEAD_DIM = 64
CMP_STRIDE = 16
CMP_BLOCK = 32
SLC_BLOCK = 64
RATIO = SLC_BLOCK // CMP_STRIDE
SLC_TOPN = 16
WINDOW = 512
Q_BLOCK = 128
N_KV_SLOTS = 4
B_HEADS = 4
B_DK = 128
B_DV = 128
CONV_W = 4
DELTA_CHUNK = 64
N_EXPERTS = 64
TOP_K = 6
D_EXPERT = 128
D_SHARED = 128
ROUTED_SCALE = 2.5
EPS = 1e-6
NEG = -1e30
BIG = 1e30

A_WIDTH = A_HEADS * HEAD_DIM
B_WIDTH = B_HEADS * B_DV
CONV_CH = 2 * B_HEADS * B_DK + B_HEADS * B_DV
KV_WIDTH = 6 * A_KV_HEADS * HEAD_DIM
IN_SPLITS = (A_WIDTH, KV_WIDTH, 3 * A_HEADS, CONV_CH, B_HEADS, B_HEADS, B_WIDTH, 2 * D_MODEL)
IN_OFFSETS = tuple(int(v) for v in np.cumsum(IN_SPLITS)[:-1])

LANES = 128
SMALL_W = LANES
C_Q = 0
C_KV = C_Q + A_WIDTH
C_QKVB = C_KV + KV_WIDTH
C_GATEB = C_QKVB + CONV_CH
C_MERGE = C_GATEB + B_WIDTH
C_SMALL = C_MERGE + 2 * D_MODEL
W_MAIN = C_SMALL + SMALL_W

VMEM_LIMIT = 56 * 1024 * 1024


def _cparams(*sem):
    return pltpu.CompilerParams(dimension_semantics=sem, vmem_limit_bytes=VMEM_LIMIT)


def _const_spec(shape, single=False):
    nd = len(shape)
    if single:
        return pl.BlockSpec(shape, lambda *_: (0,) * nd, pipeline_mode=pl.Buffered(1))
    return pl.BlockSpec(shape, lambda *_: (0,) * nd)


def _row_spec(tm, width, rows):
    if rows == 1:
        return pl.BlockSpec((1, width), lambda i: (0, 0))
    return pl.BlockSpec((tm, width), lambda i: (i, 0))


def _silu(x):
    return x * jax.nn.sigmoid(x)


def _rms_rows(x):
    return x * lax.rsqrt(jnp.mean(x * x, axis=-1, keepdims=True) + EPS)


def _mm(a, b, exact):
    if exact:
        return jnp.dot(a.astype(F32), b, preferred_element_type=F32, precision=HIGHEST)
    return jnp.dot(a.astype(BF16), b, preferred_element_type=F32)


def _split3(x):
    hi = x.astype(BF16)
    lo = (x - hi.astype(F32)).astype(BF16)
    return jnp.concatenate([hi, lo, hi], axis=1)


def _stack3(x):
    hi = x.astype(BF16)
    lo = (x - hi.astype(F32)).astype(BF16)
    return jnp.concatenate([hi, hi, lo], axis=0)


def _mm3(a, b3):
    return jnp.dot(_split3(a), b3, preferred_element_type=F32)


def _ada_kernel(c_ref, w_ref, b_ref, o_ref):
    o_ref[...] = _mm(_silu(c_ref[...]), w_ref[...], True) + b_ref[...]


def _adaln(c_all, w_ada, b_ada):
    rows = c_all.shape[0]
    tn = 1024
    return pl.pallas_call(
        _ada_kernel,
        grid=(6 * D_MODEL // tn,),
        in_specs=[pl.BlockSpec((rows, D_MODEL), lambda j: (0, 0)),
                  pl.BlockSpec((D_MODEL, tn), lambda j: (0, j)),
                  pl.BlockSpec((1, tn), lambda j: (0, j))],
        out_specs=pl.BlockSpec((rows, tn), lambda j: (0, j)),
        out_shape=jax.ShapeDtypeStruct((rows, 6 * D_MODEL), F32),
        compiler_params=_cparams("arbitrary"),
        name="adaln",
    )(c_all, w_ada, b_ada.reshape(1, -1))


def _in_kernel(x_ref, g_ref, sc_ref, sh_ref, w_ref,
               q_ref, kv_ref, win_ref, qkvb_ref, gateb_ref, merge_ref, small_ref, *, exact):
    h = _rms_rows(x_ref[...]) * g_ref[...]
    h = h * (1.0 + sc_ref[...]) + sh_ref[...]
    if not exact:
        h = h.astype(BF16)

    def proj(c0, width):
        return _mm(h, w_ref[:, c0:c0 + width], exact)

    q_ref[...] = proj(C_Q, A_WIDTH)
    kv_ref[...] = proj(C_KV, N_KV_SLOTS * A_KV_HEADS * HEAD_DIM)
    win_ref[...] = proj(C_KV + N_KV_SLOTS * A_KV_HEADS * HEAD_DIM, 2 * A_KV_HEADS * HEAD_DIM)
    qkvb_ref[...] = proj(C_QKVB, CONV_CH)
    gateb_ref[...] = proj(C_GATEB, B_WIDTH)
    merge_ref[...] = jax.nn.sigmoid(proj(C_MERGE, 2 * D_MODEL))
    small_ref[...] = proj(C_SMALL, SMALL_W)


def _in_proj(x, g1, sc, sh, w_main, tm):
    t = x.shape[0]
    widths = (A_WIDTH, 4 * A_KV_HEADS * HEAD_DIM, 2 * A_KV_HEADS * HEAD_DIM, CONV_CH, B_WIDTH,
              2 * D_MODEL, SMALL_W)
    return pl.pallas_call(
        functools.partial(_in_kernel, exact=w_main.dtype == F32),
        grid=(t // tm,),
        in_specs=[pl.BlockSpec((tm, D_MODEL), lambda i: (i, 0)),
                  _const_spec((1, D_MODEL)),
                  _row_spec(tm, D_MODEL, sc.shape[0]),
                  _row_spec(tm, D_MODEL, sh.shape[0]),
                  _const_spec((D_MODEL, W_MAIN), single=True)],
        out_specs=[pl.BlockSpec((tm, w), lambda i: (i, 0)) for w in widths],
        out_shape=[jax.ShapeDtypeStruct((t, w), F32) for w in widths],
        compiler_params=_cparams("arbitrary"),
        name="in_proj",
    )(x, g1, sc, sh, w_main)


def _merge_kernel(x_ref, oa_ref, ob_ref, msig_ref, gt1_ref, g2_ref, sc2_ref, sh2_ref,
                  wa_ref, wb_ref, wout_ref, wr_ref, x1_ref, h2_ref, score_ref, *, exact):
    pa = _mm(oa_ref[...], wa_ref[...], exact)
    pb = _mm(ob_ref[...], wb_ref[...], exact)
    m = msig_ref[:, :D_MODEL] * pa + msig_ref[:, D_MODEL:] * pb
    y = _mm(m, wout_ref[...], exact)
    x1 = x_ref[...] + gt1_ref[...] * y
    x1_ref[...] = x1
    h2 = _rms_rows(x1) * g2_ref[...]
    h2 = h2 * (1.0 + sc2_ref[...]) + sh2_ref[...]
    h2_ref[...] = h2.astype(BF16)
    score_ref[...] = jax.nn.sigmoid(_mm(h2, wr_ref[...], exact))


def _merge(x, o_a, o_b, msig, gt1, g2, sc2, sh2, wa, wb, wout, wr, tm):
    t = x.shape[0]
    return pl.pallas_call(
        functools.partial(_merge_kernel, exact=wout.dtype == F32),
        grid=(t // tm,),
        in_specs=[pl.BlockSpec((tm, D_MODEL), lambda i: (i, 0)),
                  pl.BlockSpec((tm, A_WIDTH), lambda i: (i, 0)),
                  pl.BlockSpec((tm, B_WIDTH), lambda i: (i, 0)),
                  pl.BlockSpec((tm, 2 * D_MODEL), lambda i: (i, 0)),
                  _row_spec(tm, D_MODEL, gt1.shape[0]),
                  _const_spec((1, D_MODEL)),
                  _row_spec(tm, D_MODEL, sc2.shape[0]),
                  _row_spec(tm, D_MODEL, sh2.shape[0]),
                  _const_spec((A_WIDTH, D_MODEL)),
                  _const_spec((B_WIDTH, D_MODEL)),
                  _const_spec((D_MODEL, D_MODEL)),
                  _const_spec((D_MODEL, LANES))],
        out_specs=[pl.BlockSpec((tm, D_MODEL), lambda i: (i, 0)),
                   pl.BlockSpec((tm, D_MODEL), lambda i: (i, 0)),
                   pl.BlockSpec((tm, LANES), lambda i: (i, 0))],
        out_shape=[jax.ShapeDtypeStruct((t, D_MODEL), F32),
                   jax.ShapeDtypeStruct((t, D_MODEL), BF16),
                   jax.ShapeDtypeStruct((t, LANES), F32)],
        compiler_params=_cparams("arbitrary"),
        name="merge_out",
    )(x, o_a, o_b, msig, gt1, g2, sc2, sh2, wa, wb, wout, wr)


def _route(scores, bias):
    lane = lax.broadcasted_iota(jnp.int32, scores.shape, 1)
    live = lane < N_EXPERTS
    v = jnp.where(live, scores + bias, -jnp.inf)
    sel = jnp.zeros(scores.shape, jnp.bool_)
    for _ in range(TOP_K):
        m = jnp.max(v, axis=-1, keepdims=True)
        first = jnp.min(jnp.where(v == m, lane, LANES), axis=-1, keepdims=True)
        hit = lane == first
        sel = jnp.logical_or(sel, hit)
        v = jnp.where(hit, -jnp.inf, v)
    picked = jnp.where(sel, scores, 0.0)
    return picked / jnp.sum(picked, axis=-1, keepdims=True) * ROUTED_SCALE


def _moe_kernel(h2_ref, score_ref, bias_ref, x1_ref, gt2_ref, wgu_ref, wd_ref, wsgu_ref, wsd_ref,
                out_ref, gate_ref, acc_ref):
    p = pl.program_id(1)
    h2 = h2_ref[...]

    @pl.when(p == 0)
    def _():
        gate = _route(score_ref[...], bias_ref[...])
        hi = gate.astype(BF16)
        lo = (gate - hi.astype(F32)).astype(BF16)
        gate_ref[...] = jnp.concatenate([hi, lo], axis=-1)
        s = jnp.dot(h2, wsgu_ref[...], preferred_element_type=F32)
        sact = _silu(s[:, :D_SHARED]) * s[:, D_SHARED:]
        acc_ref[...] = jnp.dot(sact.astype(BF16), wsd_ref[...], preferred_element_type=F32)

    row = lax.broadcasted_iota(jnp.int32, (2 * LANES, 2 * D_EXPERT), 0) % LANES
    col = lax.broadcasted_iota(jnp.int32, (2 * LANES, 2 * D_EXPERT), 1) // D_EXPERT
    onehot = jnp.where(row == 2 * p + col, 1.0, 0.0).astype(BF16)
    gsel = jnp.dot(gate_ref[...], onehot, preferred_element_type=F32)

    acts = []
    for e in range(2):
        au = jnp.dot(h2, wgu_ref[e], preferred_element_type=F32)
        acts.append(_silu(au[:, :D_EXPERT]) * au[:, D_EXPERT:])
    act = (jnp.concatenate(acts, axis=-1) * gsel).astype(BF16)
    acc_ref[...] += jnp.dot(act, wd_ref[0], preferred_element_type=F32)

    @pl.when(p == pl.num_programs(1) - 1)
    def _():
        out_ref[...] = x1_ref[...] + gt2_ref[...] * acc_ref[...]


def _moe(h2, scores, bias, x1, gt2, wgu, wd2, wsgu, wsd, tm):
    t = h2.shape[0]
    npairs = N_EXPERTS // 2
    return pl.pallas_call(
        _moe_kernel,
        grid=(t // tm, npairs),
        in_specs=[pl.BlockSpec((tm, D_MODEL), lambda i, p: (i, 0)),
                  pl.BlockSpec((tm, LANES), lambda i, p: (i, 0)),
                  pl.BlockSpec((1, LANES), lambda i, p: (0, 0)),
                  pl.BlockSpec((tm, D_MODEL), lambda i, p: (i, 0)),
                  (pl.BlockSpec((1, D_MODEL), lambda i, p: (0, 0)) if gt2.shape[0] == 1
                   else pl.BlockSpec((tm, D_MODEL), lambda i, p: (i, 0))),
                  pl.BlockSpec((2, D_MODEL, 2 * D_EXPERT), lambda i, p: (p, 0, 0)),
                  pl.BlockSpec((1, 2 * D_EXPERT, D_MODEL), lambda i, p: (p, 0, 0)),
                  pl.BlockSpec((D_MODEL, 2 * D_SHARED), lambda i, p: (0, 0)),
                  pl.BlockSpec((D_SHARED, D_MODEL), lambda i, p: (0, 0))],
        out_specs=pl.BlockSpec((tm, D_MODEL), lambda i, p: (i, 0)),
        out_shape=jax.ShapeDtypeStruct((t, D_MODEL), F32),
        scratch_shapes=[pltpu.VMEM((tm, 2 * LANES), BF16), pltpu.VMEM((tm, D_MODEL), F32)],
        compiler_params=_cparams("parallel", "arbitrary"),
        name="moe",
    )(h2, scores, bias, x1, gt2, wgu, wd2, wsgu, wsd)


SEL_TILE = 512
SEL_TILE_BLOCKS = SEL_TILE // SLC_BLOCK
WIN_TILES = WINDOW // Q_BLOCK + 1
CMP_ROWS = CMP_STRIDE * N_KV_SLOTS * A_KV_HEADS * HEAD_DIM
CMP_COLS = 2 * A_KV_HEADS * HEAD_DIM


def _head_slope(h):
    return float(2.0 ** (-8.0 * (h + 1) / A_HEADS))


def _compress_kernel(x_ref, w1_ref, w2_ref, pe1_ref, pe2_ref, gk_ref, kc_ref, vct_ref, f1_ref, f2_ref, vc_ref,
                     *, nb, tile):
    i = pl.program_id(0)
    nsb = nb // RATIO

    half = A_KV_HEADS * HEAD_DIM

    @pl.when(i == 0)
    def _():
        for s in range(2):
            f2_ref[s, pl.ds(nb, 8), :] = jnp.zeros((8, half), F32)

    x = x_ref[...].astype(BF16)
    row0 = pl.multiple_of(i * tile, tile)
    first = jnp.dot(x, w1_ref[...], preferred_element_type=F32)
    second = jnp.dot(x, w2_ref[...], preferred_element_type=F32)
    for s in range(2):
        f1_ref[s, pl.ds(row0, tile), :] = first[:, s * half:(s + 1) * half]
        f2_ref[s, pl.ds(row0, tile), :] = second[:, s * half:(s + 1) * half]

    @pl.when(i == pl.num_programs(0) - 1)
    def _():
        bias = (jnp.dot(pe1_ref[...], w1_ref[...], preferred_element_type=F32)
                + jnp.dot(pe2_ref[...], w2_ref[...], preferred_element_type=F32))[0:1]
        for r in range(RATIO):
            blk = [f1_ref[s, pl.ds(r, nsb, stride=RATIO), :] + f2_ref[s, pl.ds(r + 1, nsb, stride=RATIO), :]
                   + bias[:, s * half:(s + 1) * half] for s in range(2)]
            for k in range(A_KV_HEADS):
                kc = _rms_rows(blk[0][:, k * HEAD_DIM:(k + 1) * HEAD_DIM]) * gk_ref[...]
                kc_ref[k, pl.ds(r * nsb, nsb), :] = kc.astype(BF16)
            vc_ref[pl.ds(r * nsb, nsb), :] = blk[1]
        vct = vc_ref[...].T
        for k in range(A_KV_HEADS):
            vct_ref[k] = vct[k * HEAD_DIM:(k + 1) * HEAD_DIM].astype(BF16)


def _nsa_compress(kv_rows, w1, w2, pe1, pe2, gk):
    t = kv_rows.shape[0]
    nb = t // CMP_STRIDE
    tile = min(256, nb)
    x16 = kv_rows.reshape(nb, CMP_ROWS)
    return pl.pallas_call(
        functools.partial(_compress_kernel, nb=nb, tile=tile),
        grid=(nb // tile,),
        in_specs=[pl.BlockSpec((tile, CMP_ROWS), lambda i: (i, 0)),
                  _const_spec((CMP_ROWS, CMP_COLS)), _const_spec((CMP_ROWS, CMP_COLS)),
                  _const_spec((16, CMP_ROWS)), _const_spec((16, CMP_ROWS)), _const_spec((1, HEAD_DIM))],
        out_specs=[_const_spec((A_KV_HEADS, nb, HEAD_DIM)), _const_spec((A_KV_HEADS, HEAD_DIM, nb))],
        out_shape=[jax.ShapeDtypeStruct((A_KV_HEADS, nb, HEAD_DIM), BF16),
                   jax.ShapeDtypeStruct((A_KV_HEADS, HEAD_DIM, nb), BF16)],
        scratch_shapes=[pltpu.VMEM((2, nb + 8, CMP_COLS // 2), F32), pltpu.VMEM((2, nb + 8, CMP_COLS // 2), F32),
                        pltpu.VMEM((nb, A_KV_HEADS * HEAD_DIM), F32)],
        compiler_params=_cparams("arbitrary"),
        name="nsa_compress",
    )(x16, w1, w2, pe1, pe2, gk)


def _nsa_rows_kernel(kv_ref, win_ref, gs_ref, gw_ref, ks_ref, vst_ref, kw_ref, vwt_ref):
    half = A_KV_HEADS * HEAD_DIM
    kv = kv_ref[...]
    win = win_ref[...]
    vst = kv[:, 3 * half:4 * half].T
    vwt = win[:, half:2 * half].T
    for k in range(A_KV_HEADS):
        lo, hi = k * HEAD_DIM, (k + 1) * HEAD_DIM
        ks_ref[k, 0] = (_rms_rows(kv[:, 2 * half + lo:2 * half + hi]) * gs_ref[...]).astype(BF16)
        kw = (_rms_rows(win[:, lo:hi]) * gw_ref[...]).astype(BF16)
        for j in range(SEL_TILE // Q_BLOCK):
            kw_ref[k, j] = kw[j * Q_BLOCK:(j + 1) * Q_BLOCK]
            vwt_ref[k, j] = vwt[lo:hi, j * Q_BLOCK:(j + 1) * Q_BLOCK].astype(BF16)
        vst_ref[k, 0] = vst[lo:hi].astype(BF16)


def _nsa_rows(kv_rows, win_rows, gs, gw):
    t = kv_rows.shape[0]
    nt = t // SEL_TILE
    per = SEL_TILE // Q_BLOCK
    return pl.pallas_call(
        _nsa_rows_kernel,
        grid=(nt,),
        in_specs=[pl.BlockSpec((SEL_TILE, kv_rows.shape[1]), lambda i: (i, 0)),
                  pl.BlockSpec((SEL_TILE, win_rows.shape[1]), lambda i: (i, 0)),
                  _const_spec((1, HEAD_DIM)), _const_spec((1, HEAD_DIM))],
        out_specs=[pl.BlockSpec((A_KV_HEADS, 1, SEL_TILE, HEAD_DIM), lambda i: (0, i, 0, 0)),
                   pl.BlockSpec((A_KV_HEADS, 1, HEAD_DIM, SEL_TILE), lambda i: (0, i, 0, 0)),
                   pl.BlockSpec((A_KV_HEADS, per, Q_BLOCK, HEAD_DIM), lambda i: (0, i, 0, 0)),
                   pl.BlockSpec((A_KV_HEADS, per, HEAD_DIM, Q_BLOCK), lambda i: (0, i, 0, 0))],
        out_shape=[jax.ShapeDtypeStruct((A_KV_HEADS, nt, SEL_TILE, HEAD_DIM), BF16),
                   jax.ShapeDtypeStruct((A_KV_HEADS, nt, HEAD_DIM, SEL_TILE), BF16),
                   jax.ShapeDtypeStruct((A_KV_HEADS, nt * per, Q_BLOCK, HEAD_DIM), BF16),
                   jax.ShapeDtypeStruct((A_KV_HEADS, nt * per, HEAD_DIM, Q_BLOCK), BF16)],
        compiler_params=_cparams("arbitrary"),
        name="nsa_rows",
    )(kv_rows, win_rows, gs, gw)


def _nsa_cmp_kernel(q_ref, gq_ref, kc_ref, vct_ref, qt_ref, oct_ref, sel_ref, any_ref, *, nb):
    nsb = nb // RATIO
    shift = nsb.bit_length() - 1
    q0 = pl.program_id(0) * Q_BLOCK
    qt_full = q_ref[...].T
    crow = lax.broadcasted_iota(jnp.int32, (nb, Q_BLOCK), 0)
    cpos = (((crow & (nsb - 1)) * RATIO + (crow >> shift)) * CMP_STRIDE) + (CMP_BLOCK - 1)
    qpos = q0 + lax.broadcasted_iota(jnp.int32, (nb, Q_BLOCK), 1)
    dist = qpos - cpos
    valid = dist >= 0
    distf = dist.astype(F32)
    qpos_row = q0 + lax.broadcasted_iota(jnp.int32, (1, Q_BLOCK), 1)
    any_valid = jnp.where(qpos_row >= CMP_BLOCK - 1, 1.0, 0.0)
    jrow = lax.broadcasted_iota(jnp.int32, (nsb, Q_BLOCK), 0)
    cur = (q0 + lax.broadcasted_iota(jnp.int32, (nsb, Q_BLOCK), 1)) >> (SLC_BLOCK.bit_length() - 1)
    forced = jnp.logical_or(jrow == cur, jrow == 0)
    avail = jrow <= cur

    for k in range(A_KV_HEADS):
        slabs = []
        for g in range(A_GROUP):
            h = k * A_GROUP + g
            slab = qt_full[h * HEAD_DIM:(h + 1) * HEAD_DIM]
            inv = lax.rsqrt(jnp.mean(slab * slab, axis=0, keepdims=True) + EPS)
            slabs.append(slab * inv * gq_ref[...] * HEAD_DIM ** -0.5)
        qt = jnp.concatenate(slabs, axis=1).astype(BF16)
        qt_ref[0, k] = qt
        st = jnp.dot(kc_ref[k], qt, preferred_element_type=F32)
        pg = jnp.zeros((nb, Q_BLOCK), F32)
        ps = []
        for g in range(A_GROUP):
            s = st[:, g * Q_BLOCK:(g + 1) * Q_BLOCK]
            s = jnp.where(valid, s - _head_slope(k * A_GROUP + g) * distf, NEG)
            e = jnp.exp(s - jnp.max(s, axis=0, keepdims=True))
            p = e / jnp.sum(e, axis=0, keepdims=True) * any_valid
            pg = pg + p
            ps.append(p.astype(BF16))
        oct_ref[0, k] = jnp.dot(vct_ref[k], jnp.concatenate(ps, axis=1), preferred_element_type=F32)
        last = pg[3 * nsb:4 * nsb]
        prev = jnp.where(jrow == 0, 0.0, pltpu.roll(last, 1, axis=0))
        imp = pg[0:nsb] + pg[nsb:2 * nsb] + pg[2 * nsb:3 * nsb] + last + prev
        x = jnp.where(forced, BIG, jnp.where(avail, imp, NEG))
        picked = jnp.zeros((nsb, Q_BLOCK), jnp.bool_)
        for _ in range(SLC_TOPN):
            m = jnp.max(x, axis=0, keepdims=True)
            first = jnp.min(jnp.where(x == m, jrow, nsb), axis=0, keepdims=True)
            hit = jrow == first
            picked = jnp.logical_or(picked, hit)
            x = jnp.where(hit, -jnp.inf, x)
        sel = jnp.where(jnp.logical_and(picked, avail), 1.0, 0.0)
        sel_ref[0, k] = sel
        any_ref[0, k] = jnp.max(sel, axis=1, keepdims=True)


def _nsa_cmp(q_raw, gq_col, kc, vct):
    t = q_raw.shape[0]
    nqb = t // Q_BLOCK
    nb = kc.shape[1]
    nsb = nb // RATIO
    shp = (nqb, A_KV_HEADS, HEAD_DIM, A_GROUP * Q_BLOCK)
    blk = lambda *s: pl.BlockSpec((1,) + s, lambda i: (i,) + (0,) * len(s))
    return pl.pallas_call(
        functools.partial(_nsa_cmp_kernel, nb=nb),
        grid=(nqb,),
        in_specs=[pl.BlockSpec((Q_BLOCK, A_WIDTH), lambda i: (i, 0)),
                  _const_spec((HEAD_DIM, 1)),
                  _const_spec((A_KV_HEADS, nb, HEAD_DIM)),
                  _const_spec((A_KV_HEADS, HEAD_DIM, nb))],
        out_specs=[blk(*shp[1:]), blk(*shp[1:]), blk(A_KV_HEADS, nsb, Q_BLOCK), blk(A_KV_HEADS, nsb, 1)],
        out_shape=[jax.ShapeDtypeStruct(shp, BF16), jax.ShapeDtypeStruct(shp, F32),
                   jax.ShapeDtypeStruct((nqb, A_KV_HEADS, nsb, Q_BLOCK), F32),
                   jax.ShapeDtypeStruct((nqb, A_KV_HEADS, nsb, 1), F32)],
        compiler_params=_cparams("arbitrary"),
        name="nsa_compressed",
    )(q_raw, gq_col, kc, vct)


def _nsa_sel_kernel(tiles_ref, counts_ref, qt_ref, oct_ref, sel_ref, small_ref, ks_ref, vst_ref, kw_ref, vwt_ref,
                    o_ref, *, ntiles):
    qb = pl.program_id(0)
    q0 = qb * Q_BLOCK
    gates_t = jax.nn.sigmoid(small_ref[...]).T
    qpos = q0 + lax.broadcasted_iota(jnp.int32, (SEL_TILE, Q_BLOCK), 1)
    krow = lax.broadcasted_iota(jnp.int32, (SEL_TILE, Q_BLOCK), 0)
    wrow = lax.broadcasted_iota(jnp.int32, (WIN_TILES * Q_BLOCK, Q_BLOCK), 0)
    wdist = (q0 + lax.broadcasted_iota(jnp.int32, (WIN_TILES * Q_BLOCK, Q_BLOCK), 1)) - (q0 - WINDOW + wrow)
    wvalid = jnp.logical_and(jnp.logical_and(wdist >= 0, wdist <= WINDOW), q0 - WINDOW + wrow >= 0)
    wdistf = wdist.astype(F32)
    outs = []
    for k in range(A_KV_HEADS):
        qt = qt_ref[0, k]

        def tile_step(i, carry, k=k, qt=qt):
            m_prev, l_prev, acc = carry
            t = tiles_ref[(qb * A_KV_HEADS + k) * ntiles + i]
            st = jnp.dot(ks_ref[k, t], qt, preferred_element_type=F32)
            mrows = sel_ref[0, k, pl.ds(pl.multiple_of(t * SEL_TILE_BLOCKS, SEL_TILE_BLOCKS), SEL_TILE_BLOCKS), :]
            picked = jnp.concatenate(
                [jnp.broadcast_to(mrows[b:b + 1], (SLC_BLOCK, Q_BLOCK)) for b in range(SEL_TILE_BLOCKS)], axis=0)
            dist = qpos - (t * SEL_TILE + krow)
            valid = jnp.logical_and(picked > 0.5, dist >= 0)
            distf = dist.astype(F32)
            ms, ls, ps, alphas = [], [], [], []
            for g in range(A_GROUP):
                lanes = slice(g * Q_BLOCK, (g + 1) * Q_BLOCK)
                s = jnp.where(valid, st[:, lanes] - _head_slope(k * A_GROUP + g) * distf, NEG)
                m_new = jnp.maximum(m_prev[:, lanes], jnp.max(s, axis=0, keepdims=True))
                alpha = jnp.exp(m_prev[:, lanes] - m_new)
                p = jnp.where(valid, jnp.exp(s - m_new), 0.0)
                ms.append(m_new)
                ls.append(alpha * l_prev[:, lanes] + jnp.sum(p, axis=0, keepdims=True))
                alphas.append(alpha)
                ps.append(p.astype(BF16))
            pv = jnp.dot(vst_ref[k, t], jnp.concatenate(ps, axis=1), preferred_element_type=F32)
            return (jnp.concatenate(ms, axis=1), jnp.concatenate(ls, axis=1),
                    jnp.concatenate(alphas, axis=1) * acc + pv)

        init = (jnp.full((1, A_GROUP * Q_BLOCK), NEG, F32), jnp.zeros((1, A_GROUP * Q_BLOCK), F32),
                jnp.zeros((HEAD_DIM, A_GROUP * Q_BLOCK), F32))
        _, l_fin, acc = lax.fori_loop(0, counts_ref[qb * A_KV_HEADS + k], tile_step, init)
        os_t = acc / l_fin

        wk, wv = [], []
        for j in range(WIN_TILES):
            ti = jnp.maximum(qb - (WIN_TILES - 1) + j, 0)
            wk.append(kw_ref[k, ti])
            wv.append(vwt_ref[k, ti])
        sw = jnp.dot(jnp.concatenate(wk, axis=0), qt, preferred_element_type=F32)
        pw = []
        for g in range(A_GROUP):
            s = jnp.where(wvalid, sw[:, g * Q_BLOCK:(g + 1) * Q_BLOCK] - _head_slope(k * A_GROUP + g) * wdistf, NEG)
            e = jnp.exp(s - jnp.max(s, axis=0, keepdims=True))
            pw.append((e / jnp.sum(e, axis=0, keepdims=True)).astype(BF16))
        ow_t = jnp.dot(jnp.concatenate(wv, axis=1), jnp.concatenate(pw, axis=1), preferred_element_type=F32)

        oc_t = oct_ref[0, k]
        for g in range(A_GROUP):
            h = k * A_GROUP + g
            lanes = slice(g * Q_BLOCK, (g + 1) * Q_BLOCK)
            outs.append(gates_t[3 * h:3 * h + 1] * oc_t[:, lanes] + gates_t[3 * h + 1:3 * h + 2] * os_t[:, lanes]
                        + gates_t[3 * h + 2:3 * h + 3] * ow_t[:, lanes])
    o_ref[...] = jnp.concatenate(outs, axis=0).T


def _nsa_sel(tiles, counts, qt, oct, sel, small, ks, vst, kw, vwt):
    nqb = qt.shape[0]
    t = nqb * Q_BLOCK
    ntiles = ks.shape[1]
    nsb = sel.shape[2]
    shp = (A_KV_HEADS, HEAD_DIM, A_GROUP * Q_BLOCK)
    full = lambda a: pl.BlockSpec(a.shape, lambda i, *_: (0,) * a.ndim)
    grid_spec = pltpu.PrefetchScalarGridSpec(
        num_scalar_prefetch=2,
        grid=(nqb,),
        in_specs=[pl.BlockSpec((1,) + shp, lambda i, *_: (i, 0, 0, 0)),
                  pl.BlockSpec((1,) + shp, lambda i, *_: (i, 0, 0, 0)),
                  pl.BlockSpec((1, A_KV_HEADS, nsb, Q_BLOCK), lambda i, *_: (i, 0, 0, 0)),
                  pl.BlockSpec((Q_BLOCK, SMALL_W), lambda i, *_: (i, 0)),
                  full(ks), full(vst), full(kw), full(vwt)],
        out_specs=pl.BlockSpec((Q_BLOCK, A_WIDTH), lambda i, *_: (i, 0)),
    )
    return pl.pallas_call(
        functools.partial(_nsa_sel_kernel, ntiles=ntiles),
        grid_spec=grid_spec,
        out_shape=jax.ShapeDtypeStruct((t, A_WIDTH), F32),
        compiler_params=_cparams("arbitrary"),
        name="nsa_selected",
    )(tiles, counts, qt, oct, sel, small, ks, vst, kw, vwt)


def _cmp_weights(w_cmp, pe_cmp):
    eye_k = jnp.eye(A_KV_HEADS, dtype=F32)
    ws, pes = [], []
    for half in range(2):
        w = w_cmp[:, half * CMP_STRIDE:(half + 1) * CMP_STRIDE]
        full = jnp.einsum('sjde,ts,kl->jtkdsle', w, jnp.eye(N_KV_SLOTS, 2, dtype=F32), eye_k)
        ws.append(full.reshape(CMP_ROWS, CMP_COLS).astype(BF16))
        pe = pe_cmp[:, half * CMP_STRIDE:(half + 1) * CMP_STRIDE]
        pe = jnp.pad(pe, ((0, N_KV_SLOTS - 2), (0, 0), (0, 0)))
        pe = jnp.broadcast_to(pe.transpose(1, 0, 2)[:, :, None, :], (CMP_STRIDE, N_KV_SLOTS, A_KV_HEADS, HEAD_DIM))
        pes.append(jnp.broadcast_to(pe.reshape(1, CMP_ROWS), (16, CMP_ROWS)).astype(BF16))
    return ws[0], ws[1], pes[0], pes[1]


def _nsa_prompt(q_raw, kv_rows, win_rows, small, q_norm_g, k_norm_g, w_cmp, pe_cmp):
    t = q_raw.shape[0]
    w1, w2, pe1, pe2 = _cmp_weights(w_cmp, pe_cmp)
    kc, vct = _nsa_compress(kv_rows, w1, w2, pe1, pe2, k_norm_g[0].reshape(1, HEAD_DIM))
    ks, vst, kw, vwt = _nsa_rows(kv_rows, win_rows, k_norm_g[1].reshape(1, HEAD_DIM), k_norm_g[2].reshape(1, HEAD_DIM))
    qt, oct, sel, bany = _nsa_cmp(q_raw, q_norm_g.reshape(HEAD_DIM, 1), kc, vct)
    nqb, ntiles = t // Q_BLOCK, t // SEL_TILE
    flags = bany.reshape(nqb, A_KV_HEADS, ntiles, SEL_TILE_BLOCKS).max(-1) > 0.5
    order = jnp.argsort(jnp.logical_not(flags), axis=-1, stable=True).astype(jnp.int32)
    counts = flags.sum(-1).astype(jnp.int32)
    return _nsa_sel(order.reshape(-1), counts.reshape(-1), qt, oct, sel, small, ks, vst, kw, vwt)


DELTA_STEP = 2 * DELTA_CHUNK
CONV_HALO = 8
SM_A = 3 * A_HEADS
SM_B = SM_A + B_HEADS


def _softplus(x):
    return jnp.maximum(x, 0.0) + jnp.log(1.0 + jnp.exp(-jnp.abs(x)))


def _l2_rows(x):
    return x * lax.rsqrt(jnp.sum(x * x, axis=-1, keepdims=True) + EPS)


def _dot_nt3(a, b):
    hi = b.astype(BF16)
    lo = (b - hi.astype(F32)).astype(BF16)
    return lax.dot_general(_split3(a), jnp.concatenate([hi, hi, lo], axis=1), (((1,), (1,)), ((), ())),
                           preferred_element_type=F32)


def _delta_kernel(x_ref, small_ref, gate_ref, cw_ref, alog_ref, dtb_ref, on_ref, o_ref, s_out_ref, xbuf, s_ref):
    i = pl.program_id(0)
    n = DELTA_STEP

    @pl.when(i == 0)
    def _():
        xbuf[0:CONV_HALO, :] = jnp.zeros((CONV_HALO, CONV_CH), F32)
        s_ref[...] = jnp.zeros(s_ref.shape, F32)

    x = x_ref[...]
    xbuf[CONV_HALO:CONV_HALO + n, :] = x
    base = CONV_HALO - (CONV_W - 1)
    acc = xbuf[pl.ds(base, n), :] * cw_ref[0:1, :]
    for j in range(1, CONV_W):
        acc = acc + xbuf[pl.ds(base + j, n), :] * cw_ref[j:j + 1, :]
    c = _silu(acc)
    xbuf[0:CONV_HALO, :] = x[n - CONV_HALO:n, :]

    sm = small_ref[...]
    g_all = -jnp.exp(alog_ref[...]) * _softplus(sm + dtb_ref[...])
    beta_all = jax.nn.sigmoid(sm)
    r = lax.broadcasted_iota(jnp.int32, (n, n), 0)
    cc = lax.broadcasted_iota(jnp.int32, (n, n), 1)
    same = (r >= DELTA_CHUNK) == (cc >= DELTA_CHUNK)
    incl = jnp.logical_and(same, r >= cc)
    strict = jnp.logical_and(same, r > cc)
    gcum = jnp.dot(jnp.where(incl, 1.0, 0.0), g_all, preferred_element_type=F32, precision=HIGHEST)
    glast = jnp.dot(jnp.where(same, 1.0, 0.0), g_all, preferred_element_type=F32, precision=HIGHEST)
    gcum_t = gcum.T

    nk = B_HEADS * B_DK
    for h in range(B_HEADS):
        gc = gcum[:, SM_A + h:SM_A + h + 1]
        gl = glast[:, SM_A + h:SM_A + h + 1]
        decay = jnp.exp(jnp.where(incl, gc - gcum_t[SM_A + h:SM_A + h + 1, :], -jnp.inf))
        bcol = beta_all[:, SM_B + h:SM_B + h + 1]
        q = _l2_rows(c[:, h * B_DK:(h + 1) * B_DK]) * B_DK ** -0.5
        k = _l2_rows(c[:, nk + h * B_DK:nk + (h + 1) * B_DK])
        v = c[:, 2 * nk + h * B_DV:2 * nk + (h + 1) * B_DV]
        pw = -jnp.where(strict, _dot_nt3(k, k) * decay * bcol, 0.0)
        tm = pw
        for _ in range(DELTA_CHUNK.bit_length() - 2):
            pw = _mm3(pw, _stack3(pw))
            tm = tm + pw + _mm3(tm, _stack3(pw))
        eg = jnp.exp(gc)
        rhs_u = v * bcol
        rhs_w = k * bcol * eg
        solved = _mm3(tm, _stack3(jnp.concatenate([rhs_u, rhs_w], axis=1)))
        u = rhs_u + solved[:, :B_DV]
        w = rhs_w + solved[:, B_DV:]
        qk = _dot_nt3(q, k) * decay
        q_dec = q * eg
        kd_t = (k * jnp.exp(gl - gc)).T
        s = s_ref[h]
        outs = []
        for ci in range(2):
            rows = slice(ci * DELTA_CHUNK, (ci + 1) * DELTA_CHUNK)
            s3 = _stack3(s)
            v_new = u[rows] - _mm3(w[rows], s3)
            outs.append(_mm3(q_dec[rows], s3) + _mm(qk[rows, rows], v_new.astype(BF16), False))
            g_end = jnp.exp(glast[ci * DELTA_CHUNK:ci * DELTA_CHUNK + 1, SM_A + h:SM_A + h + 1])
            s = s * g_end + _mm3(kd_t[:, rows], _stack3(v_new))
        s_ref[h] = s
        o = jnp.concatenate(outs, axis=0)
        o = _rms_rows(o) * on_ref[...] * _silu(gate_ref[:, h * B_DV:(h + 1) * B_DV])
        o_ref[:, h * B_DV:(h + 1) * B_DV] = o

    @pl.when(i == pl.num_programs(0) - 1)
    def _():
        s_out_ref[...] = s_ref[...]


def _lane_slab(vals, lane0):
    return jnp.pad(vals, (lane0, LANES - lane0 - vals.shape[0])).reshape(1, LANES)


def _delta_prompt(qkv_b, small, gate_b, conv_w, a_log, dt_bias, o_norm_g):
    t = qkv_b.shape[0]
    n = DELTA_STEP
    return pl.pallas_call(
        _delta_kernel,
        grid=(t // n,),
        in_specs=[pl.BlockSpec((n, CONV_CH), lambda i: (i, 0)),
                  pl.BlockSpec((n, SMALL_W), lambda i: (i, 0)),
                  pl.BlockSpec((n, B_WIDTH), lambda i: (i, 0)),
                  _const_spec((CONV_W, CONV_CH)), _const_spec((1, LANES)), _const_spec((1, LANES)),
                  _const_spec((1, B_DV))],
        out_specs=[pl.BlockSpec((n, B_WIDTH), lambda i: (i, 0)), _const_spec((B_HEADS, B_DK, B_DV))],
        out_shape=[jax.ShapeDtypeStruct((t, B_WIDTH), F32), jax.ShapeDtypeStruct((B_HEADS, B_DK, B_DV), F32)],
        scratch_shapes=[pltpu.VMEM((CONV_HALO + n, CONV_CH), F32), pltpu.VMEM((B_HEADS, B_DK, B_DV), F32)],
        compiler_params=_cparams("arbitrary"),
        name="delta_prompt",
    )(qkv_b, small, gate_b, conv_w, _lane_slab(a_log, SM_A), _lane_slab(dt_bias, SM_A), o_norm_g.reshape(1, B_DV))


PAGE_ROWS = 128
PAGES_PER_STEP = 32
STRIPES_PER_PAGE = PAGE_ROWS // CMP_STRIDE
HALF = A_KV_HEADS * HEAD_DIM
ROW_W = N_KV_SLOTS * HALF
SUB = 8
N_PICK = SLC_TOPN - 2


def _dot_nt_exact(a, b):
    return lax.dot_general(a, b, (((1,), (1,)), ((), ())), preferred_element_type=F32, precision=HIGHEST)


def _row_slopes(k):
    row = lax.broadcasted_iota(jnp.int32, (SUB, 1), 0)
    s = jnp.zeros((SUB, 1), F32)
    for g in range(A_GROUP):
        s = jnp.where(row == g, _head_slope(k * A_GROUP + g), s)
    return s


def _head_rows(x, k, gq):
    rows = [x[:, (k * A_GROUP + g) * HEAD_DIM:(k * A_GROUP + g + 1) * HEAD_DIM] for g in range(A_GROUP)]
    q = jnp.concatenate(rows + [jnp.zeros((SUB - A_GROUP, HEAD_DIM), F32)], axis=0)
    return _rms_rows(q) * gq * HEAD_DIM ** -0.5


def _nsa_s1_kernel(pt_ref, *refs, nb, past):
    del pt_ref
    npg = PAGES_PER_STEP
    kpages, vpages = refs[:npg], refs[npg:2 * npg]
    q_ref, gq_ref, gk_ref, w_ref, bias_ref = refs[2 * npg:2 * npg + 5]
    qn_ref, oc_ref, idx_ref = refs[2 * npg + 5:2 * npg + 8]
    f1_ref, f2_ref, rows_ref = refs[2 * npg + 8:]
    c = pl.program_id(1)
    nsb = nb // RATIO
    rows_step = npg * STRIPES_PER_PAGE

    @pl.when(jnp.logical_and(pl.program_id(0) == 0, c == 0))
    def _():
        for s in range(2):
            f2_ref[s, pl.ds(nb, SUB), :] = jnp.zeros((SUB, HALF), F32)

    row0 = pl.multiple_of(c * rows_step, rows_step)
    for s, pages in enumerate((kpages, vpages)):
        for i, p in enumerate(pages):
            rows_ref[i] = p[0].T
        pieces = []
        for j in range(CMP_STRIDE):
            x = jnp.concatenate([rows_ref[i, pl.ds(j, STRIPES_PER_PAGE, stride=CMP_STRIDE), :] for i in range(npg)],
                                axis=0)
            pieces.append(_split3(x))
        acc = jnp.dot(jnp.concatenate(pieces, axis=1), w_ref[s], preferred_element_type=F32)
        f1_ref[s, pl.ds(row0, rows_step), :] = acc[:, :HALF]
        f2_ref[s, pl.ds(row0, rows_step), :] = acc[:, HALF:]

    @pl.when(c == pl.num_programs(1) - 1)
    def _():
        lane = lax.broadcasted_iota(jnp.int32, (SUB, nb), 1)
        shift = nsb.bit_length() - 1
        cpos = (((lane & (nsb - 1)) * RATIO + (lane >> shift)) * CMP_STRIDE) + (CMP_BLOCK - 1)
        dist = past - cpos
        valid = dist >= 0
        distf = dist.astype(F32)
        row = lax.broadcasted_iota(jnp.int32, (SUB, nb), 0)
        kc = [[], []]
        vc = []
        for r in range(RATIO):
            kb = f1_ref[0, pl.ds(r, nsb, stride=RATIO), :] + f2_ref[0, pl.ds(r + 1, nsb, stride=RATIO), :] + bias_ref[0]
            vc.append(f1_ref[1, pl.ds(r, nsb, stride=RATIO), :] + f2_ref[1, pl.ds(r + 1, nsb, stride=RATIO), :]
                      + bias_ref[1])
            for k in range(A_KV_HEADS):
                kc[k].append(_rms_rows(kb[:, k * HEAD_DIM:(k + 1) * HEAD_DIM]) * gk_ref[...])
        vc = jnp.concatenate(vc, axis=0)
        qn, oc, imps = [], [], []
        for k in range(A_KV_HEADS):
            q = _head_rows(q_ref[0], k, gq_ref[...])
            qn.append(q)
            s = _dot_nt_exact(q, jnp.concatenate(kc[k], axis=0))
            s = jnp.where(valid, s - _row_slopes(k) * distf, NEG)
            e = jnp.exp(s - jnp.max(s, axis=-1, keepdims=True))
            p = e / jnp.sum(e, axis=-1, keepdims=True)
            p = jnp.where(jnp.logical_and(row < A_GROUP, past >= CMP_BLOCK - 1), p, 0.0)
            oc.append(_mm(p, vc[:, k * HEAD_DIM:(k + 1) * HEAD_DIM], True))
            pg = jnp.broadcast_to(jnp.sum(p, axis=0, keepdims=True), (SUB, nb))
            last = pg[:, 3 * nsb:4 * nsb]
            lane_b = lax.broadcasted_iota(jnp.int32, (SUB, nsb), 1)
            prev = jnp.where(lane_b == 0, 0.0, pltpu.roll(last, 1, axis=1))
            imps.append(pg[:, 0:nsb] + pg[:, nsb:2 * nsb] + pg[:, 2 * nsb:3 * nsb] + last + prev)
        qn_ref[0] = jnp.concatenate(qn, axis=1)
        oc_ref[0] = jnp.concatenate(oc, axis=1)
        lane_s = lax.broadcasted_iota(jnp.int32, (SUB, nsb), 1)
        row_s = lax.broadcasted_iota(jnp.int32, (SUB, nsb), 0)
        x = jnp.zeros((SUB, nsb), F32)
        for k in range(A_KV_HEADS):
            x = jnp.where(row_s == k, imps[k], x)
        x = jnp.where(lane_s == 0, -jnp.inf, x)
        out_lane = lax.broadcasted_iota(jnp.int32, (SUB, LANES), 1)
        picks = jnp.zeros((SUB, LANES), jnp.int32)
        for i in range(N_PICK):
            m = jnp.max(x, axis=-1, keepdims=True)
            first = jnp.min(jnp.where(x == m, lane_s, nsb), axis=-1, keepdims=True)
            picks = jnp.where(out_lane == i + 1, first, picks)
            x = jnp.where(lane_s == first, -jnp.inf, x)
        idx_ref[0] = picks


def _nsa_sample_cmp(q_raw, cache_t, page_table, gq, gk, wcat, bias, past):
    db = q_raw.shape[0]
    n_pages = page_table.shape[1]
    nb = past // CMP_STRIDE
    nch = n_pages // PAGES_PER_STEP

    def page_spec(p, slot):
        return pl.BlockSpec((1, HALF, PAGE_ROWS),
                            lambda b, c, pt: (pt[jnp.minimum(b, db - 1) * n_pages + c * PAGES_PER_STEP + p], slot, 0))

    cst = lambda shape: pl.BlockSpec(shape, lambda b, c, pt: (0,) * len(shape))
    per_b = lambda: pl.BlockSpec((1, SUB, LANES), lambda b, c, pt: (b, 0, 0))
    grid_spec = pltpu.PrefetchScalarGridSpec(
        num_scalar_prefetch=1,
        grid=(db, nch),
        in_specs=([page_spec(p, 0) for p in range(PAGES_PER_STEP)] + [page_spec(p, 1) for p in range(PAGES_PER_STEP)]
                  + [pl.BlockSpec((1, 1, A_WIDTH), lambda b, c, pt: (b, 0, 0)),
                     cst((1, HEAD_DIM)), cst((1, HEAD_DIM)), cst(wcat.shape), cst(bias.shape)]),
        out_specs=[per_b(), per_b(), per_b()],
        scratch_shapes=[pltpu.VMEM((2, nb + SUB, HALF), F32), pltpu.VMEM((2, nb + SUB, HALF), F32),
                        pltpu.VMEM((PAGES_PER_STEP, PAGE_ROWS, HALF), F32)],
    )
    return pl.pallas_call(
        functools.partial(_nsa_s1_kernel, nb=nb, past=past),
        grid_spec=grid_spec,
        out_shape=[jax.ShapeDtypeStruct((db, SUB, LANES), F32), jax.ShapeDtypeStruct((db, SUB, LANES), F32),
                   jax.ShapeDtypeStruct((db, SUB, LANES), jnp.int32)],
        compiler_params=_cparams("arbitrary", "arbitrary"),
        name="nsa_sample_compressed",
    )(page_table.reshape(-1), *([cache_t] * (2 * PAGES_PER_STEP)), q_raw.reshape(db, 1, A_WIDTH), gq, gk, wcat, bias)


def _nsa_s2_kernel(pt_ref, pick_ref, *refs, past):
    del pt_ref
    nsel = SLC_TOPN - 1
    blocks = refs[:A_KV_HEADS * nsel]
    (win_ref, kvn_ref, winn_ref, qn_ref, oc_ref, small_ref, gs_ref, gsc_ref, gw_ref) = refs[A_KV_HEADS * nsel:-1]
    o_ref = refs[-1]
    b = pl.program_id(0)
    slane = lax.broadcasted_iota(jnp.int32, (1, nsel * PAGE_ROWS), 1)
    spage = slane >> (PAGE_ROWS.bit_length() - 1)
    srow = slane & (PAGE_ROWS - 1)
    gates = jax.nn.sigmoid(small_ref[0])
    row = lax.broadcasted_iota(jnp.int32, (SUB, 1), 0)
    wlane = lax.broadcasted_iota(jnp.int32, (1, WINDOW), 1)
    wdist = (WINDOW - wlane).astype(F32)
    outs = []
    for k in range(A_KV_HEADS):
        lanes = slice(k * HEAD_DIM, (k + 1) * HEAD_DIM)
        q = qn_ref[0][:, lanes]
        slopes = _row_slopes(k)

        def finish(s, s_new, pv, v_new):
            m = jnp.maximum(jnp.max(s, axis=-1, keepdims=True), s_new)
            e, e_new = jnp.exp(s - m), jnp.exp(s_new - m)
            den = jnp.sum(e, axis=-1, keepdims=True) + e_new
            return (pv(e) + e_new * v_new) / den

        kts, vts = [], []
        for i in range(nsel):
            page = blocks[k * nsel + i][0]
            kt = page[lanes]
            kts.append(kt * lax.rsqrt(jnp.mean(kt * kt, axis=0, keepdims=True) + EPS) * gsc_ref[...])
            vts.append(page[HALF + k * HEAD_DIM:HALF + (k + 1) * HEAD_DIM])
        kt_all, vt_all = jnp.concatenate(kts, axis=1), jnp.concatenate(vts, axis=1)
        blk_id = jnp.zeros((1, nsel * PAGE_ROWS), jnp.int32)
        for i in range(nsel):
            blk_id = jnp.where(spage == i, pick_ref[(b * A_KV_HEADS + k) * nsel + i], blk_id)
        per_page = PAGE_ROWS // SLC_BLOCK
        in_block = (srow >> (SLC_BLOCK.bit_length() - 1)) == (blk_id & (per_page - 1))
        sdist = (past - (blk_id * SLC_BLOCK + (srow & (SLC_BLOCK - 1)))).astype(F32)
        kvn = kvn_ref[0]
        k_new = _rms_rows(kvn[:, 2 * HALF + k * HEAD_DIM:2 * HALF + (k + 1) * HEAD_DIM]) * gs_ref[...]
        v_new = kvn[:, 3 * HALF + k * HEAD_DIM:3 * HALF + (k + 1) * HEAD_DIM]
        s_sel = jnp.where(in_block, _mm(q, kt_all, True) - slopes * sdist, NEG)
        o_s = finish(s_sel, jnp.sum(q * k_new, axis=-1, keepdims=True),
                     lambda e: _dot_nt_exact(e, vt_all), v_new)

        win = win_ref[0]
        kw = _rms_rows(win[:, lanes]) * gw_ref[...]
        vw = win[:, HALF + k * HEAD_DIM:HALF + (k + 1) * HEAD_DIM]
        winn = winn_ref[0]
        kw_new = _rms_rows(winn[:, lanes]) * gw_ref[...]
        o_w = finish(_dot_nt_exact(q, kw) - slopes * wdist, jnp.sum(q * kw_new, axis=-1, keepdims=True),
                     lambda e: _mm(e, vw, True), winn[:, HALF + k * HEAD_DIM:HALF + (k + 1) * HEAD_DIM])

        gcol = [jnp.zeros((SUB, 1), F32) for _ in range(3)]
        for g in range(A_GROUP):
            h = k * A_GROUP + g
            for br in range(3):
                gcol[br] = jnp.where(row == g, gates[:, 3 * h + br:3 * h + br + 1], gcol[br])
        outs.append(gcol[0] * oc_ref[0][:, lanes] + gcol[1] * o_s + gcol[2] * o_w)
    o_ref[0] = jnp.concatenate(outs, axis=1)


def _nsa_sample_sel(cache_t, page_table, picks, cache_win, kv_new, win_new, qn, oc, small, gs, gw, past):
    db = qn.shape[0]
    n_pages = page_table.shape[1]
    nsel = SLC_TOPN - 1
    per_page = PAGE_ROWS // SLC_BLOCK

    def block_spec(k, i):
        def imap(b, pt, pk):
            bb = jnp.minimum(b, db - 1)
            j = jnp.clip(pk[(bb * A_KV_HEADS + k) * nsel + i], 0, n_pages * per_page - 1)
            return (pt[bb * n_pages + j // per_page], 1, 0)
        return pl.BlockSpec((1, 2 * HALF, PAGE_ROWS), imap)

    cst = lambda shape: pl.BlockSpec(shape, lambda b, pt, pk: (0,) * len(shape))
    per_b = lambda a: pl.BlockSpec((1,) + a.shape[1:], lambda b, pt, pk: (b,) + (0,) * (a.ndim - 1))
    grid_spec = pltpu.PrefetchScalarGridSpec(
        num_scalar_prefetch=2,
        grid=(db,),
        in_specs=([block_spec(k, i) for k in range(A_KV_HEADS) for i in range(nsel)]
                  + [per_b(cache_win), per_b(kv_new), per_b(win_new), per_b(qn), per_b(oc), per_b(small),
                     cst((1, HEAD_DIM)), cst((HEAD_DIM, 1)), cst((1, HEAD_DIM))]),
        out_specs=pl.BlockSpec((1, SUB, LANES), lambda b, pt, pk: (b, 0, 0)),
    )
    return pl.pallas_call(
        functools.partial(_nsa_s2_kernel, past=past),
        grid_spec=grid_spec,
        out_shape=jax.ShapeDtypeStruct((db, SUB, LANES), F32),
        compiler_params=_cparams("arbitrary"),
        name="nsa_sample_selected",
    )(page_table.reshape(-1), picks.reshape(-1), *([cache_t] * (A_KV_HEADS * nsel)),
      cache_win, kv_new, win_new, qn, oc, small, gs, gs.reshape(HEAD_DIM, 1), gw)


def _nsa_sample(q_raw, kv_new, win_new, small, cache_kv, page_table, cache_win, q_norm_g, k_norm_g, w_cmp, pe_cmp):
    db = q_raw.shape[0]
    n_phys = cache_kv.shape[0]
    past = page_table.shape[1] * PAGE_ROWS
    assert cache_kv.shape[1] == PAGE_ROWS and past % SLC_BLOCK == 0 and past >= WINDOW == cache_win.shape[1]
    eye_k = jnp.eye(A_KV_HEADS, dtype=F32)
    w = jnp.einsum('sjde,kl->sjkdle', w_cmp, eye_k).reshape(2, CMP_BLOCK, HALF, HALF)
    w = jnp.concatenate([w[:, :CMP_STRIDE], w[:, CMP_STRIDE:]], axis=-1)
    hi = w.astype(BF16)
    lo = (w - hi.astype(F32)).astype(BF16)
    wcat = jnp.concatenate([hi, hi, lo], axis=2).reshape(2, CMP_STRIDE * 3 * HALF, 2 * HALF)
    with jax.default_matmul_precision("highest"):
        bias = jnp.einsum('sjd,sjde->se', pe_cmp, w_cmp)
    bias = jnp.tile(bias[:, None, :], (1, 1, A_KV_HEADS))
    cache_t = cache_kv.reshape(n_phys, PAGE_ROWS, ROW_W).transpose(0, 2, 1)
    qn, oc, picks = _nsa_sample_cmp(q_raw, cache_t, page_table, q_norm_g.reshape(1, HEAD_DIM),
                                    k_norm_g[0].reshape(1, HEAD_DIM), wcat, bias, past)
    picks = picks[:, :A_KV_HEADS, :SLC_TOPN - 1]
    o = _nsa_sample_sel(cache_t, page_table, picks, cache_win.reshape(db, WINDOW, 2 * HALF),
                        kv_new.reshape(db, 1, ROW_W), win_new.reshape(db, 1, 2 * HALF), qn, oc,
                        small.reshape(db, 1, SMALL_W), k_norm_g[1].reshape(1, HEAD_DIM),
                        k_norm_g[2].reshape(1, HEAD_DIM), past)
    o = o[:, :A_GROUP].reshape(db, A_GROUP, A_KV_HEADS, HEAD_DIM).transpose(0, 2, 1, 3)
    return o.reshape(db, A_WIDTH)


def _delta_step_kernel(conv_ref, x_ref, small_ref, gate_ref, cw_ref, alog_ref, dtb_ref, on_ref, s_ref, o_ref, s_out_ref):
    acc = conv_ref[0][0:1, :] * cw_ref[0:1, :]
    for j in range(1, CONV_W - 1):
        acc = acc + conv_ref[0][j:j + 1, :] * cw_ref[j:j + 1, :]
    c = _silu(acc + x_ref[0] * cw_ref[CONV_W - 1:CONV_W, :])
    sm = small_ref[0]
    g_all = -jnp.exp(alog_ref[...]) * _softplus(sm + dtb_ref[...])
    beta_all = jax.nn.sigmoid(sm)
    nk = B_HEADS * B_DK

    def as_columns(row):
        return jnp.broadcast_to(row, (row.shape[1], row.shape[1])).T

    for h in range(B_HEADS):
        q = _l2_rows(c[:, h * B_DK:(h + 1) * B_DK]) * B_DK ** -0.5
        k = _l2_rows(c[:, nk + h * B_DK:nk + (h + 1) * B_DK])
        v = c[:, 2 * nk + h * B_DV:2 * nk + (h + 1) * B_DV]
        kc, qc = as_columns(k), as_columns(q)
        s = s_ref[0, h] * jnp.exp(g_all[:, SM_A + h:SM_A + h + 1])
        u = beta_all[:, SM_B + h:SM_B + h + 1] * (v - jnp.sum(kc * s, axis=0, keepdims=True))
        s = s + kc * u
        s_out_ref[0, h] = s
        o = jnp.sum(qc * s, axis=0, keepdims=True)
        o = _rms_rows(o) * on_ref[...] * _silu(gate_ref[0][:, h * B_DV:(h + 1) * B_DV])
        o_ref[0, :, h * B_DV:(h + 1) * B_DV] = o


def _delta_step(state_conv, qkv_new, small, gate_b, conv_w, a_log, dt_bias, o_norm_g, state):
    db = qkv_new.shape[0]
    per_b = lambda *s: pl.BlockSpec((1,) + s, lambda b: (b,) + (0,) * len(s))
    return pl.pallas_call(
        _delta_step_kernel,
        grid=(db,),
        in_specs=[per_b(CONV_W - 1, CONV_CH), per_b(1, CONV_CH), per_b(1, SMALL_W), per_b(1, B_WIDTH),
                  _const_spec((CONV_W, CONV_CH)), _const_spec((1, LANES)), _const_spec((1, LANES)),
                  _const_spec((1, B_DV)), per_b(B_HEADS, B_DK, B_DV)],
        out_specs=[per_b(1, B_WIDTH), per_b(B_HEADS, B_DK, B_DV)],
        out_shape=[jax.ShapeDtypeStruct((db, 1, B_WIDTH), F32),
                   jax.ShapeDtypeStruct((db, B_HEADS, B_DK, B_DV), F32)],
        compiler_params=_cparams("arbitrary"),
        name="delta_step",
    )(state_conv, qkv_new.reshape(db, 1, CONV_CH), small.reshape(db, 1, SMALL_W), gate_b.reshape(db, 1, B_WIDTH),
      conv_w, _lane_slab(a_log, SM_A), _lane_slab(dt_bias, SM_A), o_norm_g.reshape(1, B_DV), state)


def _rearranged_w_in(w_in):
    o = IN_OFFSETS
    q_a, kv_a, g_a, qkv_b, a_b, b_b, gate_b, merge = (
        w_in[:, :o[0]], w_in[:, o[0]:o[1]], w_in[:, o[1]:o[2]], w_in[:, o[2]:o[3]],
        w_in[:, o[3]:o[4]], w_in[:, o[4]:o[5]], w_in[:, o[5]:o[6]], w_in[:, o[6]:])
    small = jnp.concatenate([g_a, a_b, b_b], axis=1)
    small = jnp.pad(small, ((0, 0), (0, SMALL_W - small.shape[1])))
    return jnp.concatenate([q_a, kv_a, qkv_b, gate_b, merge, small], axis=1)


def kernel(x_prompt, x_sample, cache_nsa_kv, page_table, cache_win_kv, state_conv, state_delta,
           c_prompt, c_sample, norm1_g, norm2_g, w_ada, b_ada, w_in, q_norm_g, k_norm_g, w_cmp, pe_cmp,
           conv_w, a_log, dt_bias, o_norm_g, w_proj_a, w_proj_b, w_out, w_router, b_router,
           w_exp_gu, w_exp_down, w_sh_gu, w_sh_down):
    T = x_prompt.shape[1]
    DB = x_sample.shape[0]
    assert x_prompt.shape[0] == 1 and x_sample.shape[1] == 1 and w_in.shape[0] == 1

    w_main_f = _rearranged_w_in(w_in[0])
    w_main = w_main_f.astype(BF16)
    wa, wb, wout = w_proj_a[0].astype(BF16), w_proj_b[0].astype(BF16), w_out[0].astype(BF16)
    wr_f = jnp.pad(w_router[0], ((0, 0), (0, LANES - N_EXPERTS)))
    wr = wr_f.astype(BF16)
    br = jnp.pad(b_router[0], (0, LANES - N_EXPERTS)).reshape(1, LANES)
    wgu = w_exp_gu[0].astype(BF16)
    wd2 = w_exp_down[0].astype(BF16).reshape(N_EXPERTS // 2, 2 * D_EXPERT, D_MODEL)
    wsgu, wsd = w_sh_gu[0].astype(BF16), w_sh_down[0].astype(BF16)
    g1, g2 = norm1_g[0].reshape(1, -1), norm2_g[0].reshape(1, -1)

    c_all = jnp.concatenate([c_prompt, jnp.zeros((7, D_MODEL), F32), c_sample], axis=0)
    mod = _adaln(c_all, w_ada[0], b_ada[0])
    mp = [mod[0:1, i * D_MODEL:(i + 1) * D_MODEL] for i in range(6)]
    ms = [mod[8:8 + DB, i * D_MODEL:(i + 1) * D_MODEL] for i in range(6)]

    xp = x_prompt.reshape(T, D_MODEL)
    q_raw, kv, win, qkv_b, gate_b, msig, small = _in_proj(xp, g1, mp[1], mp[0], w_main, 256)
    o_a = _nsa_prompt(q_raw, kv, win, small, q_norm_g[0], k_norm_g[0], w_cmp[0], pe_cmp[0])
    o_b, s_fin_p = _delta_prompt(qkv_b, small, gate_b, conv_w[0], a_log[0], dt_bias[0], o_norm_g[0])
    x1, h2, scores = _merge(xp, o_a, o_b, msig, mp[2], g2, mp[4], mp[3], wa, wb, wout, wr, 256)
    y_prompt = _moe(h2, scores, br, x1, mp[5], wgu, wd2, wsgu, wsd, 1024).reshape(1, T, D_MODEL)
    kv_prompt = kv.reshape(1, 1, T, N_KV_SLOTS, A_KV_HEADS, HEAD_DIM)
    win_prompt = win[T - min(WINDOW, T):].reshape(1, 1, -1, 2, A_KV_HEADS, HEAD_DIM)
    conv_prompt = qkv_b[T - (CONV_W - 1):].reshape(1, 1, CONV_W - 1, CONV_CH)
    delta_prompt = s_fin_p.reshape(1, 1, B_HEADS, B_DK, B_DV)

    xs = x_sample.reshape(DB, D_MODEL)
    q_raw, kv, win, qkv_b, gate_b, msig, small = _in_proj(xs, g1, ms[1], ms[0], w_main_f, DB)
    o_a = _nsa_sample(q_raw, kv, win, small, cache_nsa_kv[0], page_table, cache_win_kv[0],
                      q_norm_g[0], k_norm_g[0], w_cmp[0], pe_cmp[0])
    o_b, s_fin_s = _delta_step(state_conv[0], qkv_b, small, gate_b, conv_w[0], a_log[0], dt_bias[0], o_norm_g[0],
                               state_delta[0])
    x1, h2, scores = _merge(xs, o_a, o_b.reshape(DB, B_WIDTH), msig, ms[2], g2, ms[4], ms[3],
                            w_proj_a[0], w_proj_b[0], w_out[0], wr_f, DB)
    y_sample = _moe(h2, scores, br, x1, ms[5], wgu, wd2, wsgu, wsd, DB).reshape(DB, 1, D_MODEL)
    kv_sample = kv.reshape(1, DB, 1, N_KV_SLOTS, A_KV_HEADS, HEAD_DIM)
    win_sample = jnp.concatenate([cache_win_kv[0][:, 1:], win.reshape(DB, 1, 2, A_KV_HEADS, HEAD_DIM)], axis=1)[None]
    conv_sample = jnp.concatenate([state_conv[0][:, 1:], qkv_b.reshape(DB, 1, CONV_CH)], axis=1)[None]

    return (y_prompt, y_sample, kv_prompt, win_prompt, conv_prompt, delta_prompt,
            kv_sample, win_sample, conv_sample, s_fin_s[None])
```

```python
import functools

import jax
import jax.numpy as jnp
import numpy as np
from jax import lax
from jax.experimental import pallas as pl
from jax.experimental.pallas import tpu as pltpu

F32 = jnp.float32
BF16 = jnp.bfloat16
HIGHEST = lax.Precision.HIGHEST

D_MODEL = 1024
A_HEADS = 8
A_KV_HEADS = 2
A_GROUP = A_HEADS // A_KV_HEADS
HEAD_DIM = 64
CMP_STRIDE = 16
CMP_BLOCK = 32
SLC_BLOCK = 64
RATIO = SLC_BLOCK // CMP_STRIDE
SLC_TOPN = 16
WINDOW = 512
Q_BLOCK = 128
N_KV_SLOTS = 4
B_HEADS = 4
B_DK = 128
B_DV = 128
CONV_W = 4
DELTA_CHUNK = 64
N_EXPERTS = 64
TOP_K = 6
D_EXPERT = 128
D_SHARED = 128
ROUTED_SCALE = 2.5
EPS = 1e-6
NEG = -1e30
BIG = 1e30

A_WIDTH = A_HEADS * HEAD_DIM
B_WIDTH = B_HEADS * B_DV
CONV_CH = 2 * B_HEADS * B_DK + B_HEADS * B_DV
KV_WIDTH = 6 * A_KV_HEADS * HEAD_DIM
IN_SPLITS = (A_WIDTH, KV_WIDTH, 3 * A_HEADS, CONV_CH, B_HEADS, B_HEADS, B_WIDTH, 2 * D_MODEL)
IN_OFFSETS = tuple(int(v) for v in np.cumsum(IN_SPLITS)[:-1])

LANES = 128
SMALL_W = LANES
C_Q = 0
C_KV = C_Q + A_WIDTH
C_QKVB = C_KV + KV_WIDTH
C_GATEB = C_QKVB + CONV_CH
C_MERGE = C_GATEB + B_WIDTH
C_SMALL = C_MERGE + 2 * D_MODEL
W_MAIN = C_SMALL + SMALL_W

VMEM_LIMIT = 56 * 1024 * 1024


def _cparams(*sem):
    return pltpu.CompilerParams(dimension_semantics=sem, vmem_limit_bytes=VMEM_LIMIT)


def _const_spec(shape, single=False):
    nd = len(shape)
    if single:
        return pl.BlockSpec(shape, lambda *_: (0,) * nd, pipeline_mode=pl.Buffered(1))
    return pl.BlockSpec(shape, lambda *_: (0,) * nd)


def _row_spec(tm, width, rows):
    if rows == 1:
        return pl.BlockSpec((1, width), lambda i: (0, 0))
    return pl.BlockSpec((tm, width), lambda i: (i, 0))


def _silu(x):
    return x * jax.nn.sigmoid(x)


def _rms_rows(x):
    return x * lax.rsqrt(jnp.mean(x * x, axis=-1, keepdims=True) + EPS)


def _mm(a, b, exact):
    if exact:
        return jnp.dot(a.astype(F32), b, preferred_element_type=F32, precision=HIGHEST)
    return jnp.dot(a.astype(BF16), b, preferred_element_type=F32)


def _split3(x):
    hi = x.astype(BF16)
    lo = (x - hi.astype(F32)).astype(BF16)
    return jnp.concatenate([hi, lo, hi], axis=1)


def _stack3(x):
    hi = x.astype(BF16)
    lo = (x - hi.astype(F32)).astype(BF16)
    return jnp.concatenate([hi, hi, lo], axis=0)


def _mm3(a, b3):
    return jnp.dot(_split3(a), b3, preferred_element_type=F32)


def _ada_kernel(c_ref, w_ref, b_ref, o_ref):
    o_ref[...] = _mm(_silu(c_ref[...]), w_ref[...], True) + b_ref[...]


def _adaln(c_all, w_ada, b_ada):
    rows = c_all.shape[0]
    tn = 1024
    return pl.pallas_call(
        _ada_kernel,
        grid=(6 * D_MODEL // tn,),
        in_specs=[pl.BlockSpec((rows, D_MODEL), lambda j: (0, 0)),
                  pl.BlockSpec((D_MODEL, tn), lambda j: (0, j)),
                  pl.BlockSpec((1, tn), lambda j: (0, j))],
        out_specs=pl.BlockSpec((rows, tn), lambda j: (0, j)),
        out_shape=jax.ShapeDtypeStruct((rows, 6 * D_MODEL), F32),
        compiler_params=_cparams("arbitrary"),
        name="adaln",
    )(c_all, w_ada, b_ada.reshape(1, -1))


def _in_kernel(x_ref, g_ref, sc_ref, sh_ref, w_ref,
               q_ref, kv_ref, win_ref, qkvb_ref, gateb_ref, merge_ref, small_ref, *, exact):
    h = _rms_rows(x_ref[...]) * g_ref[...]
    h = h * (1.0 + sc_ref[...]) + sh_ref[...]
    if not exact:
        h = h.astype(BF16)

    def proj(c0, width):
        return _mm(h, w_ref[:, c0:c0 + width], exact)

    q_ref[...] = proj(C_Q, A_WIDTH)
    kv_ref[...] = proj(C_KV, N_KV_SLOTS * A_KV_HEADS * HEAD_DIM)
    win_ref[...] = proj(C_KV + N_KV_SLOTS * A_KV_HEADS * HEAD_DIM, 2 * A_KV_HEADS * HEAD_DIM)
    qkvb_ref[...] = proj(C_QKVB, CONV_CH)
    gateb_ref[...] = proj(C_GATEB, B_WIDTH)
    merge_ref[...] = jax.nn.sigmoid(proj(C_MERGE, 2 * D_MODEL))
    small_ref[...] = proj(C_SMALL, SMALL_W)


def _in_proj(x, g1, sc, sh, w_main, tm):
    t = x.shape[0]
    widths = (A_WIDTH, 4 * A_KV_HEADS * HEAD_DIM, 2 * A_KV_HEADS * HEAD_DIM, CONV_CH, B_WIDTH,
              2 * D_MODEL, SMALL_W)
    return pl.pallas_call(
        functools.partial(_in_kernel, exact=w_main.dtype == F32),
        grid=(t // tm,),
        in_specs=[pl.BlockSpec((tm, D_MODEL), lambda i: (i, 0)),
                  _const_spec((1, D_MODEL)),
                  _row_spec(tm, D_MODEL, sc.shape[0]),
                  _row_spec(tm, D_MODEL, sh.shape[0]),
                  _const_spec((D_MODEL, W_MAIN), single=True)],
        out_specs=[pl.BlockSpec((tm, w), lambda i: (i, 0)) for w in widths],
        out_shape=[jax.ShapeDtypeStruct((t, w), F32) for w in widths],
        compiler_params=_cparams("arbitrary"),
        name="in_proj",
    )(x, g1, sc, sh, w_main)


def _merge_kernel(x_ref, oa_ref, ob_ref, msig_ref, gt1_ref, g2_ref, sc2_ref, sh2_ref,
                  wa_ref, wb_ref, wout_ref, wr_ref, x1_ref, h2_ref, score_ref, *, exact):
    pa = _mm(oa_ref[...], wa_ref[...], exact)
    pb = _mm(ob_ref[...], wb_ref[...], exact)
    m = msig_ref[:, :D_MODEL] * pa + msig_ref[:, D_MODEL:] * pb
    y = _mm(m, wout_ref[...], exact)
    x1 = x_ref[...] + gt1_ref[...] * y
    x1_ref[...] = x1
    h2 = _rms_rows(x1) * g2_ref[...]
    h2 = h2 * (1.0 + sc2_ref[...]) + sh2_ref[...]
    h2_ref[...] = h2.astype(BF16)
    score_ref[...] = jax.nn.sigmoid(_mm(h2, wr_ref[...], exact))


def _merge(x, o_a, o_b, msig, gt1, g2, sc2, sh2, wa, wb, wout, wr, tm):
    t = x.shape[0]
    return pl.pallas_call(
        functools.partial(_merge_kernel, exact=wout.dtype == F32),
        grid=(t // tm,),
        in_specs=[pl.BlockSpec((tm, D_MODEL), lambda i: (i, 0)),
                  pl.BlockSpec((tm, A_WIDTH), lambda i: (i, 0)),
                  pl.BlockSpec((tm, B_WIDTH), lambda i: (i, 0)),
                  pl.BlockSpec((tm, 2 * D_MODEL), lambda i: (i, 0)),
                  _row_spec(tm, D_MODEL, gt1.shape[0]),
                  _const_spec((1, D_MODEL)),
                  _row_spec(tm, D_MODEL, sc2.shape[0]),
                  _row_spec(tm, D_MODEL, sh2.shape[0]),
                  _const_spec((A_WIDTH, D_MODEL)),
                  _const_spec((B_WIDTH, D_MODEL)),
                  _const_spec((D_MODEL, D_MODEL)),
                  _const_spec((D_MODEL, LANES))],
        out_specs=[pl.BlockSpec((tm, D_MODEL), lambda i: (i, 0)),
                   pl.BlockSpec((tm, D_MODEL), lambda i: (i, 0)),
                   pl.BlockSpec((tm, LANES), lambda i: (i, 0))],
        out_shape=[jax.ShapeDtypeStruct((t, D_MODEL), F32),
                   jax.ShapeDtypeStruct((t, D_MODEL), BF16),
                   jax.ShapeDtypeStruct((t, LANES), F32)],
        compiler_params=_cparams("arbitrary"),
        name="merge_out",
    )(x, o_a, o_b, msig, gt1, g2, sc2, sh2, wa, wb, wout, wr)


def _route(scores, bias):
    lane = lax.broadcasted_iota(jnp.int32, scores.shape, 1)
    live = lane < N_EXPERTS
    v = jnp.where(live, scores + bias, -jnp.inf)
    sel = jnp.zeros(scores.shape, jnp.bool_)
    for _ in range(TOP_K):
        m = jnp.max(v, axis=-1, keepdims=True)
        first = jnp.min(jnp.where(v == m, lane, LANES), axis=-1, keepdims=True)
        hit = lane == first
        sel = jnp.logical_or(sel, hit)
        v = jnp.where(hit, -jnp.inf, v)
    picked = jnp.where(sel, scores, 0.0)
    return picked / jnp.sum(picked, axis=-1, keepdims=True) * ROUTED_SCALE


def _moe_kernel(h2_ref, score_ref, bias_ref, x1_ref, gt2_ref, wgu_ref, wd_ref, wsgu_ref, wsd_ref,
                out_ref, gate_ref, acc_ref):
    p = pl.program_id(1)
    h2 = h2_ref[...]

    @pl.when(p == 0)
    def _():
        gate = _route(score_ref[...], bias_ref[...])
        hi = gate.astype(BF16)
        lo = (gate - hi.astype(F32)).astype(BF16)
        gate_ref[...] = jnp.concatenate([hi, lo], axis=-1)
        s = jnp.dot(h2, wsgu_ref[...], preferred_element_type=F32)
        sact = _silu(s[:, :D_SHARED]) * s[:, D_SHARED:]
        acc_ref[...] = jnp.dot(sact.astype(BF16), wsd_ref[...], preferred_element_type=F32)

    row = lax.broadcasted_iota(jnp.int32, (2 * LANES, 2 * D_EXPERT), 0) % LANES
    col = lax.broadcasted_iota(jnp.int32, (2 * LANES, 2 * D_EXPERT), 1) // D_EXPERT
    onehot = jnp.where(row == 2 * p + col, 1.0, 0.0).astype(BF16)
    gsel = jnp.dot(gate_ref[...], onehot, preferred_element_type=F32)

    acts = []
    for e in range(2):
        au = jnp.dot(h2, wgu_ref[e], preferred_element_type=F32)
        acts.append(_silu(au[:, :D_EXPERT]) * au[:, D_EXPERT:])
    act = (jnp.concatenate(acts, axis=-1) * gsel).astype(BF16)
    acc_ref[...] += jnp.dot(act, wd_ref[0], preferred_element_type=F32)

    @pl.when(p == pl.num_programs(1) - 1)
    def _():
        out_ref[...] = x1_ref[...] + gt2_ref[...] * acc_ref[...]


def _moe(h2, scores, bias, x1, gt2, wgu, wd2, wsgu, wsd, tm):
    t = h2.shape[0]
    npairs = N_EXPERTS // 2
    return pl.pallas_call(
        _moe_kernel,
        grid=(t // tm, npairs),
        in_specs=[pl.BlockSpec((tm, D_MODEL), lambda i, p: (i, 0)),
                  pl.BlockSpec((tm, LANES), lambda i, p: (i, 0)),
                  pl.BlockSpec((1, LANES), lambda i, p: (0, 0)),
                  pl.BlockSpec((tm, D_MODEL), lambda i, p: (i, 0)),
                  (pl.BlockSpec((1, D_MODEL), lambda i, p: (0, 0)) if gt2.shape[0] == 1
                   else pl.BlockSpec((tm, D_MODEL), lambda i, p: (i, 0))),
                  pl.BlockSpec((2, D_MODEL, 2 * D_EXPERT), lambda i, p: (p, 0, 0)),
                  pl.BlockSpec((1, 2 * D_EXPERT, D_MODEL), lambda i, p: (p, 0, 0)),
                  pl.BlockSpec((D_MODEL, 2 * D_SHARED), lambda i, p: (0, 0)),
                  pl.BlockSpec((D_SHARED, D_MODEL), lambda i, p: (0, 0))],
        out_specs=pl.BlockSpec((tm, D_MODEL), lambda i, p: (i, 0)),
        out_shape=jax.ShapeDtypeStruct((t, D_MODEL), F32),
        scratch_shapes=[pltpu.VMEM((tm, 2 * LANES), BF16), pltpu.VMEM((tm, D_MODEL), F32)],
        compiler_params=_cparams("parallel", "arbitrary"),
        name="moe",
    )(h2, scores, bias, x1, gt2, wgu, wd2, wsgu, wsd)


SEL_TILE = 512
SEL_TILE_BLOCKS = SEL_TILE // SLC_BLOCK
WIN_TILES = WINDOW // Q_BLOCK + 1
CMP_ROWS = CMP_STRIDE * N_KV_SLOTS * A_KV_HEADS * HEAD_DIM
CMP_COLS = 2 * A_KV_HEADS * HEAD_DIM


def _head_slope(h):
    return float(2.0 ** (-8.0 * (h + 1) / A_HEADS))


def _compress_kernel(x_ref, w1_ref, w2_ref, pe1_ref, pe2_ref, gk_ref, kc_ref, vct_ref, f1_ref, f2_ref, vc_ref,
                     *, nb, tile):
    i = pl.program_id(0)
    nsb = nb // RATIO

    half = A_KV_HEADS * HEAD_DIM

    @pl.when(i == 0)
    def _():
        for s in range(2):
            f2_ref[s, pl.ds(nb, 8), :] = jnp.zeros((8, half), F32)

    x = x_ref[...].astype(BF16)
    row0 = pl.multiple_of(i * tile, tile)
    first = jnp.dot(x, w1_ref[...], preferred_element_type=F32)
    second = jnp.dot(x, w2_ref[...], preferred_element_type=F32)
    for s in range(2):
        f1_ref[s, pl.ds(row0, tile), :] = first[:, s * half:(s + 1) * half]
        f2_ref[s, pl.ds(row0, tile), :] = second[:, s * half:(s + 1) * half]

    @pl.when(i == pl.num_programs(0) - 1)
    def _():
        bias = (jnp.dot(pe1_ref[...], w1_ref[...], preferred_element_type=F32)
                + jnp.dot(pe2_ref[...], w2_ref[...], preferred_element_type=F32))[0:1]
        for r in range(RATIO):
            blk = [f1_ref[s, pl.ds(r, nsb, stride=RATIO), :] + f2_ref[s, pl.ds(r + 1, nsb, stride=RATIO), :]
                   + bias[:, s * half:(s + 1) * half] for s in range(2)]
            for k in range(A_KV_HEADS):
                kc = _rms_rows(blk[0][:, k * HEAD_DIM:(k + 1) * HEAD_DIM]) * gk_ref[...]
                kc_ref[k, pl.ds(r * nsb, nsb), :] = kc.astype(BF16)
            vc_ref[pl.ds(r * nsb, nsb), :] = blk[1]
        vct = vc_ref[...].T
        for k in range(A_KV_HEADS):
            vct_ref[k] = vct[k * HEAD_DIM:(k + 1) * HEAD_DIM].astype(BF16)


def _nsa_compress(kv_rows, w1, w2, pe1, pe2, gk):
    t = kv_rows.shape[0]
    nb = t // CMP_STRIDE
    tile = min(256, nb)
    x16 = kv_rows.reshape(nb, CMP_ROWS)
    return pl.pallas_call(
        functools.partial(_compress_kernel, nb=nb, tile=tile),
        grid=(nb // tile,),
        in_specs=[pl.BlockSpec((tile, CMP_ROWS), lambda i: (i, 0)),
                  _const_spec((CMP_ROWS, CMP_COLS)), _const_spec((CMP_ROWS, CMP_COLS)),
                  _const_spec((16, CMP_ROWS)), _const_spec((16, CMP_ROWS)), _const_spec((1, HEAD_DIM))],
        out_specs=[_const_spec((A_KV_HEADS, nb, HEAD_DIM)), _const_spec((A_KV_HEADS, HEAD_DIM, nb))],
        out_shape=[jax.ShapeDtypeStruct((A_KV_HEADS, nb, HEAD_DIM), BF16),
                   jax.ShapeDtypeStruct((A_KV_HEADS, HEAD_DIM, nb), BF16)],
        scratch_shapes=[pltpu.VMEM((2, nb + 8, CMP_COLS // 2), F32), pltpu.VMEM((2, nb + 8, CMP_COLS // 2), F32),
                        pltpu.VMEM((nb, A_KV_HEADS * HEAD_DIM), F32)],
        compiler_params=_cparams("arbitrary"),
        name="nsa_compress",
    )(x16, w1, w2, pe1, pe2, gk)


def _nsa_rows_kernel(kv_ref, win_ref, gs_ref, gw_ref, ks_ref, vst_ref, kw_ref, vwt_ref):
    half = A_KV_HEADS * HEAD_DIM
    kv = kv_ref[...]
    win = win_ref[...]
    vst = kv[:, 3 * half:4 * half].T
    vwt = win[:, half:2 * half].T
    for k in range(A_KV_HEADS):
        lo, hi = k * HEAD_DIM, (k + 1) * HEAD_DIM
        ks_ref[k, 0] = (_rms_rows(kv[:, 2 * half + lo:2 * half + hi]) * gs_ref[...]).astype(BF16)
        kw = (_rms_rows(win[:, lo:hi]) * gw_ref[...]).astype(BF16)
        for j in range(SEL_TILE // Q_BLOCK):
            kw_ref[k, j] = kw[j * Q_BLOCK:(j + 1) * Q_BLOCK]
            vwt_ref[k, j] = vwt[lo:hi, j * Q_BLOCK:(j + 1) * Q_BLOCK].astype(BF16)
        vst_ref[k, 0] = vst[lo:hi].astype(BF16)


def _nsa_rows(kv_rows, win_rows, gs, gw):
    t = kv_rows.shape[0]
    nt = t // SEL_TILE
    per = SEL_TILE // Q_BLOCK
    return pl.pallas_call(
        _nsa_rows_kernel,
        grid=(nt,),
        in_specs=[pl.BlockSpec((SEL_TILE, kv_rows.shape[1]), lambda i: (i, 0)),
                  pl.BlockSpec((SEL_TILE, win_rows.shape[1]), lambda i: (i, 0)),
                  _const_spec((1, HEAD_DIM)), _const_spec((1, HEAD_DIM))],
        out_specs=[pl.BlockSpec((A_KV_HEADS, 1, SEL_TILE, HEAD_DIM), lambda i: (0, i, 0, 0)),
                   pl.BlockSpec((A_KV_HEADS, 1, HEAD_DIM, SEL_TILE), lambda i: (0, i, 0, 0)),
                   pl.BlockSpec((A_KV_HEADS, per, Q_BLOCK, HEAD_DIM), lambda i: (0, i, 0, 0)),
                   pl.BlockSpec((A_KV_HEADS, per, HEAD_DIM, Q_BLOCK), lambda i: (0, i, 0, 0))],
        out_shape=[jax.ShapeDtypeStruct((A_KV_HEADS, nt, SEL_TILE, HEAD_DIM), BF16),
                   jax.ShapeDtypeStruct((A_KV_HEADS, nt, HEAD_DIM, SEL_TILE), BF16),
                   jax.ShapeDtypeStruct((A_KV_HEADS, nt * per, Q_BLOCK, HEAD_DIM), BF16),
                   jax.ShapeDtypeStruct((A_KV_HEADS, nt * per, HEAD_DIM, Q_BLOCK), BF16)],
        compiler_params=_cparams("arbitrary"),
        name="nsa_rows",
    )(kv_rows, win_rows, gs, gw)


def _nsa_cmp_kernel(q_ref, gq_ref, kc_ref, vct_ref, qt_ref, oct_ref, sel_ref, any_ref, *, nb):
    nsb = nb // RATIO
    shift = nsb.bit_length() - 1
    q0 = pl.program_id(0) * Q_BLOCK
    qt_full = q_ref[...].T
    crow = lax.broadcasted_iota(jnp.int32, (nb, Q_BLOCK), 0)
    cpos = (((crow & (nsb - 1)) * RATIO + (crow >> shift)) * CMP_STRIDE) + (CMP_BLOCK - 1)
    qpos = q0 + lax.broadcasted_iota(jnp.int32, (nb, Q_BLOCK), 1)
    dist = qpos - cpos
    valid = dist >= 0
    distf = dist.astype(F32)
    qpos_row = q0 + lax.broadcasted_iota(jnp.int32, (1, Q_BLOCK), 1)
    any_valid = jnp.where(qpos_row >= CMP_BLOCK - 1, 1.0, 0.0)
    jrow = lax.broadcasted_iota(jnp.int32, (nsb, Q_BLOCK), 0)
    cur = (q0 + lax.broadcasted_iota(jnp.int32, (nsb, Q_BLOCK), 1)) >> (SLC_BLOCK.bit_length() - 1)
    forced = jnp.logical_or(jrow == cur, jrow == 0)
    avail = jrow <= cur

    for k in range(A_KV_HEADS):
        slabs = []
        for g in range(A_GROUP):
            h = k * A_GROUP + g
            slab = qt_full[h * HEAD_DIM:(h + 1) * HEAD_DIM]
            inv = lax.rsqrt(jnp.mean(slab * slab, axis=0, keepdims=True) + EPS)
            slabs.append(slab * inv * gq_ref[...] * HEAD_DIM ** -0.5)
        qt = jnp.concatenate(slabs, axis=1).astype(BF16)
        qt_ref[0, k] = qt
        st = jnp.dot(kc_ref[k], qt, preferred_element_type=F32)
        pg = jnp.zeros((nb, Q_BLOCK), F32)
        ps = []
        for g in range(A_GROUP):
            s = st[:, g * Q_BLOCK:(g + 1) * Q_BLOCK]
            s = jnp.where(valid, s - _head_slope(k * A_GROUP + g) * distf, NEG)
            e = jnp.exp(s - jnp.max(s, axis=0, keepdims=True))
            p = e / jnp.sum(e, axis=0, keepdims=True) * any_valid
            pg = pg + p
            ps.append(p.astype(BF16))
        oct_ref[0, k] = jnp.dot(vct_ref[k], jnp.concatenate(ps, axis=1), preferred_element_type=F32)
        last = pg[3 * nsb:4 * nsb]
        prev = jnp.where(jrow == 0, 0.0, pltpu.roll(last, 1, axis=0))
        imp = pg[0:nsb] + pg[nsb:2 * nsb] + pg[2 * nsb:3 * nsb] + last + prev
        x = jnp.where(forced, BIG, jnp.where(avail, imp, NEG))
        picked = jnp.zeros((nsb, Q_BLOCK), jnp.bool_)
        for _ in range(SLC_TOPN):
            m = jnp.max(x, axis=0, keepdims=True)
            first = jnp.min(jnp.where(x == m, jrow, nsb), axis=0, keepdims=True)
            hit = jrow == first
            picked = jnp.logical_or(picked, hit)
            x = jnp.where(hit, -jnp.inf, x)
        sel = jnp.where(jnp.logical_and(picked, avail), 1.0, 0.0)
        sel_ref[0, k] = sel
        any_ref[0, k] = jnp.max(sel, axis=1, keepdims=True)


def _nsa_cmp(q_raw, gq_col, kc, vct):
    t = q_raw.shape[0]
    nqb = t // Q_BLOCK
    nb = kc.shape[1]
    nsb = nb // RATIO
    shp = (nqb, A_KV_HEADS, HEAD_DIM, A_GROUP * Q_BLOCK)
    blk = lambda *s: pl.BlockSpec((1,) + s, lambda i: (i,) + (0,) * len(s))
    return pl.pallas_call(
        functools.partial(_nsa_cmp_kernel, nb=nb),
        grid=(nqb,),
        in_specs=[pl.BlockSpec((Q_BLOCK, A_WIDTH), lambda i: (i, 0)),
                  _const_spec((HEAD_DIM, 1)),
                  _const_spec((A_KV_HEADS, nb, HEAD_DIM)),
                  _const_spec((A_KV_HEADS, HEAD_DIM, nb))],
        out_specs=[blk(*shp[1:]), blk(*shp[1:]), blk(A_KV_HEADS, nsb, Q_BLOCK), blk(A_KV_HEADS, nsb, 1)],
        out_shape=[jax.ShapeDtypeStruct(shp, BF16), jax.ShapeDtypeStruct(shp, F32),
                   jax.ShapeDtypeStruct((nqb, A_KV_HEADS, nsb, Q_BLOCK), F32),
                   jax.ShapeDtypeStruct((nqb, A_KV_HEADS, nsb, 1), F32)],
        compiler_params=_cparams("arbitrary"),
        name="nsa_compressed",
    )(q_raw, gq_col, kc, vct)


def _nsa_sel_kernel(tiles_ref, counts_ref, qt_ref, oct_ref, sel_ref, small_ref, ks_ref, vst_ref, kw_ref, vwt_ref,
                    o_ref, *, ntiles):
    qb = pl.program_id(0)
    q0 = qb * Q_BLOCK
    gates_t = jax.nn.sigmoid(small_ref[...]).T
    qpos = q0 + lax.broadcasted_iota(jnp.int32, (SEL_TILE, Q_BLOCK), 1)
    krow = lax.broadcasted_iota(jnp.int32, (SEL_TILE, Q_BLOCK), 0)
    wrow = lax.broadcasted_iota(jnp.int32, (WIN_TILES * Q_BLOCK, Q_BLOCK), 0)
    wdist = (q0 + lax.broadcasted_iota(jnp.int32, (WIN_TILES * Q_BLOCK, Q_BLOCK), 1)) - (q0 - WINDOW + wrow)
    wvalid = jnp.logical_and(jnp.logical_and(wdist >= 0, wdist <= WINDOW), q0 - WINDOW + wrow >= 0)
    wdistf = wdist.astype(F32)
    outs = []
    for k in range(A_KV_HEADS):
        qt = qt_ref[0, k]

        def tile_step(i, carry, k=k, qt=qt):
            m_prev, l_prev, acc = carry
            t = tiles_ref[(qb * A_KV_HEADS + k) * ntiles + i]
            st = jnp.dot(ks_ref[k, t], qt, preferred_element_type=F32)
            mrows = sel_ref[0, k, pl.ds(pl.multiple_of(t * SEL_TILE_BLOCKS, SEL_TILE_BLOCKS), SEL_TILE_BLOCKS), :]
            picked = jnp.concatenate(
                [jnp.broadcast_to(mrows[b:b + 1], (SLC_BLOCK, Q_BLOCK)) for b in range(SEL_TILE_BLOCKS)], axis=0)
            dist = qpos - (t * SEL_TILE + krow)
            valid = jnp.logical_and(picked > 0.5, dist >= 0)
            distf = dist.astype(F32)
            ms, ls, ps, alphas = [], [], [], []
            for g in range(A_GROUP):
                lanes = slice(g * Q_BLOCK, (g + 1) * Q_BLOCK)
                s = jnp.where(valid, st[:, lanes] - _head_slope(k * A_GROUP + g) * distf, NEG)
                m_new = jnp.maximum(m_prev[:, lanes], jnp.max(s, axis=0, keepdims=True))
                alpha = jnp.exp(m_prev[:, lanes] - m_new)
                p = jnp.where(valid, jnp.exp(s - m_new), 0.0)
                ms.append(m_new)
                ls.append(alpha * l_prev[:, lanes] + jnp.sum(p, axis=0, keepdims=True))
                alphas.append(alpha)
                ps.append(p.astype(BF16))
            pv = jnp.dot(vst_ref[k, t], jnp.concatenate(ps, axis=1), preferred_element_type=F32)
            return (jnp.concatenate(ms, axis=1), jnp.concatenate(ls, axis=1),
                    jnp.concatenate(alphas, axis=1) * acc + pv)

        init = (jnp.full((1, A_GROUP * Q_BLOCK), NEG, F32), jnp.zeros((1, A_GROUP * Q_BLOCK), F32),
                jnp.zeros((HEAD_DIM, A_GROUP * Q_BLOCK), F32))
        _, l_fin, acc = lax.fori_loop(0, counts_ref[qb * A_KV_HEADS + k], tile_step, init)
        os_t = acc / l_fin

        wk, wv = [], []
        for j in range(WIN_TILES):
            ti = jnp.maximum(qb - (WIN_TILES - 1) + j, 0)
            wk.append(kw_ref[k, ti])
            wv.append(vwt_ref[k, ti])
        sw = jnp.dot(jnp.concatenate(wk, axis=0), qt, preferred_element_type=F32)
        pw = []
        for g in range(A_GROUP):
            s = jnp.where(wvalid, sw[:, g * Q_BLOCK:(g + 1) * Q_BLOCK] - _head_slope(k * A_GROUP + g) * wdistf, NEG)
            e = jnp.exp(s - jnp.max(s, axis=0, keepdims=True))
            pw.append((e / jnp.sum(e, axis=0, keepdims=True)).astype(BF16))
        ow_t = jnp.dot(jnp.concatenate(wv, axis=1), jnp.concatenate(pw, axis=1), preferred_element_type=F32)

        oc_t = oct_ref[0, k]
        for g in range(A_GROUP):
            h = k * A_GROUP + g
            lanes = slice(g * Q_BLOCK, (g + 1) * Q_BLOCK)
            outs.append(gates_t[3 * h:3 * h + 1] * oc_t[:, lanes] + gates_t[3 * h + 1:3 * h + 2] * os_t[:, lanes]
                        + gates_t[3 * h + 2:3 * h + 3] * ow_t[:, lanes])
    o_ref[...] = jnp.concatenate(outs, axis=0).T


def _nsa_sel(tiles, counts, qt, oct, sel, small, ks, vst, kw, vwt):
    nqb = qt.shape[0]
    t = nqb * Q_BLOCK
    ntiles = ks.shape[1]
    nsb = sel.shape[2]
    shp = (A_KV_HEADS, HEAD_DIM, A_GROUP * Q_BLOCK)
    full = lambda a: pl.BlockSpec(a.shape, lambda i, *_: (0,) * a.ndim)
    grid_spec = pltpu.PrefetchScalarGridSpec(
        num_scalar_prefetch=2,
        grid=(nqb,),
        in_specs=[pl.BlockSpec((1,) + shp, lambda i, *_: (i, 0, 0, 0)),
                  pl.BlockSpec((1,) + shp, lambda i, *_: (i, 0, 0, 0)),
                  pl.BlockSpec((1, A_KV_HEADS, nsb, Q_BLOCK), lambda i, *_: (i, 0, 0, 0)),
                  pl.BlockSpec((Q_BLOCK, SMALL_W), lambda i, *_: (i, 0)),
                  full(ks), full(vst), full(kw), full(vwt)],
        out_specs=pl.BlockSpec((Q_BLOCK, A_WIDTH), lambda i, *_: (i, 0)),
    )
    return pl.pallas_call(
        functools.partial(_nsa_sel_kernel, ntiles=ntiles),
        grid_spec=grid_spec,
        out_shape=jax.ShapeDtypeStruct((t, A_WIDTH), F32),
        compiler_params=_cparams("arbitrary"),
        name="nsa_selected",
    )(tiles, counts, qt, oct, sel, small, ks, vst, kw, vwt)


def _cmp_weights(w_cmp, pe_cmp):
    eye_k = jnp.eye(A_KV_HEADS, dtype=F32)
    ws, pes = [], []
    for half in range(2):
        w = w_cmp[:, half * CMP_STRIDE:(half + 1) * CMP_STRIDE]
        full = jnp.einsum('sjde,ts,kl->jtkdsle', w, jnp.eye(N_KV_SLOTS, 2, dtype=F32), eye_k)
        ws.append(full.reshape(CMP_ROWS, CMP_COLS).astype(BF16))
        pe = pe_cmp[:, half * CMP_STRIDE:(half + 1) * CMP_STRIDE]
        pe = jnp.pad(pe, ((0, N_KV_SLOTS - 2), (0, 0), (0, 0)))
        pe = jnp.broadcast_to(pe.transpose(1, 0, 2)[:, :, None, :], (CMP_STRIDE, N_KV_SLOTS, A_KV_HEADS, HEAD_DIM))
        pes.append(jnp.broadcast_to(pe.reshape(1, CMP_ROWS), (16, CMP_ROWS)).astype(BF16))
    return ws[0], ws[1], pes[0], pes[1]


def _nsa_prompt(q_raw, kv_rows, win_rows, small, q_norm_g, k_norm_g, w_cmp, pe_cmp):
    t = q_raw.shape[0]
    w1, w2, pe1, pe2 = _cmp_weights(w_cmp, pe_cmp)
    kc, vct = _nsa_compress(kv_rows, w1, w2, pe1, pe2, k_norm_g[0].reshape(1, HEAD_DIM))
    ks, vst, kw, vwt = _nsa_rows(kv_rows, win_rows, k_norm_g[1].reshape(1, HEAD_DIM), k_norm_g[2].reshape(1, HEAD_DIM))
    qt, oct, sel, bany = _nsa_cmp(q_raw, q_norm_g.reshape(HEAD_DIM, 1), kc, vct)
    nqb, ntiles = t // Q_BLOCK, t // SEL_TILE
    flags = bany.reshape(nqb, A_KV_HEADS, ntiles, SEL_TILE_BLOCKS).max(-1) > 0.5
    order = jnp.argsort(jnp.logical_not(flags), axis=-1, stable=True).astype(jnp.int32)
    counts = flags.sum(-1).astype(jnp.int32)
    return _nsa_sel(order.reshape(-1), counts.reshape(-1), qt, oct, sel, small, ks, vst, kw, vwt)


DELTA_STEP = 2 * DELTA_CHUNK
CONV_HALO = 8
SM_A = 3 * A_HEADS
SM_B = SM_A + B_HEADS


def _softplus(x):
    return jnp.maximum(x, 0.0) + jnp.log(1.0 + jnp.exp(-jnp.abs(x)))


def _l2_rows(x):
    return x * lax.rsqrt(jnp.sum(x * x, axis=-1, keepdims=True) + EPS)


def _dot_nt3(a, b):
    hi = b.astype(BF16)
    lo = (b - hi.astype(F32)).astype(BF16)
    return lax.dot_general(_split3(a), jnp.concatenate([hi, hi, lo], axis=1), (((1,), (1,)), ((), ())),
                           preferred_element_type=F32)


def _delta_kernel(x_ref, small_ref, gate_ref, cw_ref, alog_ref, dtb_ref, on_ref, o_ref, s_out_ref, xbuf, s_ref):
    i = pl.program_id(0)
    n = DELTA_STEP

    @pl.when(i == 0)
    def _():
        xbuf[0:CONV_HALO, :] = jnp.zeros((CONV_HALO, CONV_CH), F32)
        s_ref[...] = jnp.zeros(s_ref.shape, F32)

    x = x_ref[...]
    xbuf[CONV_HALO:CONV_HALO + n, :] = x
    base = CONV_HALO - (CONV_W - 1)
    acc = xbuf[pl.ds(base, n), :] * cw_ref[0:1, :]
    for j in range(1, CONV_W):
        acc = acc + xbuf[pl.ds(base + j, n), :] * cw_ref[j:j + 1, :]
    c = _silu(acc)
    xbuf[0:CONV_HALO, :] = x[n - CONV_HALO:n, :]

    sm = small_ref[...]
    g_all = -jnp.exp(alog_ref[...]) * _softplus(sm + dtb_ref[...])
    beta_all = jax.nn.sigmoid(sm)
    r = lax.broadcasted_iota(jnp.int32, (n, n), 0)
    cc = lax.broadcasted_iota(jnp.int32, (n, n), 1)
    same = (r >= DELTA_CHUNK) == (cc >= DELTA_CHUNK)
    incl = jnp.logical_and(same, r >= cc)
    strict = jnp.logical_and(same, r > cc)
    gcum = jnp.dot(jnp.where(incl, 1.0, 0.0), g_all, preferred_element_type=F32, precision=HIGHEST)
    glast = jnp.dot(jnp.where(same, 1.0, 0.0), g_all, preferred_element_type=F32, precision=HIGHEST)
    gcum_t = gcum.T

    nk = B_HEADS * B_DK
    heads = range(B_HEADS)
    gc = [gcum[:, SM_A + h:SM_A + h + 1] for h in heads]
    gl = [glast[:, SM_A + h:SM_A + h + 1] for h in heads]
    decay = [jnp.exp(jnp.where(incl, gc[h] - gcum_t[SM_A + h:SM_A + h + 1, :], -jnp.inf)) for h in heads]
    bcol = [beta_all[:, SM_B + h:SM_B + h + 1] for h in heads]
    q = [_l2_rows(c[:, h * B_DK:(h + 1) * B_DK]) * B_DK ** -0.5 for h in heads]
    k = [_l2_rows(c[:, nk + h * B_DK:nk + (h + 1) * B_DK]) for h in heads]
    v = [c[:, 2 * nk + h * B_DV:2 * nk + (h + 1) * B_DV] for h in heads]
    pw = [-jnp.where(strict, _dot_nt3(k[h], k[h]) * decay[h] * bcol[h], 0.0) for h in heads]
    tm = list(pw)
    for _ in range(DELTA_CHUNK.bit_length() - 2):
        pw = [_mm3(pw[h], _stack3(pw[h])) for h in heads]
        tm = [tm[h] + pw[h] + _mm3(tm[h], _stack3(pw[h])) for h in heads]
    eg = [jnp.exp(gc[h]) for h in heads]
    rhs_u = [v[h] * bcol[h] for h in heads]
    rhs_w = [k[h] * bcol[h] * eg[h] for h in heads]
    solved = [_mm3(tm[h], _stack3(jnp.concatenate([rhs_u[h], rhs_w[h]], axis=1))) for h in heads]
    u = [rhs_u[h] + solved[h][:, :B_DV] for h in heads]
    w = [rhs_w[h] + solved[h][:, B_DV:] for h in heads]
    qk = [_dot_nt3(q[h], k[h]) * decay[h] for h in heads]
    q_dec = [q[h] * eg[h] for h in heads]
    kd_t = [(k[h] * jnp.exp(gl[h] - gc[h])).T for h in heads]
    s = [s_ref[h] for h in heads]
    outs = [[] for _ in heads]
    for ci in range(2):
        rows = slice(ci * DELTA_CHUNK, (ci + 1) * DELTA_CHUNK)
        s3 = [_stack3(s[h]) for h in heads]
        v_new = [u[h][rows] - _mm3(w[h][rows], s3[h]) for h in heads]
        for h in heads:
            outs[h].append(_mm3(q_dec[h][rows], s3[h]) + _mm(qk[h][rows, rows], v_new[h].astype(BF16), False))
        g_end = [jnp.exp(glast[ci * DELTA_CHUNK:ci * DELTA_CHUNK + 1, SM_A + h:SM_A + h + 1]) for h in heads]
        s = [s[h] * g_end[h] + _mm3(kd_t[h][:, rows], _stack3(v_new[h])) for h in heads]
    for h in heads:
        s_ref[h] = s[h]
        o = jnp.concatenate(outs[h], axis=0)
        o = _rms_rows(o) * on_ref[...] * _silu(gate_ref[:, h * B_DV:(h + 1) * B_DV])
        o_ref[:, h * B_DV:(h + 1) * B_DV] = o

    @pl.when(i == pl.num_programs(0) - 1)
    def _():
        s_out_ref[...] = s_ref[...]


def _lane_slab(vals, lane0):
    return jnp.pad(vals, (lane0, LANES - lane0 - vals.shape[0])).reshape(1, LANES)


def _delta_prompt(qkv_b, small, gate_b, conv_w, a_log, dt_bias, o_norm_g):
    t = qkv_b.shape[0]
    n = DELTA_STEP
    return pl.pallas_call(
        _delta_kernel,
        grid=(t // n,),
        in_specs=[pl.BlockSpec((n, CONV_CH), lambda i: (i, 0)),
                  pl.BlockSpec((n, SMALL_W), lambda i: (i, 0)),
                  pl.BlockSpec((n, B_WIDTH), lambda i: (i, 0)),
                  _const_spec((CONV_W, CONV_CH)), _const_spec((1, LANES)), _const_spec((1, LANES)),
                  _const_spec((1, B_DV))],
        out_specs=[pl.BlockSpec((n, B_WIDTH), lambda i: (i, 0)), _const_spec((B_HEADS, B_DK, B_DV))],
        out_shape=[jax.ShapeDtypeStruct((t, B_WIDTH), F32), jax.ShapeDtypeStruct((B_HEADS, B_DK, B_DV), F32)],
        scratch_shapes=[pltpu.VMEM((CONV_HALO + n, CONV_CH), F32), pltpu.VMEM((B_HEADS, B_DK, B_DV), F32)],
        compiler_params=_cparams("arbitrary"),
        name="delta_prompt",
    )(qkv_b, small, gate_b, conv_w, _lane_slab(a_log, SM_A), _lane_slab(dt_bias, SM_A), o_norm_g.reshape(1, B_DV))


PAGE_ROWS = 128
PAGES_PER_STEP = 32
STRIPES_PER_PAGE = PAGE_ROWS // CMP_STRIDE
HALF = A_KV_HEADS * HEAD_DIM
ROW_W = N_KV_SLOTS * HALF
SUB = 8
N_PICK = SLC_TOPN - 2


def _dot_nt_exact(a, b):
    return lax.dot_general(a, b, (((1,), (1,)), ((), ())), preferred_element_type=F32, precision=HIGHEST)


def _row_slopes(k):
    row = lax.broadcasted_iota(jnp.int32, (SUB, 1), 0)
    s = jnp.zeros((SUB, 1), F32)
    for g in range(A_GROUP):
        s = jnp.where(row == g, _head_slope(k * A_GROUP + g), s)
    return s


def _head_rows(x, k, gq):
    rows = [x[:, (k * A_GROUP + g) * HEAD_DIM:(k * A_GROUP + g + 1) * HEAD_DIM] for g in range(A_GROUP)]
    q = jnp.concatenate(rows + [jnp.zeros((SUB - A_GROUP, HEAD_DIM), F32)], axis=0)
    return _rms_rows(q) * gq * HEAD_DIM ** -0.5


def _nsa_s1_kernel(pt_ref, *refs, nb, past):
    del pt_ref
    npg = PAGES_PER_STEP
    kpages, vpages = refs[:npg], refs[npg:2 * npg]
    q_ref, gq_ref, gk_ref, w_ref, bias_ref = refs[2 * npg:2 * npg + 5]
    qn_ref, oc_ref, idx_ref = refs[2 * npg + 5:2 * npg + 8]
    f1_ref, f2_ref, rows_ref = refs[2 * npg + 8:]
    c = pl.program_id(1)
    nsb = nb // RATIO
    rows_step = npg * STRIPES_PER_PAGE

    @pl.when(jnp.logical_and(pl.program_id(0) == 0, c == 0))
    def _():
        for s in range(2):
            f2_ref[s, pl.ds(nb, SUB), :] = jnp.zeros((SUB, HALF), F32)

    row0 = pl.multiple_of(c * rows_step, rows_step)
    for s, pages in enumerate((kpages, vpages)):
        for i, p in enumerate(pages):
            rows_ref[i] = p[0].T
        pieces = []
        for j in range(CMP_STRIDE):
            x = jnp.concatenate([rows_ref[i, pl.ds(j, STRIPES_PER_PAGE, stride=CMP_STRIDE), :] for i in range(npg)],
                                axis=0)
            pieces.append(_split3(x))
        acc = jnp.dot(jnp.concatenate(pieces, axis=1), w_ref[s], preferred_element_type=F32)
        f1_ref[s, pl.ds(row0, rows_step), :] = acc[:, :HALF]
        f2_ref[s, pl.ds(row0, rows_step), :] = acc[:, HALF:]

    @pl.when(c == pl.num_programs(1) - 1)
    def _():
        lane = lax.broadcasted_iota(jnp.int32, (SUB, nb), 1)
        shift = nsb.bit_length() - 1
        cpos = (((lane & (nsb - 1)) * RATIO + (lane >> shift)) * CMP_STRIDE) + (CMP_BLOCK - 1)
        dist = past - cpos
        valid = dist >= 0
        distf = dist.astype(F32)
        row = lax.broadcasted_iota(jnp.int32, (SUB, nb), 0)
        kc = [[], []]
        vc = []
        for r in range(RATIO):
            kb = f1_ref[0, pl.ds(r, nsb, stride=RATIO), :] + f2_ref[0, pl.ds(r + 1, nsb, stride=RATIO), :] + bias_ref[0]
            vc.append(f1_ref[1, pl.ds(r, nsb, stride=RATIO), :] + f2_ref[1, pl.ds(r + 1, nsb, stride=RATIO), :]
                      + bias_ref[1])
            for k in range(A_KV_HEADS):
                kc[k].append(_rms_rows(kb[:, k * HEAD_DIM:(k + 1) * HEAD_DIM]) * gk_ref[...])
        vc = jnp.concatenate(vc, axis=0)
        qn, oc, imps = [], [], []
        for k in range(A_KV_HEADS):
            q = _head_rows(q_ref[0], k, gq_ref[...])
            qn.append(q)
            s = _dot_nt_exact(q, jnp.concatenate(kc[k], axis=0))
            s = jnp.where(valid, s - _row_slopes(k) * distf, NEG)
            e = jnp.exp(s - jnp.max(s, axis=-1, keepdims=True))
            p = e / jnp.sum(e, axis=-1, keepdims=True)
            p = jnp.where(jnp.logical_and(row < A_GROUP, past >= CMP_BLOCK - 1), p, 0.0)
            oc.append(_mm(p, vc[:, k * HEAD_DIM:(k + 1) * HEAD_DIM], True))
            pg = jnp.broadcast_to(jnp.sum(p, axis=0, keepdims=True), (SUB, nb))
            last = pg[:, 3 * nsb:4 * nsb]
            lane_b = lax.broadcasted_iota(jnp.int32, (SUB, nsb), 1)
            prev = jnp.where(lane_b == 0, 0.0, pltpu.roll(last, 1, axis=1))
            imps.append(pg[:, 0:nsb] + pg[:, nsb:2 * nsb] + pg[:, 2 * nsb:3 * nsb] + last + prev)
        qn_ref[0] = jnp.concatenate(qn, axis=1)
        oc_ref[0] = jnp.concatenate(oc, axis=1)
        lane_s = lax.broadcasted_iota(jnp.int32, (SUB, nsb), 1)
        row_s = lax.broadcasted_iota(jnp.int32, (SUB, nsb), 0)
        x = jnp.zeros((SUB, nsb), F32)
        for k in range(A_KV_HEADS):
            x = jnp.where(row_s == k, imps[k], x)
        x = jnp.where(lane_s == 0, -jnp.inf, x)
        out_lane = lax.broadcasted_iota(jnp.int32, (SUB, LANES), 1)
        picks = jnp.zeros((SUB, LANES), jnp.int32)
        for i in range(N_PICK):
            m = jnp.max(x, axis=-1, keepdims=True)
            first = jnp.min(jnp.where(x == m, lane_s, nsb), axis=-1, keepdims=True)
            picks = jnp.where(out_lane == i + 1, first, picks)
            x = jnp.where(lane_s == first, -jnp.inf, x)
        idx_ref[0] = picks


def _nsa_sample_cmp(q_raw, cache_t, page_table, gq, gk, wcat, bias, past):
    db = q_raw.shape[0]
    n_pages = page_table.shape[1]
    nb = past // CMP_STRIDE
    nch = n_pages // PAGES_PER_STEP

    def page_spec(p, slot):
        return pl.BlockSpec((1, HALF, PAGE_ROWS),
                            lambda b, c, pt: (pt[jnp.minimum(b, db - 1) * n_pages + c * PAGES_PER_STEP + p], slot, 0))

    cst = lambda shape: pl.BlockSpec(shape, lambda b, c, pt: (0,) * len(shape))
    per_b = lambda: pl.BlockSpec((1, SUB, LANES), lambda b, c, pt: (b, 0, 0))
    grid_spec = pltpu.PrefetchScalarGridSpec(
        num_scalar_prefetch=1,
        grid=(db, nch),
        in_specs=([page_spec(p, 0) for p in range(PAGES_PER_STEP)] + [page_spec(p, 1) for p in range(PAGES_PER_STEP)]
                  + [pl.BlockSpec((1, 1, A_WIDTH), lambda b, c, pt: (b, 0, 0)),
                     cst((1, HEAD_DIM)), cst((1, HEAD_DIM)), cst(wcat.shape), cst(bias.shape)]),
        out_specs=[per_b(), per_b(), per_b()],
        scratch_shapes=[pltpu.VMEM((2, nb + SUB, HALF), F32), pltpu.VMEM((2, nb + SUB, HALF), F32),
                        pltpu.VMEM((PAGES_PER_STEP, PAGE_ROWS, HALF), F32)],
    )
    return pl.pallas_call(
        functools.partial(_nsa_s1_kernel, nb=nb, past=past),
        grid_spec=grid_spec,
        out_shape=[jax.ShapeDtypeStruct((db, SUB, LANES), F32), jax.ShapeDtypeStruct((db, SUB, LANES), F32),
                   jax.ShapeDtypeStruct((db, SUB, LANES), jnp.int32)],
        compiler_params=_cparams("arbitrary", "arbitrary"),
        name="nsa_sample_compressed",
    )(page_table.reshape(-1), *([cache_t] * (2 * PAGES_PER_STEP)), q_raw.reshape(db, 1, A_WIDTH), gq, gk, wcat, bias)


def _nsa_s2_kernel(pt_ref, pick_ref, *refs, past):
    del pt_ref
    nsel = SLC_TOPN - 1
    blocks = refs[:A_KV_HEADS * nsel]
    (win_ref, kvn_ref, winn_ref, qn_ref, oc_ref, small_ref, gs_ref, gsc_ref, gw_ref) = refs[A_KV_HEADS * nsel:-1]
    o_ref = refs[-1]
    b = pl.program_id(0)
    slane = lax.broadcasted_iota(jnp.int32, (1, nsel * PAGE_ROWS), 1)
    spage = slane >> (PAGE_ROWS.bit_length() - 1)
    srow = slane & (PAGE_ROWS - 1)
    gates = jax.nn.sigmoid(small_ref[0])
    row = lax.broadcasted_iota(jnp.int32, (SUB, 1), 0)
    wlane = lax.broadcasted_iota(jnp.int32, (1, WINDOW), 1)
    wdist = (WINDOW - wlane).astype(F32)
    outs = []
    for k in range(A_KV_HEADS):
        lanes = slice(k * HEAD_DIM, (k + 1) * HEAD_DIM)
        q = qn_ref[0][:, lanes]
        slopes = _row_slopes(k)

        def finish(s, s_new, pv, v_new):
            m = jnp.maximum(jnp.max(s, axis=-1, keepdims=True), s_new)
            e, e_new = jnp.exp(s - m), jnp.exp(s_new - m)
            den = jnp.sum(e, axis=-1, keepdims=True) + e_new
            return (pv(e) + e_new * v_new) / den

        kts, vts = [], []
        for i in range(nsel):
            page = blocks[k * nsel + i][0]
            kt = page[lanes]
            kts.append(kt * lax.rsqrt(jnp.mean(kt * kt, axis=0, keepdims=True) + EPS) * gsc_ref[...])
            vts.append(page[HALF + k * HEAD_DIM:HALF + (k + 1) * HEAD_DIM])
        kt_all, vt_all = jnp.concatenate(kts, axis=1), jnp.concatenate(vts, axis=1)
        blk_id = jnp.zeros((1, nsel * PAGE_ROWS), jnp.int32)
        for i in range(nsel):
            blk_id = jnp.where(spage == i, pick_ref[(b * A_KV_HEADS + k) * nsel + i], blk_id)
        per_page = PAGE_ROWS // SLC_BLOCK
        in_block = (srow >> (SLC_BLOCK.bit_length() - 1)) == (blk_id & (per_page - 1))
        sdist = (past - (blk_id * SLC_BLOCK + (srow & (SLC_BLOCK - 1)))).astype(F32)
        kvn = kvn_ref[0]
        k_new = _rms_rows(kvn[:, 2 * HALF + k * HEAD_DIM:2 * HALF + (k + 1) * HEAD_DIM]) * gs_ref[...]
        v_new = kvn[:, 3 * HALF + k * HEAD_DIM:3 * HALF + (k + 1) * HEAD_DIM]
        s_sel = jnp.where(in_block, _mm(q, kt_all, True) - slopes * sdist, NEG)
        o_s = finish(s_sel, jnp.sum(q * k_new, axis=-1, keepdims=True),
                     lambda e: _dot_nt_exact(e, vt_all), v_new)

        win = win_ref[0]
        kw = _rms_rows(win[:, lanes]) * gw_ref[...]
        vw = win[:, HALF + k * HEAD_DIM:HALF + (k + 1) * HEAD_DIM]
        winn = winn_ref[0]
        kw_new = _rms_rows(winn[:, lanes]) * gw_ref[...]
        o_w = finish(_dot_nt_exact(q, kw) - slopes * wdist, jnp.sum(q * kw_new, axis=-1, keepdims=True),
                     lambda e: _mm(e, vw, True), winn[:, HALF + k * HEAD_DIM:HALF + (k + 1) * HEAD_DIM])

        gcol = [jnp.zeros((SUB, 1), F32) for _ in range(3)]
        for g in range(A_GROUP):
            h = k * A_GROUP + g
            for br in range(3):
                gcol[br] = jnp.where(row == g, gates[:, 3 * h + br:3 * h + br + 1], gcol[br])
        outs.append(gcol[0] * oc_ref[0][:, lanes] + gcol[1] * o_s + gcol[2] * o_w)
    o_ref[0] = jnp.concatenate(outs, axis=1)


def _nsa_sample_sel(cache_t, page_table, picks, cache_win, kv_new, win_new, qn, oc, small, gs, gw, past):
    db = qn.shape[0]
    n_pages = page_table.shape[1]
    nsel = SLC_TOPN - 1
    per_page = PAGE_ROWS // SLC_BLOCK

    def block_spec(k, i):
        def imap(b, pt, pk):
            bb = jnp.minimum(b, db - 1)
            j = jnp.clip(pk[(bb * A_KV_HEADS + k) * nsel + i], 0, n_pages * per_page - 1)
            return (pt[bb * n_pages + j // per_page], 1, 0)
        return pl.BlockSpec((1, 2 * HALF, PAGE_ROWS), imap)

    cst = lambda shape: pl.BlockSpec(shape, lambda b, pt, pk: (0,) * len(shape))
    per_b = lambda a: pl.BlockSpec((1,) + a.shape[1:], lambda b, pt, pk: (b,) + (0,) * (a.ndim - 1))
    grid_spec = pltpu.PrefetchScalarGridSpec(
        num_scalar_prefetch=2,
        grid=(db,),
        in_specs=([block_spec(k, i) for k in range(A_KV_HEADS) for i in range(nsel)]
                  + [per_b(cache_win), per_b(kv_new), per_b(win_new), per_b(qn), per_b(oc), per_b(small),
                     cst((1, HEAD_DIM)), cst((HEAD_DIM, 1)), cst((1, HEAD_DIM))]),
        out_specs=pl.BlockSpec((1, SUB, LANES), lambda b, pt, pk: (b, 0, 0)),
    )
    return pl.pallas_call(
        functools.partial(_nsa_s2_kernel, past=past),
        grid_spec=grid_spec,
        out_shape=jax.ShapeDtypeStruct((db, SUB, LANES), F32),
        compiler_params=_cparams("arbitrary"),
        name="nsa_sample_selected",
    )(page_table.reshape(-1), picks.reshape(-1), *([cache_t] * (A_KV_HEADS * nsel)),
      cache_win, kv_new, win_new, qn, oc, small, gs, gs.reshape(HEAD_DIM, 1), gw)


def _nsa_sample(q_raw, kv_new, win_new, small, cache_kv, page_table, cache_win, q_norm_g, k_norm_g, w_cmp, pe_cmp):
    db = q_raw.shape[0]
    n_phys = cache_kv.shape[0]
    past = page_table.shape[1] * PAGE_ROWS
    assert cache_kv.shape[1] == PAGE_ROWS and past % SLC_BLOCK == 0 and past >= WINDOW == cache_win.shape[1]
    eye_k = jnp.eye(A_KV_HEADS, dtype=F32)
    w = jnp.einsum('sjde,kl->sjkdle', w_cmp, eye_k).reshape(2, CMP_BLOCK, HALF, HALF)
    w = jnp.concatenate([w[:, :CMP_STRIDE], w[:, CMP_STRIDE:]], axis=-1)
    hi = w.astype(BF16)
    lo = (w - hi.astype(F32)).astype(BF16)
    wcat = jnp.concatenate([hi, hi, lo], axis=2).reshape(2, CMP_STRIDE * 3 * HALF, 2 * HALF)
    with jax.default_matmul_precision("highest"):
        bias = jnp.einsum('sjd,sjde->se', pe_cmp, w_cmp)
    bias = jnp.tile(bias[:, None, :], (1, 1, A_KV_HEADS))
    cache_t = cache_kv.reshape(n_phys, PAGE_ROWS, ROW_W).transpose(0, 2, 1)
    qn, oc, picks = _nsa_sample_cmp(q_raw, cache_t, page_table, q_norm_g.reshape(1, HEAD_DIM),
                                    k_norm_g[0].reshape(1, HEAD_DIM), wcat, bias, past)
    picks = picks[:, :A_KV_HEADS, :SLC_TOPN - 1]
    o = _nsa_sample_sel(cache_t, page_table, picks, cache_win.reshape(db, WINDOW, 2 * HALF),
                        kv_new.reshape(db, 1, ROW_W), win_new.reshape(db, 1, 2 * HALF), qn, oc,
                        small.reshape(db, 1, SMALL_W), k_norm_g[1].reshape(1, HEAD_DIM),
                        k_norm_g[2].reshape(1, HEAD_DIM), past)
    o = o[:, :A_GROUP].reshape(db, A_GROUP, A_KV_HEADS, HEAD_DIM).transpose(0, 2, 1, 3)
    return o.reshape(db, A_WIDTH)


def _delta_step_kernel(conv_ref, x_ref, small_ref, gate_ref, cw_ref, alog_ref, dtb_ref, on_ref, s_ref, o_ref, s_out_ref):
    acc = conv_ref[0][0:1, :] * cw_ref[0:1, :]
    for j in range(1, CONV_W - 1):
        acc = acc + conv_ref[0][j:j + 1, :] * cw_ref[j:j + 1, :]
    c = _silu(acc + x_ref[0] * cw_ref[CONV_W - 1:CONV_W, :])
    sm = small_ref[0]
    g_all = -jnp.exp(alog_ref[...]) * _softplus(sm + dtb_ref[...])
    beta_all = jax.nn.sigmoid(sm)
    nk = B_HEADS * B_DK

    def as_columns(row):
        return jnp.broadcast_to(row, (row.shape[1], row.shape[1])).T

    for h in range(B_HEADS):
        q = _l2_rows(c[:, h * B_DK:(h + 1) * B_DK]) * B_DK ** -0.5
        k = _l2_rows(c[:, nk + h * B_DK:nk + (h + 1) * B_DK])
        v = c[:, 2 * nk + h * B_DV:2 * nk + (h + 1) * B_DV]
        kc, qc = as_columns(k), as_columns(q)
        s = s_ref[0, h] * jnp.exp(g_all[:, SM_A + h:SM_A + h + 1])
        u = beta_all[:, SM_B + h:SM_B + h + 1] * (v - jnp.sum(kc * s, axis=0, keepdims=True))
        s = s + kc * u
        s_out_ref[0, h] = s
        o = jnp.sum(qc * s, axis=0, keepdims=True)
        o = _rms_rows(o) * on_ref[...] * _silu(gate_ref[0][:, h * B_DV:(h + 1) * B_DV])
        o_ref[0, :, h * B_DV:(h + 1) * B_DV] = o


def _delta_step(state_conv, qkv_new, small, gate_b, conv_w, a_log, dt_bias, o_norm_g, state):
    db = qkv_new.shape[0]
    per_b = lambda *s: pl.BlockSpec((1,) + s, lambda b: (b,) + (0,) * len(s))
    return pl.pallas_call(
        _delta_step_kernel,
        grid=(db,),
        in_specs=[per_b(CONV_W - 1, CONV_CH), per_b(1, CONV_CH), per_b(1, SMALL_W), per_b(1, B_WIDTH),
                  _const_spec((CONV_W, CONV_CH)), _const_spec((1, LANES)), _const_spec((1, LANES)),
                  _const_spec((1, B_DV)), per_b(B_HEADS, B_DK, B_DV)],
        out_specs=[per_b(1, B_WIDTH), per_b(B_HEADS, B_DK, B_DV)],
        out_shape=[jax.ShapeDtypeStruct((db, 1, B_WIDTH), F32),
                   jax.ShapeDtypeStruct((db, B_HEADS, B_DK, B_DV), F32)],
        compiler_params=_cparams("arbitrary"),
        name="delta_step",
    )(state_conv, qkv_new.reshape(db, 1, CONV_CH), small.reshape(db, 1, SMALL_W), gate_b.reshape(db, 1, B_WIDTH),
      conv_w, _lane_slab(a_log, SM_A), _lane_slab(dt_bias, SM_A), o_norm_g.reshape(1, B_DV), state)


def _rearranged_w_in(w_in):
    o = IN_OFFSETS
    q_a, kv_a, g_a, qkv_b, a_b, b_b, gate_b, merge = (
        w_in[:, :o[0]], w_in[:, o[0]:o[1]], w_in[:, o[1]:o[2]], w_in[:, o[2]:o[3]],
        w_in[:, o[3]:o[4]], w_in[:, o[4]:o[5]], w_in[:, o[5]:o[6]], w_in[:, o[6]:])
    small = jnp.concatenate([g_a, a_b, b_b], axis=1)
    small = jnp.pad(small, ((0, 0), (0, SMALL_W - small.shape[1])))
    return jnp.concatenate([q_a, kv_a, qkv_b, gate_b, merge, small], axis=1)


def kernel(x_prompt, x_sample, cache_nsa_kv, page_table, cache_win_kv, state_conv, state_delta,
           c_prompt, c_sample, norm1_g, norm2_g, w_ada, b_ada, w_in, q_norm_g, k_norm_g, w_cmp, pe_cmp,
           conv_w, a_log, dt_bias, o_norm_g, w_proj_a, w_proj_b, w_out, w_router, b_router,
           w_exp_gu, w_exp_down, w_sh_gu, w_sh_down):
    T = x_prompt.shape[1]
    DB = x_sample.shape[0]
    assert x_prompt.shape[0] == 1 and x_sample.shape[1] == 1 and w_in.shape[0] == 1

    w_main_f = _rearranged_w_in(w_in[0])
    w_main = w_main_f.astype(BF16)
    wa, wb, wout = w_proj_a[0].astype(BF16), w_proj_b[0].astype(BF16), w_out[0].astype(BF16)
    wr_f = jnp.pad(w_router[0], ((0, 0), (0, LANES - N_EXPERTS)))
    wr = wr_f.astype(BF16)
    br = jnp.pad(b_router[0], (0, LANES - N_EXPERTS)).reshape(1, LANES)
    wgu = w_exp_gu[0].astype(BF16)
    wd2 = w_exp_down[0].astype(BF16).reshape(N_EXPERTS // 2, 2 * D_EXPERT, D_MODEL)
    wsgu, wsd = w_sh_gu[0].astype(BF16), w_sh_down[0].astype(BF16)
    g1, g2 = norm1_g[0].reshape(1, -1), norm2_g[0].reshape(1, -1)

    c_all = jnp.concatenate([c_prompt, jnp.zeros((7, D_MODEL), F32), c_sample], axis=0)
    mod = _adaln(c_all, w_ada[0], b_ada[0])
    mp = [mod[0:1, i * D_MODEL:(i + 1) * D_MODEL] for i in range(6)]
    ms = [mod[8:8 + DB, i * D_MODEL:(i + 1) * D_MODEL] for i in range(6)]

    xp = x_prompt.reshape(T, D_MODEL)
    q_raw, kv, win, qkv_b, gate_b, msig, small = _in_proj(xp, g1, mp[1], mp[0], w_main, 256)
    o_a = _nsa_prompt(q_raw, kv, win, small, q_norm_g[0], k_norm_g[0], w_cmp[0], pe_cmp[0])
    o_b, s_fin_p = _delta_prompt(qkv_b, small, gate_b, conv_w[0], a_log[0], dt_bias[0], o_norm_g[0])
    x1, h2, scores = _merge(xp, o_a, o_b, msig, mp[2], g2, mp[4], mp[3], wa, wb, wout, wr, 256)
    y_prompt = _moe(h2, scores, br, x1, mp[5], wgu, wd2, wsgu, wsd, 1024).reshape(1, T, D_MODEL)
    kv_prompt = kv.reshape(1, 1, T, N_KV_SLOTS, A_KV_HEADS, HEAD_DIM)
    win_prompt = win[T - min(WINDOW, T):].reshape(1, 1, -1, 2, A_KV_HEADS, HEAD_DIM)
    conv_prompt = qkv_b[T - (CONV_W - 1):].reshape(1, 1, CONV_W - 1, CONV_CH)
    delta_prompt = s_fin_p.reshape(1, 1, B_HEADS, B_DK, B_DV)

    xs = x_sample.reshape(DB, D_MODEL)
    q_raw, kv, win, qkv_b, gate_b, msig, small = _in_proj(xs, g1, ms[1], ms[0], w_main_f, DB)
    o_a = _nsa_sample(q_raw, kv, win, small, cache_nsa_kv[0], page_table, cache_win_kv[0],
                      q_norm_g[0], k_norm_g[0], w_cmp[0], pe_cmp[0])
    o_b, s_fin_s = _delta_step(state_conv[0], qkv_b, small, gate_b, conv_w[0], a_log[0], dt_bias[0], o_norm_g[0],
                               state_delta[0])
    x1, h2, scores = _merge(xs, o_a, o_b.reshape(DB, B_WIDTH), msig, ms[2], g2, ms[4], ms[3],
                            w_proj_a[0], w_proj_b[0], w_out[0], wr_f, DB)
    y_sample = _moe(h2, scores, br, x1, ms[5], wgu, wd2, wsgu, wsd, DB).reshape(DB, 1, D_MODEL)
    kv_sample = kv.reshape(1, DB, 1, N_KV_SLOTS, A_KV_HEADS, HEAD_DIM)
    win_sample = jnp.concatenate([cache_win_kv[0][:, 1:], win.reshape(DB, 1, 2, A_KV_HEADS, HEAD_DIM)], axis=1)[None]
    conv_sample = jnp.concatenate([state_conv[0][:, 1:], qkv_b.reshape(DB, 1, CONV_CH)], axis=1)[None]

    return (y_prompt, y_sample, kv_prompt, win_prompt, conv_prompt, delta_prompt,
            kv_sample, win_sample, conv_sample, s_fin_s[None])
```
